```python
import math
import jax, jax.numpy as jnp
from jax import lax
import numpy as np

D_MODEL = 1024
BATCH = 8
SEQ = 2048
DEPTH = 2
DEC_BATCH = 128
DEC_SEQ = 8
PAST_LEN = 16384
PAGE_SIZE = 128

N_MIXERS = 2
N_A = (DEPTH + 1) // 2
N_B = DEPTH // 2
HG_DK = 128
HG_HEADS = D_MODEL // HG_DK
HG_DV = D_MODEL // HG_HEADS
ML_INNER = 2 * D_MODEL
ML_HEADS = 4
ML_DH = ML_INNER // ML_HEADS
ML_CONV = 4
ML_QKV_BLOCK = 4
MOE_GROUPS = 4
MOE_EPG = 4
MOE_EXPERTS = MOE_GROUPS * MOE_EPG
MOE_TOPK = 2
MOE_DFF = 256
CHUNK = 64
EPS = 1e-6

kernel_name = 'hgrn2_mlstm_hier_moe_adaln_step'

F32 = jnp.float32


def rmsnorm(x, g):
    x32 = x.astype(F32)
    y = x32 * lax.rsqrt(jnp.mean(x32 * x32, axis=-1, keepdims=True) + EPS)
    return (y * g.astype(F32)).astype(x.dtype)


def _to_chunks(a, L):
    B, T = a.shape[:2]
    a = a.reshape((B, T // L, L) + a.shape[2:])
    return a.transpose((1, 0, 3, 2) + tuple(range(4, a.ndim)))


def _from_chunks(a):
    nC, B, H, L = a.shape[:4]
    a = a.transpose((1, 0, 3, 2) + tuple(range(4, a.ndim)))
    return a.reshape((B, nC * L, H) + a.shape[4:])


def gla_chunk_scan(q, k, v, log_f, S0):
    T = q.shape[1]
    L = math.gcd(T, CHUNK)
    mask = jnp.tril(jnp.ones((L, L), dtype=bool))

    def step(S, inp):
        qc, kc, vc, gc = inp
        b = jnp.cumsum(gc, axis=2)
        diff = b[:, :, :, None, :] - b[:, :, None, :, :]
        decay = jnp.exp(jnp.where(mask[:, :, None], diff, -jnp.inf))
        A = jnp.einsum('bhtk,bhsk,bhtsk->bhts', qc, kc, decay)
        o = jnp.einsum('bhtk,bhkv->bhtv', qc * jnp.exp(b), S) + jnp.einsum('bhts,bhsv->bhtv', A, vc)
        bL = b[:, :, -1:, :]
        S_new = jnp.exp(bL[:, :, 0, :])[..., None] * S + jnp.einsum('bhsk,bhsv->bhkv', kc * jnp.exp(bL - b), vc)
        return S_new, o

    S_T, o = lax.scan(step, S0, (_to_chunks(q, L), _to_chunks(k, L), _to_chunks(v, L), _to_chunks(log_f, L)))
    return _from_chunks(o), S_T


def hgrn2_mixer(h, S0, w_in, w_o, g_norm, lb):
    B, T, _ = h.shape
    HK = HG_HEADS * HG_DK
    HV = HG_HEADS * HG_DV
    proj = h @ w_in
    zq = proj[..., :HK]
    zf = proj[..., HK:2 * HK]
    zi = proj[..., 2 * HK:2 * HK + HV]
    zg = proj[..., 2 * HK + HV:]
    lb32 = lb.astype(F32)
    f = lb32 + (1.0 - lb32) * jax.nn.sigmoid(zf.astype(F32))
    q = jax.nn.silu(zq.astype(F32)).reshape(B, T, HG_HEADS, HG_DK)
    k = (1.0 - f).reshape(B, T, HG_HEADS, HG_DK)
    log_f = jnp.log(f).reshape(B, T, HG_HEADS, HG_DK)
    v = zi.astype(F32).reshape(B, T, HG_HEADS, HG_DV)
    o, S_T = gla_chunk_scan(q, k, v, log_f, S0.astype(F32))
    o = o * lax.rsqrt(jnp.mean(o * o, axis=-1, keepdims=True) + EPS) * g_norm.astype(F32)
    o = o.reshape(B, T, HV) * jax.nn.silu(zg.astype(F32))
    return o.astype(h.dtype) @ w_o, S_T


def mlstm_chunk_scan(q, k, v, ig, lf, C0, n0, m0):
    T = q.shape[1]
    L = math.gcd(T, CHUNK)
    mask = jnp.tril(jnp.ones((L, L), dtype=bool))

    def step(carry, inp):
        C, n, m = carry
        qc, kc, vc, igc, lfc = inp
        b = jnp.cumsum(lfc, axis=-1)
        a = b + m[..., None]
        Dm = jnp.where(mask, b[..., :, None] - b[..., None, :] + igc[..., None, :], -jnp.inf)
        m_t = jnp.maximum(a, jnp.max(Dm, axis=-1))
        w_c = jnp.exp(a - m_t)
        W = jnp.exp(Dm - m_t[..., None])
        S = jnp.einsum('bhtk,bhsk->bhts', qc, kc) * W
        num = w_c[..., None] * jnp.einsum('bhtk,bhkv->bhtv', qc, C) + jnp.einsum('bhts,bhsv->bhtv', S, vc)
        den = w_c * jnp.einsum('bhtk,bhk->bht', qc, n) + jnp.sum(S, axis=-1)
        hc = num / jnp.maximum(jnp.abs(den), jnp.exp(-m_t))[..., None]
        mL = m_t[..., -1]
        wL_c = jnp.exp(a[..., -1] - mL)
        wL = jnp.exp(Dm[..., -1, :] - mL[..., None])
        C_new = wL_c[..., None, None] * C + jnp.einsum('bhs,bhsk,bhsv->bhkv', wL, kc, vc)
        n_new = wL_c[..., None] * n + jnp.einsum('bhs,bhsk->bhk', wL, kc)
        return (C_new, n_new, mL), hc

    (C_T, n_T, m_T), hs = lax.scan(step, (C0, n0, m0), (_to_chunks(q, L), _to_chunks(k, L), _to_chunks(v, L), _to_chunks(ig, L), _to_chunks(lf, L)))
    return _from_chunks(hs), C_T, n_T, m_T


def mlstm_mixer(h, C0, n0, m0, conv0, p, j):
    B, T, _ = h.shape
    up = h @ p['ml_wup'][j]
    xm = up[..., :ML_INNER]
    z = up[..., ML_INNER:]
    xpad = jnp.concatenate([conv0.astype(xm.dtype), xm], axis=1)
    conv_w = p['ml_conv_w'][j]
    xc = p['ml_conv_b'][j]
    for tap in range(ML_CONV):
        xc = xc + xpad[:, tap:tap + T] * conv_w[tap]
    xc = jax.nn.silu(xc)
    new_conv = xpad[:, -(ML_CONV - 1):]

    def headwise(u, w):
        return jnp.einsum('btni,noi->btno', u.reshape(B, T, ML_INNER // ML_QKV_BLOCK, ML_QKV_BLOCK), w).reshape(B, T, ML_INNER)

    q = headwise(xc, p['ml_wq'][j])
    k = headwise(xc, p['ml_wk'][j])
    v = headwise(xm, p['ml_wv'][j])
    gin = jnp.concatenate([q, k, v], axis=-1)
    ig = (gin @ p['ml_wig'][j] + p['ml_big'][j]).astype(F32)
    lf = jax.nn.log_sigmoid((gin @ p['ml_wfg'][j] + p['ml_bfg'][j]).astype(F32))

    def rs(u):
        return u.astype(F32).reshape(B, T, ML_HEADS, ML_DH)

    hc, C_T, n_T, m_T = mlstm_chunk_scan(rs(q), rs(k) * (ML_DH ** -0.5), rs(v), ig, lf,
                                         C0.astype(F32), n0.astype(F32), m0.astype(F32))
    mu = jnp.mean(hc, axis=-1, keepdims=True)
    var = jnp.mean((hc - mu) ** 2, axis=-1, keepdims=True)
    hn = ((hc - mu) * lax.rsqrt(var + EPS)).reshape(B, T, ML_INNER) * p['ml_norm'][j].astype(F32)
    ho = (hn + p['ml_skip'][j].astype(F32) * xc.astype(F32)) * jax.nn.silu(z.astype(F32))
    return ho.astype(h.dtype) @ p['ml_wdown'][j], C_T, n_T, m_T, new_conv


def hier_moe(h, w_rg, b_rg, w_re, b_re, w1, w3, w2):
    B, T, D = h.shape
    hf = h.reshape(B * T, D)
    g_logits = (hf @ w_rg + b_rg).astype(F32)
    p_group = jax.nn.softmax(g_logits, axis=-1)
    g_star = jnp.argmax(g_logits, axis=-1)
    p_star = jnp.max(p_group, axis=-1)
    e_logits = (jnp.einsum('nd,dge->nge', hf, w_re) + b_re).astype(F32)
    e_sel = jnp.einsum('nge,ng->ne', e_logits, jax.nn.one_hot(g_star, MOE_GROUPS, dtype=F32))
    top_v, top_i = lax.top_k(e_sel, MOE_TOPK)
    w_top = p_star[:, None] * jax.nn.softmax(top_v, axis=-1)
    eid = g_star[:, None] * MOE_EPG + top_i
    combine = jnp.sum(jax.nn.one_hot(eid, MOE_EXPERTS, dtype=F32) * w_top[..., None], axis=1)
    hid = jax.nn.silu(jnp.einsum('nd,edf->nef', hf, w1)) * jnp.einsum('nd,edf->nef', hf, w3)
    out = jnp.einsum('nef,efd->nd', hid * combine[..., None].astype(hid.dtype), w2)
    return out.reshape(B, T, D)


def trunk(x, c, s_hg, s_c, s_n, s_m, s_conv, p):
    lb_all = jnp.cumsum(jax.nn.softmax(p['hg_lb'].astype(F32), axis=0), axis=0)
    cmod = jax.nn.silu(c)
    new_hg, new_c, new_n, new_m, new_conv = [], [], [], [], []
    for l in range(DEPTH):
        mod = cmod @ p['w_ada'][l] + p['b_ada'][l]
        sh1, sc1, g1, sh2, sc2, g2 = [u[:, None, :] for u in jnp.split(mod, 6, axis=-1)]
        h = rmsnorm(x, p['norm_g'][l, 0]) * (1 + sc1) + sh1
        if l % N_MIXERS == 0:
            a = l // N_MIXERS
            mix, S = hgrn2_mixer(h, s_hg[a], p['hg_win'][a], p['hg_wo'][a], p['hg_norm'][a], lb_all[l])
            new_hg.append(S.astype(x.dtype))
        else:
            j = l // N_MIXERS
            mix, C, n, m, cv = mlstm_mixer(h, s_c[j], s_n[j], s_m[j], s_conv[j], p, j)
            new_c.append(C.astype(x.dtype))
            new_n.append(n.astype(x.dtype))
            new_m.append(m.astype(x.dtype))
            new_conv.append(cv.astype(x.dtype))
        x = x + g1 * mix
        h = rmsnorm(x, p['norm_g'][l, 1]) * (1 + sc2) + sh2
        x = x + g2 * hier_moe(h, p['moe_wrg'][l], p['moe_brg'][l], p['moe_wre'][l], p['moe_bre'][l],
                              p['moe_w1'][l], p['moe_w3'][l], p['moe_w2'][l])
    y = rmsnorm(x, p['final_g'])
    return y, jnp.stack(new_hg), jnp.stack(new_c), jnp.stack(new_n), jnp.stack(new_m), jnp.stack(new_conv)


def setup_inputs(seed: int = 0) -> dict:
    key = jax.random.key(seed)
    ks = iter(jax.random.split(key, 48))

    def nrm(shape, scale):
        return jax.random.normal(next(ks), shape, F32) * scale

    D = D_MODEL
    HK = HG_HEADS * HG_DK
    HV = HG_HEADS * HG_DV
    NBLK = ML_INNER // ML_QKV_BLOCK
    inp = {}
    inp['x_prompt'] = nrm((BATCH, SEQ, D), 1.0)
    inp['x_sample'] = nrm((DEC_BATCH, DEC_SEQ, D), 1.0)
    inp['c_prompt'] = nrm((BATCH, D), 1.0)
    inp['c_sample'] = nrm((DEC_BATCH, D), 1.0)
    inp['state_hgrn'] = nrm((N_A, DEC_BATCH, HG_HEADS, HG_DK, HG_DV), 0.1)
    inp['state_mlstm_c'] = nrm((N_B, DEC_BATCH, ML_HEADS, ML_DH, ML_DH), 0.05)
    inp['state_mlstm_n'] = nrm((N_B, DEC_BATCH, ML_HEADS, ML_DH), 0.05)
    inp['state_mlstm_m'] = jax.random.uniform(next(ks), (N_B, DEC_BATCH, ML_HEADS), F32, 0.0, 3.0)
    inp['state_conv'] = nrm((N_B, DEC_BATCH, ML_CONV - 1, ML_INNER), 1.0)
    inp['w_ada'] = nrm((DEPTH, D, 6 * D), 0.5 * D ** -0.5)
    inp['b_ada'] = nrm((DEPTH, 6 * D), 0.02)
    inp['norm_g'] = 1.0 + nrm((DEPTH, 2, D), 0.02)
    inp['final_g'] = 1.0 + nrm((D,), 0.02)
    inp['hg_win'] = nrm((N_A, D, 2 * HK + 2 * HV), D ** -0.5)
    inp['hg_wo'] = nrm((N_A, HV, D), HV ** -0.5)
    inp['hg_norm'] = 1.0 + nrm((N_A, HG_DV), 0.02)
    inp['hg_lb'] = 1.0 + nrm((DEPTH + 1, HK), 0.1)
    inp['ml_wup'] = nrm((N_B, D, 2 * ML_INNER), D ** -0.5)
    inp['ml_conv_w'] = nrm((N_B, ML_CONV, ML_INNER), ML_CONV ** -0.5)
    inp['ml_conv_b'] = nrm((N_B, ML_INNER), 0.02)
    inp['ml_wq'] = nrm((N_B, NBLK, ML_QKV_BLOCK, ML_QKV_BLOCK), ML_QKV_BLOCK ** -0.5)
    inp['ml_wk'] = nrm((N_B, NBLK, ML_QKV_BLOCK, ML_QKV_BLOCK), ML_QKV_BLOCK ** -0.5)
    inp['ml_wv'] = nrm((N_B, NBLK, ML_QKV_BLOCK, ML_QKV_BLOCK), ML_QKV_BLOCK ** -0.5)
    inp['ml_wig'] = nrm((N_B, 3 * ML_INNER, ML_HEADS), (3 * ML_INNER) ** -0.5)
    inp['ml_big'] = nrm((N_B, ML_HEADS), 0.1)
    inp['ml_wfg'] = nrm((N_B, 3 * ML_INNER, ML_HEADS), (3 * ML_INNER) ** -0.5)
    inp['ml_bfg'] = jnp.linspace(3.0, 6.0, ML_HEADS, dtype=F32)[None, :] + nrm((N_B, ML_HEADS), 0.1)
    inp['ml_norm'] = 1.0 + nrm((N_B, ML_INNER), 0.02)
    inp['ml_skip'] = 1.0 + nrm((N_B, ML_INNER), 0.02)
    inp['ml_wdown'] = nrm((N_B, ML_INNER, D), ML_INNER ** -0.5)
    inp['moe_wrg'] = nrm((DEPTH, D, MOE_GROUPS), D ** -0.5)
    inp['moe_brg'] = nrm((DEPTH, MOE_GROUPS), 0.01)
    inp['moe_wre'] = nrm((DEPTH, D, MOE_GROUPS, MOE_EPG), D ** -0.5)
    inp['moe_bre'] = nrm((DEPTH, MOE_GROUPS, MOE_EPG), 0.01)
    inp['moe_w1'] = nrm((DEPTH, MOE_EXPERTS, D, MOE_DFF), D ** -0.5)
    inp['moe_w3'] = nrm((DEPTH, MOE_EXPERTS, D, MOE_DFF), D ** -0.5)
    inp['moe_w2'] = nrm((DEPTH, MOE_EXPERTS, MOE_DFF, D), MOE_DFF ** -0.5)
    return inp


def reference(x_prompt, x_sample, c_prompt, c_sample, state_hgrn, state_mlstm_c, state_mlstm_n, state_mlstm_m, state_conv,
              w_ada, b_ada, norm_g, final_g, hg_win, hg_wo, hg_norm, hg_lb,
              ml_wup, ml_conv_w, ml_conv_b, ml_wq, ml_wk, ml_wv, ml_wig, ml_big, ml_wfg, ml_bfg, ml_norm, ml_skip, ml_wdown,
              moe_wrg, moe_brg, moe_wre, moe_bre, moe_w1, moe_w3, moe_w2):
    p = dict(w_ada=w_ada, b_ada=b_ada, norm_g=norm_g, final_g=final_g,
             hg_win=hg_win, hg_wo=hg_wo, hg_norm=hg_norm, hg_lb=hg_lb,
             ml_wup=ml_wup, ml_conv_w=ml_conv_w, ml_conv_b=ml_conv_b, ml_wq=ml_wq, ml_wk=ml_wk, ml_wv=ml_wv,
             ml_wig=ml_wig, ml_big=ml_big, ml_wfg=ml_wfg, ml_bfg=ml_bfg, ml_norm=ml_norm, ml_skip=ml_skip, ml_wdown=ml_wdown,
             moe_wrg=moe_wrg, moe_brg=moe_brg, moe_wre=moe_wre, moe_bre=moe_bre, moe_w1=moe_w1, moe_w3=moe_w3, moe_w2=moe_w2)
    Bp = x_prompt.shape[0]
    z_hg = jnp.zeros((N_A, Bp, HG_HEADS, HG_DK, HG_DV), F32)
    z_c = jnp.zeros((N_B, Bp, ML_HEADS, ML_DH, ML_DH), F32)
    z_n = jnp.zeros((N_B, Bp, ML_HEADS, ML_DH), F32)
    z_m = jnp.zeros((N_B, Bp, ML_HEADS), F32)
    z_conv = jnp.zeros((N_B, Bp, ML_CONV - 1, ML_INNER), x_prompt.dtype)
    y_prompt, hg_p, mc_p, mn_p, mm_p, conv_p = trunk(x_prompt, c_prompt, z_hg, z_c, z_n, z_m, z_conv, p)
    y_sample, hg_s, mc_s, mn_s, mm_s, conv_s = trunk(x_sample, c_sample, state_hgrn, state_mlstm_c, state_mlstm_n,
                                                     state_mlstm_m, state_conv, p)
    return (y_prompt, y_sample, hg_p, mc_p, mn_p, mm_p, conv_p, hg_s, mc_s, mn_s, mm_s, conv_s)
```

```python
import functools
import math

import numpy as np
import jax
import jax.numpy as jnp
from jax import lax
from jax.experimental import pallas as pl
from jax.experimental.pallas import tpu as pltpu

F32 = jnp.float32
BF16 = jnp.bfloat16
EPS = 1e-6

HG_DK = 128
ML_HEADS = 4
ML_CONV = 4
ML_QKV_BLOCK = 4
MOE_GROUPS = 4
MOE_EPG = 4
CHUNK = 64

LANES = 128
SUBLANES = 8
MXU_DIM = 256
VMEM_LIMIT_BYTES = 56 * 1024 * 1024


def _cparams(*sem):
    return pltpu.CompilerParams(dimension_semantics=sem, vmem_limit_bytes=VMEM_LIMIT_BYTES)


def _silu(x):
    return x * jax.nn.sigmoid(x)


def _dot(a, b):
    return jnp.dot(a, b, preferred_element_type=F32)


def _dot_nt(a, b):
    return lax.dot_general(a, b, (((1,), (1,)), ((), ())), preferred_element_type=F32)


def _dot_tn(a, b):
    return lax.dot_general(a, b, (((0,), (0,)), ((), ())), preferred_element_type=F32)


def _rms_mod(x, g, sc, sh):
    ms = jnp.mean(x * x, axis=-1, keepdims=True)
    h = x * lax.rsqrt(ms + EPS) * g
    return h * (1.0 + sc) + sh


def _ada_kernel(c_ref, w_ref, b_ref, o_ref):
    cm = _silu(c_ref[...]).astype(BF16)
    o_ref[0] = _dot(cm, w_ref[0].astype(BF16)) + b_ref[0]


def _ada_call(c_all, w_ada, b_ada):
    depth, d, n_out = w_ada.shape
    m = c_all.shape[0]
    tn = 512
    return pl.pallas_call(
        _ada_kernel,
        grid=(depth, n_out // tn),
        in_specs=[
            pl.BlockSpec((m, d), lambda l, j: (0, 0)),
            pl.BlockSpec((1, d, tn), lambda l, j: (l, 0, j)),
            pl.BlockSpec((1, 1, tn), lambda l, j: (l, 0, j)),
        ],
        out_specs=pl.BlockSpec((1, m, tn), lambda l, j: (l, 0, j)),
        out_shape=jax.ShapeDtypeStruct((depth, m, n_out), F32),
        compiler_params=_cparams("parallel", "parallel"),
        name="ada_mod",
    )(c_all, w_ada, b_ada.reshape(depth, 1, n_out))


def _mod_operand(m, t_len, tm):
    b, d = m.shape
    if t_len % tm == 0:
        per_b = t_len // tm
        return m.reshape(b, 1, d), pl.BlockSpec((1, 1, d), lambda i, *_: (i // per_b, 0, 0))
    assert tm % t_len == 0
    full = jnp.broadcast_to(m[:, None, :], (b, t_len, d)).reshape(b * t_len // tm, tm, d)
    return full, pl.BlockSpec((1, tm, d), lambda i, *_: (i, 0, 0))


def _row_tile(n, t_len, target):
    tm = min(target, n)
    while n % tm or (t_len % tm and tm % t_len):
        tm //= 2
    return tm


def _norm_mm_kernel(x_ref, g_ref, sc_ref, sh_ref, w_ref, o_ref, *, col_chunk):
    hb = _rms_mod(x_ref[...], g_ref[...], sc_ref[0], sh_ref[0]).astype(BF16)
    for c0 in range(0, o_ref.shape[1], col_chunk):
        o_ref[:, c0:c0 + col_chunk] = _dot(hb, w_ref[:, c0:c0 + col_chunk])


def _norm_mm_call(x, g, sc, sh, w, t_len, name):
    n, d = x.shape
    n_out = w.shape[1]
    tm = _row_tile(n, t_len, 512)
    sc_op, sc_spec = _mod_operand(sc, t_len, tm)
    sh_op, sh_spec = _mod_operand(sh, t_len, tm)
    return pl.pallas_call(
        functools.partial(_norm_mm_kernel, col_chunk=512),
        grid=(n // tm,),
        in_specs=[
            pl.BlockSpec((tm, d), lambda i: (i, 0)),
            pl.BlockSpec((1, d), lambda i: (0, 0)),
            sc_spec, sh_spec,
            pl.BlockSpec((d, n_out), lambda i: (0, 0)),
        ],
        out_specs=pl.BlockSpec((tm, n_out), lambda i: (i, 0)),
        out_shape=jax.ShapeDtypeStruct((n, n_out), F32),
        compiler_params=_cparams("parallel"),
        name=name,
    )(x, g.reshape(1, d), sc_op, sh_op, w)


def _mm_res_kernel(a_ref, w_ref, x_ref, gate_ref, o_ref):
    o_ref[...] = x_ref[...] + gate_ref[0] * _dot(a_ref[...], w_ref[...])


def _mm_res_call(a, w, x, gate, t_len, name):
    n, k = a.shape
    d = w.shape[1]
    tm = _row_tile(n, t_len, 512)
    gate_op, gate_spec = _mod_operand(gate, t_len, tm)
    return pl.pallas_call(
        _mm_res_kernel,
        grid=(n // tm,),
        in_specs=[
            pl.BlockSpec((tm, k), lambda i: (i, 0)),
            pl.BlockSpec((k, d), lambda i: (0, 0)),
            pl.BlockSpec((tm, d), lambda i: (i, 0)),
            gate_spec,
        ],
        out_specs=pl.BlockSpec((tm, d), lambda i: (i, 0)),
        out_shape=jax.ShapeDtypeStruct((n, d), F32),
        compiler_params=_cparams("parallel"),
        name=name,
    )(a, w, x, gate_op)


def _gla_tables(t_sub, nseq):
    r = t_sub * nseq
    levels = []
    m = t_sub // 2
    while m >= 1:
        levels.append(m)
        m //= 2
    n_lev = len(levels)
    wall = np.zeros((n_lev + 2, r, r), np.float32)
    mask = np.zeros((n_lev + 1, r, r), np.float32)
    for li, m in enumerate(levels):
        for row in range(r):
            blk = (row // (2 * m)) * 2 * m
            mid = blk + m - 1
            if row - blk >= m:
                wall[li, row, mid + 1:row + 1] = 1.0
                mask[li, row, blk:blk + m] = 1.0
            else:
                wall[li, row, row + 1:mid + 1] = 1.0
    for row in range(r):
        s0 = (row // t_sub) * t_sub
        wall[n_lev, row, s0:row + 1] = 1.0
        wall[n_lev + 1, row, row + 1:s0 + t_sub] = 1.0
        mask[n_lev, row, row] = 1.0
    return wall.reshape((n_lev + 2) * r, r), mask, n_lev


def _gla_kernel(*refs, t_sub, nseq, n_lev, n_heads, has_state):
    if has_state:
        proj_ref, lb_ref, gn_ref, wall_ref, mask_ref, s0_ref, o_ref, sout_ref, st_ref = refs
    else:
        proj_ref, lb_ref, gn_ref, wall_ref, mask_ref, o_ref, sout_ref, st_ref = refs
    r = t_sub * nseq
    dk = HG_DK
    hk = n_heads * dk
    c = pl.program_id(1)

    @pl.when(c == 0)
    def _():
        if has_state:
            for j in range(nseq):
                for h in range(n_heads):
                    st_ref[j, h] = s0_ref[j, h].T
        else:
            st_ref[...] = jnp.zeros_like(st_ref)

    zq = proj_ref[:, 0:hk]
    zf = proj_ref[:, hk:2 * hk]
    lb = lb_ref[...]
    f = lb + (1.0 - lb) * jax.nn.sigmoid(zf)
    lf = jnp.log(f)
    q = _silu(zq)
    k = 1.0 - f

    p0 = lf.astype(BF16)
    r1 = lf - p0.astype(F32)
    p1 = r1.astype(BF16)
    p2 = (r1 - p1.astype(F32)).astype(BF16)
    wall = wall_ref[...]
    decay = jnp.exp(_dot(wall, p0) + _dot(wall, p1) + _dot(wall, p2))
    e_cum = decay[n_lev * r:(n_lev + 1) * r]
    e_end = decay[(n_lev + 1) * r:(n_lev + 2) * r]

    gn = gn_ref[...]
    for h in range(n_heads):
        hs = slice(h * dk, (h + 1) * dk)
        qh = q[:, hs]
        kh = k[:, hs]
        vh = proj_ref[:, 2 * hk + h * dk:2 * hk + (h + 1) * dk]
        a = _dot_nt(qh.astype(BF16), kh.astype(BF16)) * mask_ref[n_lev]
        for li in range(n_lev):
            z = decay[li * r:(li + 1) * r, hs]
            a = a + _dot_nt((qh * z).astype(BF16), (kh * z).astype(BF16)) * mask_ref[li]
        o_intra = _dot(a.astype(BF16), vh.astype(BF16))
        qd = (qh * e_cum[:, hs]).astype(BF16)
        kd = (kh * e_end[:, hs]).astype(BF16)
        vb = vh.astype(BF16)
        o_parts = []
        for j in range(nseq):
            rs = slice(j * t_sub, (j + 1) * t_sub)
            st = st_ref[j, h]
            o_parts.append(_dot_nt(qd[rs], st.astype(BF16)))
            d_last = e_cum[(j + 1) * t_sub - 1:(j + 1) * t_sub, hs]
            st_ref[j, h] = st * d_last + _dot_tn(vb[rs], kd[rs])
        o_inter = o_parts[0] if nseq == 1 else jnp.concatenate(o_parts, axis=0)
        o = o_intra + o_inter
        o = o * lax.rsqrt(jnp.mean(o * o, axis=-1, keepdims=True) + EPS) * gn
        zg = proj_ref[:, 3 * hk + h * dk:3 * hk + (h + 1) * dk]
        o_ref[:, hs] = (o * _silu(zg)).astype(BF16)

    @pl.when(c == pl.num_programs(1) - 1)
    def _():
        for j in range(nseq):
            for h in range(n_heads):
                sout_ref[j, h] = st_ref[j, h].T


def _gla_call(proj, lb, gn, s0, b, t_len):
    n = proj.shape[0]
    hk = lb.shape[-1]
    n_heads = hk // HG_DK
    has_state = s0 is not None
    if t_len % CHUNK == 0:
        t_sub, nseq = CHUNK, 1
    else:
        t_sub, nseq = t_len, CHUNK // t_len
        assert t_sub * nseq == CHUNK and b % nseq == 0
    r = t_sub * nseq
    wall, mask, n_lev = _gla_tables(t_sub, nseq)
    n_outer = b // nseq
    n_inner = (t_len // t_sub)
    in_specs = [
        pl.BlockSpec((r, proj.shape[1]), lambda i, c: (i * n_inner + c, 0)),
        pl.BlockSpec((1, hk), lambda i, c: (0, 0)),
        pl.BlockSpec((1, HG_DK), lambda i, c: (0, 0)),
        pl.BlockSpec(wall.shape, lambda i, c: (0, 0)),
        pl.BlockSpec(mask.shape, lambda i, c: (0, 0, 0)),
    ]
    args = [proj, lb.reshape(1, hk), gn.reshape(1, HG_DK), jnp.asarray(wall, BF16), jnp.asarray(mask, F32)]
    s_spec = pl.BlockSpec((nseq, n_heads, HG_DK, HG_DK), lambda i, c: (i, 0, 0, 0))
    if has_state:
        in_specs.append(s_spec)
        args.append(s0)
    o, s_out = pl.pallas_call(
        functools.partial(_gla_kernel, t_sub=t_sub, nseq=nseq, n_lev=n_lev, n_heads=n_heads,
                          has_state=has_state),
        grid=(n_outer, n_inner),
        in_specs=in_specs,
        out_specs=[
            pl.BlockSpec((r, hk), lambda i, c: (i * n_inner + c, 0)),
            s_spec,
        ],
        out_shape=[
            jax.ShapeDtypeStruct((n, hk), BF16),
            jax.ShapeDtypeStruct((b, n_heads, HG_DK, HG_DK), F32),
        ],
        scratch_shapes=[pltpu.VMEM((nseq, n_heads, HG_DK, HG_DK), F32)],
        compiler_params=_cparams("parallel", "arbitrary"),
        name="gla_scan",
    )(*args)
    return o, s_out


def _moe_kernel(*refs, n_groups, epg, final):
    if final:
        (x_ref, g_ref, sc_ref, sh_ref, gate_ref, wr_ref, br_ref, w1_ref, w3_ref, w2_ref, fg_ref,
         o_ref, h_scr, comb_scr, acc_scr) = refs
    else:
        (x_ref, g_ref, sc_ref, sh_ref, gate_ref, wr_ref, br_ref, w1_ref, w3_ref, w2_ref,
         o_ref, h_scr, comb_scr, acc_scr) = refs
    e = pl.program_id(1)
    neg = -jnp.inf
    far = float(LANES)

    @pl.when(e == 0)
    def _():
        hb = _rms_mod(x_ref[...], g_ref[...], sc_ref[0], sh_ref[0]).astype(BF16)
        h_scr[...] = hb
        logit = _dot(hb, wr_ref[...]) + br_ref[...]
        lane = lax.broadcasted_iota(jnp.int32, logit.shape, 1).astype(F32)
        gm = lane < n_groups
        gmax = jnp.max(jnp.where(gm, logit, neg), axis=-1, keepdims=True)
        gstar = jnp.min(jnp.where(gm, jnp.where(logit == gmax, lane, far), far), axis=-1, keepdims=True)
        psum = jnp.sum(jnp.where(gm, jnp.exp(logit - gmax), 0.0), axis=-1, keepdims=True)
        pstar = 1.0 / psum
        lo = n_groups + gstar * epg
        em = jnp.where(lane >= lo, jnp.where(lane < lo + epg, 1.0, 0.0), 0.0)
        l1 = jnp.where(em > 0.0, logit, neg)
        v1 = jnp.max(l1, axis=-1, keepdims=True)
        i1 = jnp.min(jnp.where(l1 == v1, lane, far), axis=-1, keepdims=True)
        l2 = jnp.where(lane == i1, neg, l1)
        v2 = jnp.max(l2, axis=-1, keepdims=True)
        i2 = jnp.min(jnp.where(l2 == v2, lane, far), axis=-1, keepdims=True)
        e2 = jnp.exp(v2 - v1)
        wt1 = pstar / (1.0 + e2)
        wt2 = pstar * e2 / (1.0 + e2)
        comb_scr[...] = jnp.where(lane == i1, wt1, 0.0) + jnp.where(lane == i2, wt2, 0.0)
        acc_scr[...] = jnp.zeros_like(acc_scr)

    hb = h_scr[...]
    a = _dot(hb, w1_ref[0])
    b = _dot(hb, w3_ref[0])
    comb = comb_scr[...]
    lane = lax.broadcasted_iota(jnp.int32, comb.shape, 1)
    ce = jnp.sum(jnp.where(lane == e + n_groups, comb, 0.0), axis=-1, keepdims=True)
    hid = (_silu(a) * b * ce).astype(BF16)
    acc_scr[...] += _dot(hid, w2_ref[0])

    @pl.when(e == pl.num_programs(1) - 1)
    def _():
        y = x_ref[...] + gate_ref[0] * acc_scr[...]
        if final:
            y = y * lax.rsqrt(jnp.mean(y * y, axis=-1, keepdims=True) + EPS) * fg_ref[...]
        o_ref[...] = y


def _moe_call(x, g, sc, sh, gate, wr, br, w1, w3, w2, final_g, t_len, name):
    n, d = x.shape
    n_exp, _, dff = w1.shape
    tm = _row_tile(n, t_len, 1024 if t_len % 1024 == 0 else 256)
    sc_op, sc_spec = _mod_operand(sc, t_len, tm)
    sh_op, sh_spec = _mod_operand(sh, t_len, tm)
    gate_op, gate_spec = _mod_operand(gate, t_len, tm)
    final = final_g is not None
    in_specs = [
        pl.BlockSpec((tm, d), lambda i, e: (i, 0)),
        pl.BlockSpec((1, d), lambda i, e: (0, 0)),
        sc_spec, sh_spec, gate_spec,
        pl.BlockSpec((d, LANES), lambda i, e: (0, 0)),
        pl.BlockSpec((1, LANES), lambda i, e: (0, 0)),
        pl.BlockSpec((1, d, dff), lambda i, e: (e, 0, 0)),
        pl.BlockSpec((1, d, dff), lambda i, e: (e, 0, 0)),
        pl.BlockSpec((1, dff, d), lambda i, e: (e, 0, 0)),
    ]
    args = [x, g.reshape(1, d), sc_op, sh_op, gate_op, wr, br, w1, w3, w2]
    if final:
        in_specs.append(pl.BlockSpec((1, d), lambda i, e: (0, 0)))
        args.append(final_g.reshape(1, d))
    return pl.pallas_call(
        functools.partial(_moe_kernel, n_groups=MOE_GROUPS, epg=MOE_EPG, final=final),
        grid=(n // tm, n_exp),
        in_specs=in_specs,
        out_specs=pl.BlockSpec((tm, d), lambda i, e: (i, 0)),
        out_shape=jax.ShapeDtypeStruct((n, d), F32),
        scratch_shapes=[
            pltpu.VMEM((tm, d), BF16),
            pltpu.VMEM((tm, LANES), F32),
            pltpu.VMEM((tm, d), F32),
        ],
        compiler_params=_cparams("parallel", "arbitrary"),
        name=name,
    )(*args)


def _conv_taps(xm, halo, conv_w, conv_b, row_in_seq, axis):
    acc = conv_b + xm * conv_w[ML_CONV - 1]
    for s in range(1, ML_CONV):
        shifted = pltpu.roll(xm, s, axis)
        fill = pltpu.roll(halo, (s + SUBLANES - (ML_CONV - 1)) % SUBLANES, axis)
        if axis == 0:
            top = jnp.where(row_in_seq < s, fill, shifted[0:SUBLANES])
            shifted = jnp.concatenate([top, shifted[SUBLANES:]], axis=0)
        else:
            shifted = jnp.where(row_in_seq < s, fill, shifted)
        acc = acc + shifted * conv_w[ML_CONV - 1 - s]
    return acc


def _conv_qkv_kernel(xm_ref, c0_ref, cw_ref, cb_ref, wqk_ref, wv_ref, wg_ref, bg_ref,
                     q_ref, k_ref, v_ref, xc_ref, gates_ref, carry_scr, *, short_seq, k_scale):
    conv_w = [cw_ref[i:i + 1, :] for i in range(ML_CONV)]
    conv_b = cb_ref[...]
    if short_seq:
        xm3 = xm_ref[...]
        t_idx = lax.broadcasted_iota(jnp.int32, xm3.shape, 1)
        conv = _conv_taps(xm3, c0_ref[...], conv_w, conv_b, t_idx, 1)
        rows = xm3.shape[0] * xm3.shape[1]
        conv = conv.reshape(rows, xm3.shape[2])
        xm = xm3.reshape(rows, xm3.shape[2])
    else:
        t = pl.program_id(1)

        @pl.when(t == 0)
        def _():
            carry_scr[...] = c0_ref[0]

        xm = xm_ref[...]
        row8 = lax.broadcasted_iota(jnp.int32, (SUBLANES, xm.shape[1]), 0)
        conv = _conv_taps(xm, carry_scr[...], conv_w, conv_b, row8, 0)
        carry_scr[...] = pltpu.roll(xm[xm.shape[0] - SUBLANES:], ML_CONV - 1, 0)
    xc = _silu(conv)
    xc_ref[...] = xc
    inner = xm.shape[1]
    gates = bg_ref[...]
    xcb = xc.astype(BF16)
    xmb = xm.astype(BF16)
    for i in range(inner // MXU_DIM):
        cs = slice(i * MXU_DIM, (i + 1) * MXU_DIM)
        qk = _dot(xcb[:, cs], wqk_ref[i])
        qi = qk[:, :MXU_DIM]
        ki = qk[:, MXU_DIM:]
        vi = _dot(xmb[:, cs], wv_ref[i])
        q_ref[:, cs] = qi.astype(q_ref.dtype)
        k_ref[:, cs] = (ki * k_scale).astype(k_ref.dtype)
        v_ref[:, cs] = vi.astype(v_ref.dtype)
        gates = gates + _dot(qi.astype(BF16), wg_ref[0, cs, :])
        gates = gates + _dot(ki.astype(BF16), wg_ref[1, cs, :])
        gates = gates + _dot(vi.astype(BF16), wg_ref[2, cs, :])
    lane = lax.broadcasted_iota(jnp.int32, gates.shape, 1)
    log_sig = jnp.minimum(gates, 0.0) - jnp.log1p(jnp.exp(-jnp.abs(gates)))
    gates_ref[...] = jnp.where(lane < ML_HEADS, gates, log_sig)


def _conv_qkv_call(up, conv0, conv_w, conv_b, wqk, wv, wg, bg, b, t_len):
    n = up.shape[0]
    inner = up.shape[1] // 2
    short_seq = t_len == SUBLANES
    dh = inner // ML_HEADS
    k_scale = dh ** -0.5
    qkv_dtype = F32 if short_seq else BF16
    halo = jnp.zeros((b, SUBLANES, inner), F32)
    if conv0 is not None:
        halo = halo.at[:, :ML_CONV - 1].set(conv0)
    if short_seq:
        nb = min(32, b)
        grid = (b // nb,)
        up_op = up.reshape(b, t_len, 2 * inner)
        xm_spec = pl.BlockSpec((nb, t_len, inner), lambda i: (i, 0, 0))
        c0_spec = pl.BlockSpec((nb, SUBLANES, inner), lambda i: (i, 0, 0))
        rows = nb * t_len
        row_spec = lambda w: pl.BlockSpec((rows, w), lambda i: (i, 0))
        const2 = lambda s: pl.BlockSpec(s, lambda i: (0, 0))
        const3 = lambda s: pl.BlockSpec(s, lambda i: (0, 0, 0))
        sem = ("parallel",)
    else:
        tm = math.gcd(256, t_len)
        nt = t_len // tm
        grid = (b, nt)
        up_op = up
        xm_spec = pl.BlockSpec((tm, inner), lambda i, t: (i * nt + t, 0))
        c0_spec = pl.BlockSpec((1, SUBLANES, inner), lambda i, t: (i, 0, 0))
        rows = tm
        row_spec = lambda w: pl.BlockSpec((rows, w), lambda i, t: (i * nt + t, 0))
        const2 = lambda s: pl.BlockSpec(s, lambda i, t: (0, 0))
        const3 = lambda s: pl.BlockSpec(s, lambda i, t: (0, 0, 0))
        sem = ("parallel", "arbitrary")
    return pl.pallas_call(
        functools.partial(_conv_qkv_kernel, short_seq=short_seq, k_scale=k_scale),
        grid=grid,
        in_specs=[
            xm_spec, c0_spec,
            const2((ML_CONV, inner)), const2((1, inner)),
            const3(wqk.shape), const3(wv.shape), const3(wg.shape), const2((1, LANES)),
        ],
        out_specs=[row_spec(inner), row_spec(inner), row_spec(inner), row_spec(inner), row_spec(LANES)],
        out_shape=[
            jax.ShapeDtypeStruct((n, inner), qkv_dtype),
            jax.ShapeDtypeStruct((n, inner), qkv_dtype),
            jax.ShapeDtypeStruct((n, inner), qkv_dtype),
            jax.ShapeDtypeStruct((n, inner), F32),
            jax.ShapeDtypeStruct((n, LANES), F32),
        ],
        scratch_shapes=[pltpu.VMEM((SUBLANES, inner), F32)],
        compiler_params=_cparams(*sem),
        name="mlstm_conv_qkv",
    )(up_op, halo, conv_w, conv_b.reshape(1, inner), wqk, wv, wg, bg)


def _mlstm_kernel(*refs, has_state):
    if has_state:
        (q_ref, k_ref, v_ref, gates_ref, xc_ref, z_ref, nw_ref, sk_ref, c0_ref, n0_ref, m0_ref,
         o_ref, c_ref, n_ref, m_ref) = refs
    else:
        (q_ref, k_ref, v_ref, gates_ref, xc_ref, z_ref, nw_ref, sk_ref,
         o_ref, c_ref, n_ref, m_ref) = refs
    ch = pl.program_id(1)
    length = q_ref.shape[0]
    dh = q_ref.shape[1] // ML_HEADS

    @pl.when(ch == 0)
    def _():
        if has_state:
            c_ref[...] = c0_ref[...]
            n_ref[...] = n0_ref[...]
            m_ref[...] = m0_ref[...]
        else:
            c_ref[...] = jnp.zeros_like(c_ref)
            n_ref[...] = jnp.zeros_like(n_ref)
            m_ref[...] = jnp.zeros_like(m_ref)

    gates = gates_ref[...]
    gates_t = gates.T
    t_idx = lax.broadcasted_iota(jnp.int32, (length, length), 0)
    s_idx = lax.broadcasted_iota(jnp.int32, (length, length), 1)
    causal = s_idx <= t_idx
    m_all = m_ref[0]
    m_new = m_all
    lane = lax.broadcasted_iota(jnp.int32, m_all.shape, 1)
    for h in range(ML_HEADS):
        hs = slice(h * dh, (h + 1) * dh)
        ig_col = gates[:, h:h + 1]
        lf_col = gates[:, ML_HEADS + h:ML_HEADS + h + 1]
        ig_row = gates_t[h:h + 1, :]
        lf_row = gates_t[ML_HEADS + h:ML_HEADS + h + 1, :]
        b_col = jnp.sum(jnp.where(causal, lf_row, 0.0), axis=1, keepdims=True)
        b_row = jnp.sum(jnp.where(t_idx <= s_idx, lf_col, 0.0), axis=0, keepdims=True)
        m_prev = m_all[:, h:h + 1]
        dm = jnp.where(causal, b_col - b_row + ig_row, -jnp.inf)
        a = b_col + m_prev
        m_t = jnp.maximum(a, jnp.max(dm, axis=1, keepdims=True))
        w_c = jnp.exp(a - m_t)
        w = jnp.exp(dm - m_t)
        qh = q_ref[:, hs].astype(BF16)
        kh = k_ref[:, hs]
        vh = v_ref[:, hs].astype(BF16)
        c_h = c_ref[0, h]
        n_h = n_ref[0, :, hs]
        s = _dot_nt(qh, kh.astype(BF16)) * w
        num = w_c * _dot(qh, c_h.astype(BF16)) + _dot(s.astype(BF16), vh)
        den = w_c * jnp.sum(qh.astype(F32) * n_h, axis=-1, keepdims=True) + jnp.sum(s, axis=-1, keepdims=True)
        hc = num / jnp.maximum(jnp.abs(den), jnp.exp(-m_t))
        m_last = m_t[length - 1:length]
        b_last = b_col[length - 1:length]
        wl_c = jnp.exp(a[length - 1:length] - m_last)
        wl_col = jnp.exp(b_last - b_col + ig_col - m_last)
        kw = kh.astype(F32) * wl_col
        c_ref[0, h] = wl_c * c_h + _dot_tn(kw.astype(BF16), vh)
        n_ref[0, :, hs] = wl_c * n_h + jnp.sum(kw, axis=0, keepdims=True)
        m_new = jnp.where(lane == h, m_last, m_new)
        mu = jnp.mean(hc, axis=-1, keepdims=True)
        dev = hc - mu
        var = jnp.mean(dev * dev, axis=-1, keepdims=True)
        hn = dev * lax.rsqrt(var + EPS) * nw_ref[:, hs]
        ho = (hn + sk_ref[:, hs] * xc_ref[:, hs]) * _silu(z_ref[:, hs])
        o_ref[:, hs] = ho.astype(o_ref.dtype)
    m_ref[0] = m_new


def _mlstm_call(q, k, v, gates, xc, up, norm_w, skip, c0, n0, m0, b, t_len):
    n, inner = q.shape
    dh = inner // ML_HEADS
    has_state = c0 is not None
    length = math.gcd(t_len, CHUNK)
    nc = t_len // length
    row = lambda w, col=0: pl.BlockSpec((length, w), lambda i, c: (i * nc + c, col))
    const = lambda w: pl.BlockSpec((1, w), lambda i, c: (0, 0))
    c_spec = pl.BlockSpec((1, ML_HEADS, dh, dh), lambda i, c: (i, 0, 0, 0))
    n_spec = pl.BlockSpec((1, 1, inner), lambda i, c: (i, 0, 0))
    m_spec = pl.BlockSpec((1, 1, LANES), lambda i, c: (i, 0, 0))
    in_specs = [row(inner), row(inner), row(inner), row(LANES), row(inner), row(inner, 1),
                const(inner), const(inner)]
    args = [q, k, v, gates, xc, up, norm_w.reshape(1, inner), skip.reshape(1, inner)]
    if has_state:
        in_specs += [c_spec, n_spec, m_spec]
        m0_pad = jnp.zeros((b, 1, LANES), F32).at[:, 0, :ML_HEADS].set(m0)
        args += [c0, n0.reshape(b, 1, inner), m0_pad]
    ho, c_t, n_t, m_t = pl.pallas_call(
        functools.partial(_mlstm_kernel, has_state=has_state),
        grid=(b, nc),
        in_specs=in_specs,
        out_specs=[row(inner), c_spec, n_spec, m_spec],
        out_shape=[
            jax.ShapeDtypeStruct((n, inner), BF16 if length % 16 == 0 else F32),
            jax.ShapeDtypeStruct((b, ML_HEADS, dh, dh), F32),
            jax.ShapeDtypeStruct((b, 1, inner), F32),
            jax.ShapeDtypeStruct((b, 1, LANES), F32),
        ],
        compiler_params=_cparams("parallel", "arbitrary"),
        name="mlstm_scan",
    )(*args)
    return ho, c_t, n_t.reshape(b, ML_HEADS, dh), m_t[:, 0, :ML_HEADS]


def _block_diag_tiles(w):
    n_blk, blk, _ = w.shape
    per = MXU_DIM // blk
    wt = w.reshape(n_blk // per, per, blk, blk)
    eye = jnp.eye(per, dtype=w.dtype)
    dense = jnp.einsum('tnoi,nm->tnimo', wt, eye)
    return dense.reshape(n_blk // per, MXU_DIM, MXU_DIM)


def _prep_weights(p):
    w = {}
    w['hg_win'] = p['hg_win'].astype(BF16)
    w['hg_wo'] = p['hg_wo'].astype(BF16)
    w['ml_wup'] = p['ml_wup'].astype(BF16)
    w['ml_wdown'] = p['ml_wdown'].astype(BF16)
    w['moe_w1'] = p['moe_w1'].astype(BF16)
    w['moe_w3'] = p['moe_w3'].astype(BF16)
    w['moe_w2'] = p['moe_w2'].astype(BF16)
    depth, d, g = p['moe_wrg'].shape
    n_exp = g * p['moe_wre'].shape[-1]
    wr = jnp.zeros((depth, d, LANES), F32)
    wr = wr.at[:, :, :g].set(p['moe_wrg']).at[:, :, g:g + n_exp].set(p['moe_wre'].reshape(depth, d, n_exp))
    w['moe_wr'] = wr.astype(BF16)
    br = jnp.zeros((depth, 1, LANES), F32)
    br = br.at[:, 0, :g].set(p['moe_brg']).at[:, 0, g:g + n_exp].set(p['moe_bre'].reshape(depth, n_exp))
    w['moe_br'] = br
    n_b = p['ml_wq'].shape[0]
    wq = jnp.stack([_block_diag_tiles(p['ml_wq'][j]) for j in range(n_b)])
    wk = jnp.stack([_block_diag_tiles(p['ml_wk'][j]) for j in range(n_b)])
    wv = jnp.stack([_block_diag_tiles(p['ml_wv'][j]) for j in range(n_b)])
    w['ml_wqk'] = jnp.concatenate([wq, wk], axis=-1).astype(BF16)
    w['ml_wv'] = wv.astype(BF16)
    inner = p['ml_conv_b'].shape[-1]
    wg = jnp.zeros((n_b, 3 * inner, LANES), F32)
    wg = wg.at[:, :, :ML_HEADS].set(p['ml_wig']).at[:, :, ML_HEADS:2 * ML_HEADS].set(p['ml_wfg'])
    w['ml_wg'] = wg.reshape(n_b, 3, inner, LANES).astype(BF16)
    bg = jnp.zeros((n_b, 1, LANES), F32)
    bg = bg.at[:, 0, :ML_HEADS].set(p['ml_big']).at[:, 0, ML_HEADS:2 * ML_HEADS].set(p['ml_bfg'])
    w['ml_bg'] = bg
    return w


def _lb_kernel(lb_ref, o_ref):
    x = lb_ref[...]
    mx = jnp.max(x, axis=0, keepdims=True)
    ex = jnp.exp(x - mx)
    sm = ex / jnp.sum(ex, axis=0, keepdims=True)
    rows = []
    run = jnp.zeros_like(sm[0:1])
    for i in range(x.shape[0]):
        run = run + sm[i:i + 1]
        rows.append(run)
    o_ref[...] = jnp.concatenate(rows, axis=0)


def _lb_call(hg_lb):
    return pl.pallas_call(
        _lb_kernel,
        out_shape=jax.ShapeDtypeStruct(hg_lb.shape, F32),
        name="hgrn_lower_bound",
    )(hg_lb)


def _trunk(x3, mods, s_hg, s_c, s_n, s_m, s_conv, p, w, lb_all):
    b, t_len, d = x3.shape
    n = b * t_len
    x = x3.reshape(n, d)
    depth = p['norm_g'].shape[0]
    new_hg, new_c, new_n, new_m, new_conv = [], [], [], [], []
    for l in range(depth):
        sh1, sc1, g1, sh2, sc2, g2 = [mods[l][:, i * d:(i + 1) * d] for i in range(6)]
        if l % 2 == 0:
            a = l // 2
            proj = _norm_mm_call(x, p['norm_g'][l, 0], sc1, sh1, w['hg_win'][a], t_len, "hgrn_in_proj")
            o, s_t = _gla_call(proj, lb_all[l], p['hg_norm'][a], None if s_hg is None else s_hg[a], b, t_len)
            new_hg.append(s_t)
            x = _mm_res_call(o, w['hg_wo'][a], x, g1, t_len, "hgrn_out_proj")
        else:
            j = l // 2
            up = _norm_mm_call(x, p['norm_g'][l, 0], sc1, sh1, w['ml_wup'][j], t_len, "mlstm_up_proj")
            inner = up.shape[1] // 2
            q, k, v, xc, gates = _conv_qkv_call(
                up, None if s_conv is None else s_conv[j], p['ml_conv_w'][j], p['ml_conv_b'][j],
                w['ml_wqk'][j], w['ml_wv'][j], w['ml_wg'][j], w['ml_bg'][j], b, t_len)
            ho, c_t, n_t, m_t = _mlstm_call(
                q, k, v, gates, xc, up, p['ml_norm'][j], p['ml_skip'][j],
                None if s_c is None else s_c[j], None if s_n is None else s_n[j],
                None if s_m is None else s_m[j], b, t_len)
            new_c.append(c_t)
            new_n.append(n_t)
            new_m.append(m_t)
            new_conv.append(up.reshape(b, t_len, 2 * inner)[:, t_len - (ML_CONV - 1):, :inner])
            x = _mm_res_call(ho.astype(BF16), w['ml_wdown'][j], x, g1, t_len, "mlstm_down_proj")
        x = _moe_call(x, p['norm_g'][l, 1], sc2, sh2, g2, w['moe_wr'][l], w['moe_br'][l],
                      w['moe_w1'][l], w['moe_w3'][l], w['moe_w2'][l],
                      p['final_g'] if l == depth - 1 else None, t_len, "moe_layer%d" % l)
    return (x.reshape(b, t_len, d), jnp.stack(new_hg), jnp.stack(new_c), jnp.stack(new_n),
            jnp.stack(new_m), jnp.stack(new_conv))


def kernel(x_prompt, x_sample, c_prompt, c_sample, state_hgrn, state_mlstm_c, state_mlstm_n, state_mlstm_m, state_conv, w_ada, b_ada, norm_g, final_g, hg_win, hg_wo, hg_norm, hg_lb, ml_wup, ml_conv_w, ml_conv_b, ml_wq, ml_wk, ml_wv, ml_wig, ml_big, ml_wfg, ml_bfg, ml_norm, ml_skip, ml_wdown, moe_wrg, moe_brg, moe_wre, moe_bre, moe_w1, moe_w3, moe_w2):
    p = dict(w_ada=w_ada, b_ada=b_ada, norm_g=norm_g, final_g=final_g,
             hg_win=hg_win, hg_wo=hg_wo, hg_norm=hg_norm, hg_lb=hg_lb,
             ml_wup=ml_wup, ml_conv_w=ml_conv_w, ml_conv_b=ml_conv_b, ml_wq=ml_wq, ml_wk=ml_wk, ml_wv=ml_wv,
             ml_wig=ml_wig, ml_big=ml_big, ml_wfg=ml_wfg, ml_bfg=ml_bfg, ml_norm=ml_norm, ml_skip=ml_skip,
             ml_wdown=ml_wdown, moe_wrg=moe_wrg, moe_brg=moe_brg, moe_wre=moe_wre, moe_bre=moe_bre,
             moe_w1=moe_w1, moe_w3=moe_w3, moe_w2=moe_w2)
    w = _prep_weights(p)
    lb_all = _lb_call(hg_lb)
    bp = x_prompt.shape[0]
    c_all = jnp.concatenate([c_prompt, c_sample], axis=0)
    mod_all = _ada_call(c_all, w_ada, b_ada)
    mods_p = [mod_all[l, :bp] for l in range(mod_all.shape[0])]
    mods_s = [mod_all[l, bp:] for l in range(mod_all.shape[0])]
    y_p, hg_p, mc_p, mn_p, mm_p, conv_p = _trunk(x_prompt, mods_p, None, None, None, None, None, p, w, lb_all)
    y_s, hg_s, mc_s, mn_s, mm_s, conv_s = _trunk(x_sample, mods_s, state_hgrn, state_mlstm_c, state_mlstm_n,
                                                 state_mlstm_m, state_conv, p, w, lb_all)
    return (y_p, y_s, hg_p, mc_p, mn_p, mm_p, conv_p, hg_s, mc_s, mn_s, mm_s, conv_s)
```

```python
import functools
import math

import numpy as np
import jax
import jax.numpy as jnp
from jax import lax
from jax.experimental import pallas as pl
from jax.experimental.pallas import tpu as pltpu

F32 = jnp.float32
BF16 = jnp.bfloat16
EPS = 1e-6

HG_DK = 128
ML_HEADS = 4
ML_CONV = 4
ML_QKV_BLOCK = 4
MOE_GROUPS = 4
MOE_EPG = 4
CHUNK = 64
GLA_CHUNKS_PER_STEP = 4
ML_CHUNK = 256

LANES = 128
SUBLANES = 8
MXU_DIM = 256
VMEM_LIMIT_BYTES = 56 * 1024 * 1024


def _cparams(*sem):
    return pltpu.CompilerParams(dimension_semantics=sem, vmem_limit_bytes=VMEM_LIMIT_BYTES)


def _silu(x):
    return x * jax.nn.sigmoid(x)


def _dot(a, b):
    return jnp.dot(a, b, preferred_element_type=F32)


def _dot_nt(a, b):
    return lax.dot_general(a, b, (((1,), (1,)), ((), ())), preferred_element_type=F32)


def _dot_tn(a, b):
    return lax.dot_general(a, b, (((0,), (0,)), ((), ())), preferred_element_type=F32)


def _rms_mod(x, g, sc, sh):
    ms = jnp.mean(x * x, axis=-1, keepdims=True)
    h = x * lax.rsqrt(ms + EPS) * g
    return h * (1.0 + sc) + sh


def _ada_kernel(c_ref, w_ref, b_ref, o_ref):
    cm = _silu(c_ref[...]).astype(BF16)
    o_ref[0] = _dot(cm, w_ref[0].astype(BF16)) + b_ref[0]


def _ada_call(c_all, w_ada, b_ada):
    depth, d, n_out = w_ada.shape
    m = c_all.shape[0]
    tn = 512
    return pl.pallas_call(
        _ada_kernel,
        grid=(depth, n_out // tn),
        in_specs=[
            pl.BlockSpec((m, d), lambda l, j: (0, 0)),
            pl.BlockSpec((1, d, tn), lambda l, j: (l, 0, j)),
            pl.BlockSpec((1, 1, tn), lambda l, j: (l, 0, j)),
        ],
        out_specs=pl.BlockSpec((1, m, tn), lambda l, j: (l, 0, j)),
        out_shape=jax.ShapeDtypeStruct((depth, m, n_out), F32),
        compiler_params=_cparams("parallel", "parallel"),
        name="ada_mod",
    )(c_all, w_ada, b_ada.reshape(depth, 1, n_out))


def _mod_operand(m, t_len, tm):
    b, d = m.shape
    if t_len % tm == 0:
        per_b = t_len // tm
        return m.reshape(b, 1, d), pl.BlockSpec((1, 1, d), lambda i, *_: (i // per_b, 0, 0))
    assert tm % t_len == 0
    full = jnp.broadcast_to(m[:, None, :], (b, t_len, d)).reshape(b * t_len // tm, tm, d)
    return full, pl.BlockSpec((1, tm, d), lambda i, *_: (i, 0, 0))


def _row_tile(n, t_len, target):
    tm = min(target, n)
    while n % tm or (t_len % tm and tm % t_len):
        tm //= 2
    return tm


def _norm_mm_kernel(x_ref, g_ref, sc_ref, sh_ref, w_ref, o_ref, *, col_chunk):
    hb = _rms_mod(x_ref[...], g_ref[...], sc_ref[0], sh_ref[0]).astype(BF16)
    for c0 in range(0, o_ref.shape[1], col_chunk):
        o_ref[:, c0:c0 + col_chunk] = _dot(hb, w_ref[:, c0:c0 + col_chunk])


def _norm_mm_call(x, g, sc, sh, w, t_len, name):
    n, d = x.shape
    n_out = w.shape[1]
    tm = _row_tile(n, t_len, 512)
    sc_op, sc_spec = _mod_operand(sc, t_len, tm)
    sh_op, sh_spec = _mod_operand(sh, t_len, tm)
    return pl.pallas_call(
        functools.partial(_norm_mm_kernel, col_chunk=512),
        grid=(n // tm,),
        in_specs=[
            pl.BlockSpec((tm, d), lambda i: (i, 0)),
            pl.BlockSpec((1, d), lambda i: (0, 0)),
            sc_spec, sh_spec,
            pl.BlockSpec((d, n_out), lambda i: (0, 0)),
        ],
        out_specs=pl.BlockSpec((tm, n_out), lambda i: (i, 0)),
        out_shape=jax.ShapeDtypeStruct((n, n_out), F32),
        compiler_params=_cparams("parallel"),
        name=name,
    )(x, g.reshape(1, d), sc_op, sh_op, w)


def _mm_res_kernel(a_ref, w_ref, x_ref, gate_ref, o_ref):
    o_ref[...] = x_ref[...] + gate_ref[0] * _dot(a_ref[...], w_ref[...])


def _mm_res_call(a, w, x, gate, t_len, name):
    n, k = a.shape
    d = w.shape[1]
    tm = _row_tile(n, t_len, 512)
    gate_op, gate_spec = _mod_operand(gate, t_len, tm)
    return pl.pallas_call(
        _mm_res_kernel,
        grid=(n // tm,),
        in_specs=[
            pl.BlockSpec((tm, k), lambda i: (i, 0)),
            pl.BlockSpec((k, d), lambda i: (0, 0)),
            pl.BlockSpec((tm, d), lambda i: (i, 0)),
            gate_spec,
        ],
        out_specs=pl.BlockSpec((tm, d), lambda i: (i, 0)),
        out_shape=jax.ShapeDtypeStruct((n, d), F32),
        compiler_params=_cparams("parallel"),
        name=name,
    )(a, w, x, gate_op)


def _gla_tables(t_sub, nseq):
    r = t_sub * nseq
    levels = []
    m = t_sub // 2
    while m >= 1:
        levels.append(m)
        m //= 2
    n_lev = len(levels)
    tril = np.zeros((r, r), np.float32)
    mask = np.zeros((n_lev + 1, r, r), np.float32)
    for li, m in enumerate(levels):
        for row in range(r):
            blk = (row // (2 * m)) * 2 * m
            if row - blk >= m:
                mask[li, row, blk:blk + m] = 1.0
    for row in range(r):
        s0 = (row // t_sub) * t_sub
        tril[row, s0:row + 1] = 1.0
        mask[n_lev, row, row] = 1.0
    return tril, mask, levels


def _bcast_block_row(b, block, row_in_block):
    parts = [jnp.broadcast_to(b[s + row_in_block:s + row_in_block + 1, :], (block, b.shape[1]))
             for s in range(0, b.shape[0], block)]
    return parts[0] if len(parts) == 1 else jnp.concatenate(parts, axis=0)


def _level_decay(b, m):
    r = b.shape[0]
    pos = lax.broadcasted_iota(jnp.int32, b.shape, 0) & (2 * m - 1)
    if 2 * m >= SUBLANES:
        b_mid = _bcast_block_row(b, 2 * m, m - 1)
    else:
        b_mid = b
        for p in range(2 * m):
            if p != m - 1:
                b_mid = jnp.where(pos == p, pltpu.roll(b, (p - (m - 1)) % r, 0), b_mid)
    return jnp.exp(jnp.where(pos >= m, b - b_mid, b_mid - b))


def _gla_kernel(*refs, t_sub, nseq, n_ch, levels, n_heads, has_state):
    if has_state:
        proj_ref, lb_ref, gn_ref, tril_ref, mask_ref, s0_ref, o_ref, sout_ref, st_ref = refs
    else:
        proj_ref, lb_ref, gn_ref, tril_ref, mask_ref, o_ref, sout_ref, st_ref = refs
    r = t_sub * nseq
    n_lev = len(levels)
    dk = HG_DK
    hk = n_heads * dk
    c = pl.program_id(1)

    @pl.when(c == 0)
    def _():
        if has_state:
            for j in range(nseq):
                for h in range(n_heads):
                    st_ref[j, h] = s0_ref[j, h].T
        else:
            st_ref[...] = jnp.zeros_like(st_ref)

    lb = lb_ref[...]
    gn = gn_ref[...]
    tril = tril_ref[...]
    for ci in range(n_ch):
        rows = slice(ci * r, (ci + 1) * r)
        zq = proj_ref[rows, 0:hk]
        zf = proj_ref[rows, hk:2 * hk]
        f = lb + (1.0 - lb) * jax.nn.sigmoid(zf)
        lf = jnp.log(f)
        q = _silu(zq)
        k = 1.0 - f

        p0 = lf.astype(BF16)
        r1 = lf - p0.astype(F32)
        p1 = r1.astype(BF16)
        p2 = (r1 - p1.astype(F32)).astype(BF16)
        b = _dot(tril, p0) + _dot(tril, p1) + _dot(tril, p2)
        e_cum = jnp.exp(b)
        e_end = jnp.exp(_bcast_block_row(b, t_sub, t_sub - 1) - b)
        zs = [_level_decay(b, m) for m in levels]

        for h in range(n_heads):
            hs = slice(h * dk, (h + 1) * dk)
            qh = q[:, hs]
            kh = k[:, hs]
            vh = proj_ref[rows, 2 * hk + h * dk:2 * hk + (h + 1) * dk]
            a = _dot_nt(qh.astype(BF16), kh.astype(BF16)) * mask_ref[n_lev]
            for li in range(n_lev):
                z = zs[li][:, hs]
                a = a + _dot_nt((qh * z).astype(BF16), (kh * z).astype(BF16)) * mask_ref[li]
            vb = vh.astype(BF16)
            o_intra = _dot(a.astype(BF16), vb)
            qd = (qh * e_cum[:, hs]).astype(BF16)
            kd = (kh * e_end[:, hs]).astype(BF16)
            o_parts = []
            for j in range(nseq):
                rs = slice(j * t_sub, (j + 1) * t_sub)
                st = st_ref[j, h]
                o_parts.append(_dot_nt(qd[rs], st.astype(BF16)))
                d_last = e_cum[(j + 1) * t_sub - 1:(j + 1) * t_sub, hs]
                st_ref[j, h] = st * d_last + _dot_tn(vb[rs], kd[rs])
            o_inter = o_parts[0] if nseq == 1 else jnp.concatenate(o_parts, axis=0)
            o = o_intra + o_inter
            o = o * lax.rsqrt(jnp.mean(o * o, axis=-1, keepdims=True) + EPS) * gn
            zg = proj_ref[rows, 3 * hk + h * dk:3 * hk + (h + 1) * dk]
            o_ref[rows, hs] = (o * _silu(zg)).astype(BF16)

    @pl.when(c == pl.num_programs(1) - 1)
    def _():
        for j in range(nseq):
            for h in range(n_heads):
                sout_ref[j, h] = st_ref[j, h].T


def _gla_call(proj, lb, gn, s0, b, t_len):
    n = proj.shape[0]
    hk = lb.shape[-1]
    n_heads = hk // HG_DK
    has_state = s0 is not None
    if t_len % CHUNK == 0:
        t_sub, nseq = CHUNK, 1
    else:
        t_sub, nseq = t_len, CHUNK // t_len
        assert t_sub * nseq == CHUNK and b % nseq == 0
    r = t_sub * nseq
    tril, mask, levels = _gla_tables(t_sub, nseq)
    n_ch = math.gcd(GLA_CHUNKS_PER_STEP, t_len // t_sub)
    n_outer = b // nseq
    n_inner = t_len // (t_sub * n_ch)
    in_specs = [
        pl.BlockSpec((n_ch * r, proj.shape[1]), lambda i, c: (i * n_inner + c, 0)),
        pl.BlockSpec((1, hk), lambda i, c: (0, 0)),
        pl.BlockSpec((1, HG_DK), lambda i, c: (0, 0)),
        pl.BlockSpec(tril.shape, lambda i, c: (0, 0)),
        pl.BlockSpec(mask.shape, lambda i, c: (0, 0, 0)),
    ]
    args = [proj, lb.reshape(1, hk), gn.reshape(1, HG_DK), jnp.asarray(tril, BF16), jnp.asarray(mask, F32)]
    s_spec = pl.BlockSpec((nseq, n_heads, HG_DK, HG_DK), lambda i, c: (i, 0, 0, 0))
    if has_state:
        in_specs.append(s_spec)
        args.append(s0)
    o, s_out = pl.pallas_call(
        functools.partial(_gla_kernel, t_sub=t_sub, nseq=nseq, n_ch=n_ch, levels=tuple(levels),
                          n_heads=n_heads, has_state=has_state),
        grid=(n_outer, n_inner),
        in_specs=in_specs,
        out_specs=[
            pl.BlockSpec((n_ch * r, hk), lambda i, c: (i * n_inner + c, 0)),
            s_spec,
        ],
        out_shape=[
            jax.ShapeDtypeStruct((n, hk), BF16),
            jax.ShapeDtypeStruct((b, n_heads, HG_DK, HG_DK), F32),
        ],
        scratch_shapes=[pltpu.VMEM((nseq, n_heads, HG_DK, HG_DK), F32)],
        compiler_params=_cparams("parallel", "arbitrary"),
        name="gla_scan",
    )(*args)
    return o, s_out


def _moe_kernel(*refs, n_groups, epg, final):
    if final:
        (x_ref, g_ref, sc_ref, sh_ref, gate_ref, wr_ref, br_ref, w1_ref, w3_ref, w2_ref, fg_ref,
         o_ref, h_scr, comb_scr, acc_scr) = refs
    else:
        (x_ref, g_ref, sc_ref, sh_ref, gate_ref, wr_ref, br_ref, w1_ref, w3_ref, w2_ref,
         o_ref, h_scr, comb_scr, acc_scr) = refs
    e = pl.program_id(1)
    neg = -jnp.inf
    far = float(LANES)

    @pl.when(e == 0)
    def _():
        hb = _rms_mod(x_ref[...], g_ref[...], sc_ref[0], sh_ref[0]).astype(BF16)
        h_scr[...] = hb
        logit = _dot(hb, wr_ref[...]) + br_ref[...]
        lane = lax.broadcasted_iota(jnp.int32, logit.shape, 1).astype(F32)
        gm = lane < n_groups
        gmax = jnp.max(jnp.where(gm, logit, neg), axis=-1, keepdims=True)
        gstar = jnp.min(jnp.where(gm, jnp.where(logit == gmax, lane, far), far), axis=-1, keepdims=True)
        psum = jnp.sum(jnp.where(gm, jnp.exp(logit - gmax), 0.0), axis=-1, keepdims=True)
        pstar = 1.0 / psum
        lo = n_groups + gstar * epg
        em = jnp.where(lane >= lo, jnp.where(lane < lo + epg, 1.0, 0.0), 0.0)
        l1 = jnp.where(em > 0.0, logit, neg)
        v1 = jnp.max(l1, axis=-1, keepdims=True)
        i1 = jnp.min(jnp.where(l1 == v1, lane, far), axis=-1, keepdims=True)
        l2 = jnp.where(lane == i1, neg, l1)
        v2 = jnp.max(l2, axis=-1, keepdims=True)
        i2 = jnp.min(jnp.where(l2 == v2, lane, far), axis=-1, keepdims=True)
        e2 = jnp.exp(v2 - v1)
        wt1 = pstar / (1.0 + e2)
        wt2 = pstar * e2 / (1.0 + e2)
        comb_scr[...] = jnp.where(lane == i1, wt1, 0.0) + jnp.where(lane == i2, wt2, 0.0)
        acc_scr[...] = jnp.zeros_like(acc_scr)

    hb = h_scr[...]
    comb = comb_scr[...]
    lane = lax.broadcasted_iota(jnp.int32, comb.shape, 1)
    hids = []
    for j in range(epg):
        a = _dot(hb, w1_ref[0, j])
        b = _dot(hb, w3_ref[0, j])
        ce = jnp.sum(jnp.where(lane == n_groups + e * epg + j, comb, 0.0), axis=-1, keepdims=True)
        hids.append((_silu(a) * b * ce).astype(BF16))
    hid = jnp.concatenate(hids, axis=1)
    w2g = w2_ref[0].reshape(hid.shape[1], w2_ref.shape[-1])
    acc_scr[...] += _dot(hid, w2g)

    @pl.when(e == pl.num_programs(1) - 1)
    def _():
        y = x_ref[...] + gate_ref[0] * acc_scr[...]
        if final:
            y = y * lax.rsqrt(jnp.mean(y * y, axis=-1, keepdims=True) + EPS) * fg_ref[...]
        o_ref[...] = y


def _moe_call(x, g, sc, sh, gate, wr, br, w1, w3, w2, final_g, t_len, name):
    n, d = x.shape
    n_exp, _, dff = w1.shape
    tm = _row_tile(n, t_len, 1024 if t_len % 1024 == 0 else 256)
    sc_op, sc_spec = _mod_operand(sc, t_len, tm)
    sh_op, sh_spec = _mod_operand(sh, t_len, tm)
    gate_op, gate_spec = _mod_operand(gate, t_len, tm)
    final = final_g is not None
    in_specs = [
        pl.BlockSpec((tm, d), lambda i, e: (i, 0)),
        pl.BlockSpec((1, d), lambda i, e: (0, 0)),
        sc_spec, sh_spec, gate_spec,
        pl.BlockSpec((d, LANES), lambda i, e: (0, 0)),
        pl.BlockSpec((1, LANES), lambda i, e: (0, 0)),
        pl.BlockSpec((1, MOE_EPG, d, dff), lambda i, e: (e, 0, 0, 0)),
        pl.BlockSpec((1, MOE_EPG, d, dff), lambda i, e: (e, 0, 0, 0)),
        pl.BlockSpec((1, MOE_EPG, dff, d), lambda i, e: (e, 0, 0, 0)),
    ]
    n_grp = n_exp // MOE_EPG
    args = [x, g.reshape(1, d), sc_op, sh_op, gate_op, wr, br,
            w1.reshape(n_grp, MOE_EPG, d, dff), w3.reshape(n_grp, MOE_EPG, d, dff),
            w2.reshape(n_grp, MOE_EPG, dff, d)]
    if final:
        in_specs.append(pl.BlockSpec((1, d), lambda i, e: (0, 0)))
        args.append(final_g.reshape(1, d))
    return pl.pallas_call(
        functools.partial(_moe_kernel, n_groups=MOE_GROUPS, epg=MOE_EPG, final=final),
        grid=(n // tm, n_grp),
        in_specs=in_specs,
        out_specs=pl.BlockSpec((tm, d), lambda i, e: (i, 0)),
        out_shape=jax.ShapeDtypeStruct((n, d), F32),
        scratch_shapes=[
            pltpu.VMEM((tm, d), BF16),
            pltpu.VMEM((tm, LANES), F32),
            pltpu.VMEM((tm, d), F32),
        ],
        compiler_params=_cparams("parallel", "arbitrary"),
        name=name,
    )(*args)


def _conv_taps(xm, halo, conv_w, conv_b, row_in_seq, axis):
    acc = conv_b + xm * conv_w[ML_CONV - 1]
    for s in range(1, ML_CONV):
        shifted = pltpu.roll(xm, s, axis)
        fill = pltpu.roll(halo, (s + SUBLANES - (ML_CONV - 1)) % SUBLANES, axis)
        if axis == 0:
            top = jnp.where(row_in_seq < s, fill, shifted[0:SUBLANES])
            shifted = jnp.concatenate([top, shifted[SUBLANES:]], axis=0)
        else:
            shifted = jnp.where(row_in_seq < s, fill, shifted)
        acc = acc + shifted * conv_w[ML_CONV - 1 - s]
    return acc


def _conv_qkv_kernel(xm_ref, c0_ref, cw_ref, cb_ref, wqk_ref, wv_ref, wg_ref, bg_ref,
                     q_ref, k_ref, v_ref, xc_ref, gates_ref, carry_scr, *, short_seq, k_scale):
    conv_w = [cw_ref[i:i + 1, :] for i in range(ML_CONV)]
    conv_b = cb_ref[...]
    if short_seq:
        xm3 = xm_ref[...]
        t_idx = lax.broadcasted_iota(jnp.int32, xm3.shape, 1)
        conv = _conv_taps(xm3, c0_ref[...], conv_w, conv_b, t_idx, 1)
        rows = xm3.shape[0] * xm3.shape[1]
        conv = conv.reshape(rows, xm3.shape[2])
        xm = xm3.reshape(rows, xm3.shape[2])
    else:
        t = pl.program_id(1)

        @pl.when(t == 0)
        def _():
            carry_scr[...] = c0_ref[0]

        xm = xm_ref[...]
        row8 = lax.broadcasted_iota(jnp.int32, (SUBLANES, xm.shape[1]), 0)
        conv = _conv_taps(xm, carry_scr[...], conv_w, conv_b, row8, 0)
        carry_scr[...] = pltpu.roll(xm[xm.shape[0] - SUBLANES:], ML_CONV - 1, 0)
    xc = _silu(conv)
    xc_ref[...] = xc
    inner = xm.shape[1]
    gates = bg_ref[...]
    xcb = xc.astype(BF16)
    xmb = xm.astype(BF16)
    for i in range(inner // MXU_DIM):
        cs = slice(i * MXU_DIM, (i + 1) * MXU_DIM)
        qk = _dot(xcb[:, cs], wqk_ref[i])
        qi = qk[:, :MXU_DIM]
        ki = qk[:, MXU_DIM:]
        vi = _dot(xmb[:, cs], wv_ref[i])
        q_ref[:, cs] = qi.astype(q_ref.dtype)
        k_ref[:, cs] = (ki * k_scale).astype(k_ref.dtype)
        v_ref[:, cs] = vi.astype(v_ref.dtype)
        gates = gates + _dot(qi.astype(BF16), wg_ref[0, cs, :])
        gates = gates + _dot(ki.astype(BF16), wg_ref[1, cs, :])
        gates = gates + _dot(vi.astype(BF16), wg_ref[2, cs, :])
    lane = lax.broadcasted_iota(jnp.int32, gates.shape, 1)
    log_sig = jnp.minimum(gates, 0.0) - jnp.log1p(jnp.exp(-jnp.abs(gates)))
    gates_ref[...] = jnp.where(lane < ML_HEADS, gates, log_sig)


def _conv_qkv_call(up, conv0, conv_w, conv_b, wqk, wv, wg, bg, b, t_len):
    n = up.shape[0]
    inner = up.shape[1] // 2
    short_seq = t_len == SUBLANES
    dh = inner // ML_HEADS
    k_scale = dh ** -0.5
    qkv_dtype = F32 if short_seq else BF16
    halo = jnp.zeros((b, SUBLANES, inner), F32)
    if conv0 is not None:
        halo = halo.at[:, :ML_CONV - 1].set(conv0)
    if short_seq:
        nb = min(32, b)
        grid = (b // nb,)
        up_op = up.reshape(b, t_len, 2 * inner)
        xm_spec = pl.BlockSpec((nb, t_len, inner), lambda i: (i, 0, 0))
        c0_spec = pl.BlockSpec((nb, SUBLANES, inner), lambda i: (i, 0, 0))
        rows = nb * t_len
        row_spec = lambda w: pl.BlockSpec((rows, w), lambda i: (i, 0))
        const2 = lambda s: pl.BlockSpec(s, lambda i: (0, 0))
        const3 = lambda s: pl.BlockSpec(s, lambda i: (0, 0, 0))
        sem = ("parallel",)
    else:
        tm = math.gcd(256, t_len)
        nt = t_len // tm
        grid = (b, nt)
        up_op = up
        xm_spec = pl.BlockSpec((tm, inner), lambda i, t: (i * nt + t, 0))
        c0_spec = pl.BlockSpec((1, SUBLANES, inner), lambda i, t: (i, 0, 0))
        rows = tm
        row_spec = lambda w: pl.BlockSpec((rows, w), lambda i, t: (i * nt + t, 0))
        const2 = lambda s: pl.BlockSpec(s, lambda i, t: (0, 0))
        const3 = lambda s: pl.BlockSpec(s, lambda i, t: (0, 0, 0))
        sem = ("parallel", "arbitrary")
    return pl.pallas_call(
        functools.partial(_conv_qkv_kernel, short_seq=short_seq, k_scale=k_scale),
        grid=grid,
        in_specs=[
            xm_spec, c0_spec,
            const2((ML_CONV, inner)), const2((1, inner)),
            const3(wqk.shape), const3(wv.shape), const3(wg.shape), const2((1, LANES)),
        ],
        out_specs=[row_spec(inner), row_spec(inner), row_spec(inner), row_spec(inner), row_spec(LANES)],
        out_shape=[
            jax.ShapeDtypeStruct((n, inner), qkv_dtype),
            jax.ShapeDtypeStruct((n, inner), qkv_dtype),
            jax.ShapeDtypeStruct((n, inner), qkv_dtype),
            jax.ShapeDtypeStruct((n, inner), F32),
            jax.ShapeDtypeStruct((n, LANES), F32),
        ],
        scratch_shapes=[pltpu.VMEM((SUBLANES, inner), F32)],
        compiler_params=_cparams(*sem),
        name="mlstm_conv_qkv",
    )(up_op, halo, conv_w, conv_b.reshape(1, inner), wqk, wv, wg, bg)


def _mlstm_kernel(*refs, has_state):
    if has_state:
        (q_ref, k_ref, v_ref, gates_ref, xc_ref, z_ref, nw_ref, sk_ref, c0_ref, n0_ref, m0_ref,
         o_ref, c_ref, n_ref, m_ref) = refs
    else:
        (q_ref, k_ref, v_ref, gates_ref, xc_ref, z_ref, nw_ref, sk_ref,
         o_ref, c_ref, n_ref, m_ref) = refs
    ch = pl.program_id(1)
    length = q_ref.shape[0]
    dh = q_ref.shape[1] // ML_HEADS

    @pl.when(ch == 0)
    def _():
        if has_state:
            c_ref[...] = c0_ref[...]
            n_ref[...] = n0_ref[...]
            m_ref[...] = m0_ref[...]
        else:
            c_ref[...] = jnp.zeros_like(c_ref)
            n_ref[...] = jnp.zeros_like(n_ref)
            m_ref[...] = jnp.zeros_like(m_ref)

    gates = gates_ref[...]
    gates_t = gates.T
    t_idx = lax.broadcasted_iota(jnp.int32, (length, length), 0)
    s_idx = lax.broadcasted_iota(jnp.int32, (length, length), 1)
    causal = s_idx <= t_idx
    m_all = m_ref[0]
    m_new = m_all
    lane = lax.broadcasted_iota(jnp.int32, m_all.shape, 1)
    for h in range(ML_HEADS):
        hs = slice(h * dh, (h + 1) * dh)
        ig_col = gates[:, h:h + 1]
        lf_col = gates[:, ML_HEADS + h:ML_HEADS + h + 1]
        ig_row = gates_t[h:h + 1, :]
        lf_row = gates_t[ML_HEADS + h:ML_HEADS + h + 1, :]
        b_col = jnp.sum(jnp.where(causal, lf_row, 0.0), axis=1, keepdims=True)
        b_row = jnp.sum(jnp.where(t_idx <= s_idx, lf_col, 0.0), axis=0, keepdims=True)
        m_prev = m_all[:, h:h + 1]
        dm = jnp.where(causal, b_col - b_row + ig_row, -jnp.inf)
        a = b_col + m_prev
        m_t = jnp.maximum(a, jnp.max(dm, axis=1, keepdims=True))
        w_c = jnp.exp(a - m_t)
        w = jnp.exp(dm - m_t)
        qh = q_ref[:, hs].astype(BF16)
        kh = k_ref[:, hs]
        vh = v_ref[:, hs].astype(BF16)
        c_h = c_ref[0, h]
        n_h = n_ref[0, :, hs]
        s = _dot_nt(qh, kh.astype(BF16)) * w
        num = w_c * _dot(qh, c_h.astype(BF16)) + _dot(s.astype(BF16), vh)
        den = w_c * jnp.sum(qh.astype(F32) * n_h, axis=-1, keepdims=True) + jnp.sum(s, axis=-1, keepdims=True)
        hc = num / jnp.maximum(jnp.abs(den), jnp.exp(-m_t))
        m_last = m_t[length - 1:length]
        b_last = b_col[length - 1:length]
        wl_c = jnp.exp(a[length - 1:length] - m_last)
        wl_col = jnp.exp(b_last - b_col + ig_col - m_last)
        kw = kh.astype(F32) * wl_col
        c_ref[0, h] = wl_c * c_h + _dot_tn(kw.astype(BF16), vh)
        n_ref[0, :, hs] = wl_c * n_h + jnp.sum(kw, axis=0, keepdims=True)
        m_new = jnp.where(lane == h, m_last, m_new)
        mu = jnp.mean(hc, axis=-1, keepdims=True)
        dev = hc - mu
        var = jnp.mean(dev * dev, axis=-1, keepdims=True)
        hn = dev * lax.rsqrt(var + EPS) * nw_ref[:, hs]
        ho = (hn + sk_ref[:, hs] * xc_ref[:, hs]) * _silu(z_ref[:, hs])
        o_ref[:, hs] = ho.astype(o_ref.dtype)
    m_ref[0] = m_new


def _mlstm_call(q, k, v, gates, xc, up, norm_w, skip, c0, n0, m0, b, t_len):
    n, inner = q.shape
    dh = inner // ML_HEADS
    has_state = c0 is not None
    length = math.gcd(t_len, ML_CHUNK)
    nc = t_len // length
    row = lambda w, col=0: pl.BlockSpec((length, w), lambda i, c: (i * nc + c, col))
    const = lambda w: pl.BlockSpec((1, w), lambda i, c: (0, 0))
    c_spec = pl.BlockSpec((1, ML_HEADS, dh, dh), lambda i, c: (i, 0, 0, 0))
    n_spec = pl.BlockSpec((1, 1, inner), lambda i, c: (i, 0, 0))
    m_spec = pl.BlockSpec((1, 1, LANES), lambda i, c: (i, 0, 0))
    in_specs = [row(inner), row(inner), row(inner), row(LANES), row(inner), row(inner, 1),
                const(inner), const(inner)]
    args = [q, k, v, gates, xc, up, norm_w.reshape(1, inner), skip.reshape(1, inner)]
    if has_state:
        in_specs += [c_spec, n_spec, m_spec]
        m0_pad = jnp.zeros((b, 1, LANES), F32).at[:, 0, :ML_HEADS].set(m0)
        args += [c0, n0.reshape(b, 1, inner), m0_pad]
    ho, c_t, n_t, m_t = pl.pallas_call(
        functools.partial(_mlstm_kernel, has_state=has_state),
        grid=(b, nc),
        in_specs=in_specs,
        out_specs=[row(inner), c_spec, n_spec, m_spec],
        out_shape=[
            jax.ShapeDtypeStruct((n, inner), BF16 if length % 16 == 0 else F32),
            jax.ShapeDtypeStruct((b, ML_HEADS, dh, dh), F32),
            jax.ShapeDtypeStruct((b, 1, inner), F32),
            jax.ShapeDtypeStruct((b, 1, LANES), F32),
        ],
        compiler_params=_cparams("parallel", "arbitrary"),
        name="mlstm_scan",
    )(*args)
    return ho, c_t, n_t.reshape(b, ML_HEADS, dh), m_t[:, 0, :ML_HEADS]


def _block_diag_tiles(w):
    n_blk, blk, _ = w.shape
    per = MXU_DIM // blk
    wt = w.reshape(n_blk // per, per, blk, blk)
    eye = jnp.eye(per, dtype=w.dtype)
    dense = jnp.einsum('tnoi,nm->tnimo', wt, eye)
    return dense.reshape(n_blk // per, MXU_DIM, MXU_DIM)


def _prep_weights(p):
    w = {}
    w['hg_win'] = p['hg_win'].astype(BF16)
    w['hg_wo'] = p['hg_wo'].astype(BF16)
    w['ml_wup'] = p['ml_wup'].astype(BF16)
    w['ml_wdown'] = p['ml_wdown'].astype(BF16)
    w['moe_w1'] = p['moe_w1'].astype(BF16)
    w['moe_w3'] = p['moe_w3'].astype(BF16)
    w['moe_w2'] = p['moe_w2'].astype(BF16)
    depth, d, g = p['moe_wrg'].shape
    n_exp = g * p['moe_wre'].shape[-1]
    wr = jnp.zeros((depth, d, LANES), F32)
    wr = wr.at[:, :, :g].set(p['moe_wrg']).at[:, :, g:g + n_exp].set(p['moe_wre'].reshape(depth, d, n_exp))
    w['moe_wr'] = wr.astype(BF16)
    br = jnp.zeros((depth, 1, LANES), F32)
    br = br.at[:, 0, :g].set(p['moe_brg']).at[:, 0, g:g + n_exp].set(p['moe_bre'].reshape(depth, n_exp))
    w['moe_br'] = br
    n_b = p['ml_wq'].shape[0]
    wq = jnp.stack([_block_diag_tiles(p['ml_wq'][j]) for j in range(n_b)])
    wk = jnp.stack([_block_diag_tiles(p['ml_wk'][j]) for j in range(n_b)])
    wv = jnp.stack([_block_diag_tiles(p['ml_wv'][j]) for j in range(n_b)])
    w['ml_wqk'] = jnp.concatenate([wq, wk], axis=-1).astype(BF16)
    w['ml_wv'] = wv.astype(BF16)
    inner = p['ml_conv_b'].shape[-1]
    wg = jnp.zeros((n_b, 3 * inner, LANES), F32)
    wg = wg.at[:, :, :ML_HEADS].set(p['ml_wig']).at[:, :, ML_HEADS:2 * ML_HEADS].set(p['ml_wfg'])
    w['ml_wg'] = wg.reshape(n_b, 3, inner, LANES).astype(BF16)
    bg = jnp.zeros((n_b, 1, LANES), F32)
    bg = bg.at[:, 0, :ML_HEADS].set(p['ml_big']).at[:, 0, ML_HEADS:2 * ML_HEADS].set(p['ml_bfg'])
    w['ml_bg'] = bg
    return w


def _lb_kernel(lb_ref, o_ref):
    x = lb_ref[...]
    mx = jnp.max(x, axis=0, keepdims=True)
    ex = jnp.exp(x - mx)
    sm = ex / jnp.sum(ex, axis=0, keepdims=True)
    rows = []
    run = jnp.zeros_like(sm[0:1])
    for i in range(x.shape[0]):
        run = run + sm[i:i + 1]
        rows.append(run)
    o_ref[...] = jnp.concatenate(rows, axis=0)


def _lb_call(hg_lb):
    return pl.pallas_call(
        _lb_kernel,
        out_shape=jax.ShapeDtypeStruct(hg_lb.shape, F32),
        name="hgrn_lower_bound",
    )(hg_lb)


def _trunk(x3, mods, s_hg, s_c, s_n, s_m, s_conv, p, w, lb_all):
    b, t_len, d = x3.shape
    n = b * t_len
    x = x3.reshape(n, d)
    depth = p['norm_g'].shape[0]
    new_hg, new_c, new_n, new_m, new_conv = [], [], [], [], []
    for l in range(depth):
        sh1, sc1, g1, sh2, sc2, g2 = [mods[l][:, i * d:(i + 1) * d] for i in range(6)]
        if l % 2 == 0:
            a = l // 2
            proj = _norm_mm_call(x, p['norm_g'][l, 0], sc1, sh1, w['hg_win'][a], t_len, "hgrn_in_proj")
            o, s_t = _gla_call(proj, lb_all[l], p['hg_norm'][a], None if s_hg is None else s_hg[a], b, t_len)
            new_hg.append(s_t)
            x = _mm_res_call(o, w['hg_wo'][a], x, g1, t_len, "hgrn_out_proj")
        else:
            j = l // 2
            up = _norm_mm_call(x, p['norm_g'][l, 0], sc1, sh1, w['ml_wup'][j], t_len, "mlstm_up_proj")
            inner = up.shape[1] // 2
            q, k, v, xc, gates = _conv_qkv_call(
                up, None if s_conv is None else s_conv[j], p['ml_conv_w'][j], p['ml_conv_b'][j],
                w['ml_wqk'][j], w['ml_wv'][j], w['ml_wg'][j], w['ml_bg'][j], b, t_len)
            ho, c_t, n_t, m_t = _mlstm_call(
                q, k, v, gates, xc, up, p['ml_norm'][j], p['ml_skip'][j],
                None if s_c is None else s_c[j], None if s_n is None else s_n[j],
                None if s_m is None else s_m[j], b, t_len)
            new_c.append(c_t)
            new_n.append(n_t)
            new_m.append(m_t)
            new_conv.append(up.reshape(b, t_len, 2 * inner)[:, t_len - (ML_CONV - 1):, :inner])
            x = _mm_res_call(ho.astype(BF16), w['ml_wdown'][j], x, g1, t_len, "mlstm_down_proj")
        x = _moe_call(x, p['norm_g'][l, 1], sc2, sh2, g2, w['moe_wr'][l], w['moe_br'][l],
                      w['moe_w1'][l], w['moe_w3'][l], w['moe_w2'][l],
                      p['final_g'] if l == depth - 1 else None, t_len, "moe_layer%d" % l)
    return (x.reshape(b, t_len, d), jnp.stack(new_hg), jnp.stack(new_c), jnp.stack(new_n),
            jnp.stack(new_m), jnp.stack(new_conv))


def kernel(x_prompt, x_sample, c_prompt, c_sample, state_hgrn, state_mlstm_c, state_mlstm_n, state_mlstm_m, state_conv, w_ada, b_ada, norm_g, final_g, hg_win, hg_wo, hg_norm, hg_lb, ml_wup, ml_conv_w, ml_conv_b, ml_wq, ml_wk, ml_wv, ml_wig, ml_big, ml_wfg, ml_bfg, ml_norm, ml_skip, ml_wdown, moe_wrg, moe_brg, moe_wre, moe_bre, moe_w1, moe_w3, moe_w2):
    p = dict(w_ada=w_ada, b_ada=b_ada, norm_g=norm_g, final_g=final_g,
             hg_win=hg_win, hg_wo=hg_wo, hg_norm=hg_norm, hg_lb=hg_lb,
             ml_wup=ml_wup, ml_conv_w=ml_conv_w, ml_conv_b=ml_conv_b, ml_wq=ml_wq, ml_wk=ml_wk, ml_wv=ml_wv,
             ml_wig=ml_wig, ml_big=ml_big, ml_wfg=ml_wfg, ml_bfg=ml_bfg, ml_norm=ml_norm, ml_skip=ml_skip,
             ml_wdown=ml_wdown, moe_wrg=moe_wrg, moe_brg=moe_brg, moe_wre=moe_wre, moe_bre=moe_bre,
             moe_w1=moe_w1, moe_w3=moe_w3, moe_w2=moe_w2)
    w = _prep_weights(p)
    lb_all = _lb_call(hg_lb)
    bp = x_prompt.shape[0]
    c_all = jnp.concatenate([c_prompt, c_sample], axis=0)
    mod_all = _ada_call(c_all, w_ada, b_ada)
    mods_p = [mod_all[l, :bp] for l in range(mod_all.shape[0])]
    mods_s = [mod_all[l, bp:] for l in range(mod_all.shape[0])]
    y_p, hg_p, mc_p, mn_p, mm_p, conv_p = _trunk(x_prompt, mods_p, None, None, None, None, None, p, w, lb_all)
    y_s, hg_s, mc_s, mn_s, mm_s, conv_s = _trunk(x_sample, mods_s, state_hgrn, state_mlstm_c, state_mlstm_n,
                                                 state_mlstm_m, state_conv, p, w, lb_all)
    return (y_p, y_s, hg_p, mc_p, mn_p, mm_p, conv_p, hg_s, mc_s, mn_s, mm_s, conv_s)
```

```python
import functools
import math

import numpy as np
import jax
import jax.numpy as jnp
from jax import lax
from jax.experimental import pallas as pl
from jax.experimental.pallas import tpu as pltpu

F32 = jnp.float32
BF16 = jnp.bfloat16
EPS = 1e-6

HG_DK = 128
ML_HEADS = 4
ML_CONV = 4
ML_QKV_BLOCK = 4
MOE_GROUPS = 4
MOE_EPG = 4
CHUNK = 64
GLA_CHUNKS_PER_STEP = 4
ML_CHUNK = 256

LANES = 128
SUBLANES = 8
MXU_DIM = 256
VMEM_LIMIT_BYTES = 56 * 1024 * 1024


def _cparams(*sem):
    return pltpu.CompilerParams(dimension_semantics=sem, vmem_limit_bytes=VMEM_LIMIT_BYTES)


def _silu(x):
    return x * jax.nn.sigmoid(x)


def _dot(a, b):
    return jnp.dot(a, b, preferred_element_type=F32)


def _dot_nt(a, b):
    return lax.dot_general(a, b, (((1,), (1,)), ((), ())), preferred_element_type=F32)


def _dot_tn(a, b):
    return lax.dot_general(a, b, (((0,), (0,)), ((), ())), preferred_element_type=F32)


def _rms_mod(x, g, sc, sh):
    ms = jnp.mean(x * x, axis=-1, keepdims=True)
    h = x * lax.rsqrt(ms + EPS) * g
    return h * (1.0 + sc) + sh


def _ada_kernel(c_ref, w_ref, b_ref, o_ref):
    cm = _silu(c_ref[...]).astype(BF16)
    o_ref[0] = _dot(cm, w_ref[0].astype(BF16)) + b_ref[0]


def _ada_call(c_all, w_ada, b_ada):
    depth, d, n_out = w_ada.shape
    m = c_all.shape[0]
    tn = 512
    return pl.pallas_call(
        _ada_kernel,
        grid=(depth, n_out // tn),
        in_specs=[
            pl.BlockSpec((m, d), lambda l, j: (0, 0)),
            pl.BlockSpec((1, d, tn), lambda l, j: (l, 0, j)),
            pl.BlockSpec((1, 1, tn), lambda l, j: (l, 0, j)),
        ],
        out_specs=pl.BlockSpec((1, m, tn), lambda l, j: (l, 0, j)),
        out_shape=jax.ShapeDtypeStruct((depth, m, n_out), F32),
        compiler_params=_cparams("parallel", "parallel"),
        name="ada_mod",
    )(c_all, w_ada, b_ada.reshape(depth, 1, n_out))


def _mod_operand(m, t_len, tm):
    b, d = m.shape
    if t_len % tm == 0:
        per_b = t_len // tm
        return m.reshape(b, 1, d), pl.BlockSpec((1, 1, d), lambda i, *_: (i // per_b, 0, 0))
    assert tm % t_len == 0
    full = jnp.broadcast_to(m[:, None, :], (b, t_len, d)).reshape(b * t_len // tm, tm, d)
    return full, pl.BlockSpec((1, tm, d), lambda i, *_: (i, 0, 0))


def _row_tile(n, t_len, target):
    tm = min(target, n)
    while n % tm or (t_len % tm and tm % t_len):
        tm //= 2
    return tm


def _norm_mm_kernel(x_ref, g_ref, sc_ref, sh_ref, w_ref, o_ref, *, col_chunk):
    hb = _rms_mod(x_ref[...], g_ref[...], sc_ref[0], sh_ref[0]).astype(BF16)
    for c0 in range(0, o_ref.shape[1], col_chunk):
        o_ref[:, c0:c0 + col_chunk] = _dot(hb, w_ref[:, c0:c0 + col_chunk])


def _norm_mm_call(x, g, sc, sh, w, t_len, name):
    n, d = x.shape
    n_out = w.shape[1]
    tm = _row_tile(n, t_len, 512)
    sc_op, sc_spec = _mod_operand(sc, t_len, tm)
    sh_op, sh_spec = _mod_operand(sh, t_len, tm)
    return pl.pallas_call(
        functools.partial(_norm_mm_kernel, col_chunk=512),
        grid=(n // tm,),
        in_specs=[
            pl.BlockSpec((tm, d), lambda i: (i, 0)),
            pl.BlockSpec((1, d), lambda i: (0, 0)),
            sc_spec, sh_spec,
            pl.BlockSpec((d, n_out), lambda i: (0, 0)),
        ],
        out_specs=pl.BlockSpec((tm, n_out), lambda i: (i, 0)),
        out_shape=jax.ShapeDtypeStruct((n, n_out), F32),
        compiler_params=_cparams("parallel"),
        name=name,
    )(x, g.reshape(1, d), sc_op, sh_op, w)


def _mm_res_kernel(a_ref, w_ref, x_ref, gate_ref, o_ref):
    o_ref[...] = x_ref[...] + gate_ref[0] * _dot(a_ref[...], w_ref[...])


def _mm_res_call(a, w, x, gate, t_len, name):
    n, k = a.shape
    d = w.shape[1]
    tm = _row_tile(n, t_len, 512)
    gate_op, gate_spec = _mod_operand(gate, t_len, tm)
    return pl.pallas_call(
        _mm_res_kernel,
        grid=(n // tm,),
        in_specs=[
            pl.BlockSpec((tm, k), lambda i: (i, 0)),
            pl.BlockSpec((k, d), lambda i: (0, 0)),
            pl.BlockSpec((tm, d), lambda i: (i, 0)),
            gate_spec,
        ],
        out_specs=pl.BlockSpec((tm, d), lambda i: (i, 0)),
        out_shape=jax.ShapeDtypeStruct((n, d), F32),
        compiler_params=_cparams("parallel"),
        name=name,
    )(a, w, x, gate_op)


def _gla_tables(t_sub, nseq):
    r = t_sub * nseq
    levels = []
    m = t_sub // 2
    while m >= 1:
        levels.append(m)
        m //= 2
    n_lev = len(levels)
    tril = np.zeros((r, r), np.float32)
    mask = np.zeros((n_lev + 1, r, r), np.float32)
    for li, m in enumerate(levels):
        for row in range(r):
            blk = (row // (2 * m)) * 2 * m
            if row - blk >= m:
                mask[li, row, blk:blk + m] = 1.0
    for row in range(r):
        s0 = (row // t_sub) * t_sub
        tril[row, s0:row + 1] = 1.0
        mask[n_lev, row, row] = 1.0
    return tril, mask, levels


def _bcast_block_row(b, block, row_in_block):
    parts = [jnp.broadcast_to(b[s + row_in_block:s + row_in_block + 1, :], (block, b.shape[1]))
             for s in range(0, b.shape[0], block)]
    return parts[0] if len(parts) == 1 else jnp.concatenate(parts, axis=0)


def _level_decay(b, m):
    r = b.shape[0]
    pos = lax.broadcasted_iota(jnp.int32, b.shape, 0) & (2 * m - 1)
    if 2 * m >= SUBLANES:
        b_mid = _bcast_block_row(b, 2 * m, m - 1)
    else:
        b_mid = b
        for p in range(2 * m):
            if p != m - 1:
                b_mid = jnp.where(pos == p, pltpu.roll(b, (p - (m - 1)) % r, 0), b_mid)
    return jnp.exp(jnp.where(pos >= m, b - b_mid, b_mid - b))


def _gla_kernel(*refs, t_sub, nseq, n_ch, levels, n_heads, has_state):
    if has_state:
        proj_ref, lb_ref, gn_ref, tril_ref, mask_ref, s0_ref, o_ref, sout_ref, st_ref = refs
    else:
        proj_ref, lb_ref, gn_ref, tril_ref, mask_ref, o_ref, sout_ref, st_ref = refs
    r = t_sub * nseq
    n_lev = len(levels)
    dk = HG_DK
    hk = n_heads * dk
    c = pl.program_id(1)

    @pl.when(c == 0)
    def _():
        if has_state:
            for j in range(nseq):
                for h in range(n_heads):
                    st_ref[j, h] = s0_ref[j, h].T
        else:
            st_ref[...] = jnp.zeros_like(st_ref)

    lb = lb_ref[...]
    gn = gn_ref[...]
    tril = tril_ref[...]
    for ci in range(n_ch):
        rows = slice(ci * r, (ci + 1) * r)
        zq = proj_ref[rows, 0:hk]
        zf = proj_ref[rows, hk:2 * hk]
        f = lb + (1.0 - lb) * jax.nn.sigmoid(zf)
        lf = jnp.log(f)
        q = _silu(zq)
        k = 1.0 - f

        p0 = lf.astype(BF16)
        r1 = lf - p0.astype(F32)
        p1 = r1.astype(BF16)
        p2 = (r1 - p1.astype(F32)).astype(BF16)
        b = _dot(tril, p0) + _dot(tril, p1) + _dot(tril, p2)
        e_cum = jnp.exp(b)
        e_end = jnp.exp(_bcast_block_row(b, t_sub, t_sub - 1) - b)
        zs = [_level_decay(b, m) for m in levels]

        for h in range(n_heads):
            hs = slice(h * dk, (h + 1) * dk)
            qh = q[:, hs]
            kh = k[:, hs]
            vh = proj_ref[rows, 2 * hk + h * dk:2 * hk + (h + 1) * dk]
            a = _dot_nt(qh.astype(BF16), kh.astype(BF16)) * mask_ref[n_lev]
            for li in range(n_lev):
                z = zs[li][:, hs]
                a = a + _dot_nt((qh * z).astype(BF16), (kh * z).astype(BF16)) * mask_ref[li]
            vb = vh.astype(BF16)
            o_intra = _dot(a.astype(BF16), vb)
            qd = (qh * e_cum[:, hs]).astype(BF16)
            kd = (kh * e_end[:, hs]).astype(BF16)
            o_parts = []
            for j in range(nseq):
                rs = slice(j * t_sub, (j + 1) * t_sub)
                st = st_ref[j, h]
                o_parts.append(_dot_nt(qd[rs], st.astype(BF16)))
                d_last = e_cum[(j + 1) * t_sub - 1:(j + 1) * t_sub, hs]
                st_ref[j, h] = st * d_last + _dot_tn(vb[rs], kd[rs])
            o_inter = o_parts[0] if nseq == 1 else jnp.concatenate(o_parts, axis=0)
            o = o_intra + o_inter
            o = o * lax.rsqrt(jnp.mean(o * o, axis=-1, keepdims=True) + EPS) * gn
            zg = proj_ref[rows, 3 * hk + h * dk:3 * hk + (h + 1) * dk]
            o_ref[rows, hs] = (o * _silu(zg)).astype(BF16)

    @pl.when(c == pl.num_programs(1) - 1)
    def _():
        for j in range(nseq):
            for h in range(n_heads):
                sout_ref[j, h] = st_ref[j, h].T


def _gla_call(proj, lb, gn, s0, b, t_len):
    n = proj.shape[0]
    hk = lb.shape[-1]
    n_heads = hk // HG_DK
    has_state = s0 is not None
    if t_len % CHUNK == 0:
        t_sub, nseq = CHUNK, 1
    else:
        t_sub, nseq = t_len, CHUNK // t_len
        assert t_sub * nseq == CHUNK and b % nseq == 0
    r = t_sub * nseq
    tril, mask, levels = _gla_tables(t_sub, nseq)
    n_ch = math.gcd(GLA_CHUNKS_PER_STEP, t_len // t_sub)
    n_outer = b // nseq
    n_inner = t_len // (t_sub * n_ch)
    in_specs = [
        pl.BlockSpec((n_ch * r, proj.shape[1]), lambda i, c: (i * n_inner + c, 0)),
        pl.BlockSpec((1, hk), lambda i, c: (0, 0)),
        pl.BlockSpec((1, HG_DK), lambda i, c: (0, 0)),
        pl.BlockSpec(tril.shape, lambda i, c: (0, 0)),
        pl.BlockSpec(mask.shape, lambda i, c: (0, 0, 0)),
    ]
    args = [proj, lb.reshape(1, hk), gn.reshape(1, HG_DK), jnp.asarray(tril, BF16), jnp.asarray(mask, F32)]
    s_spec = pl.BlockSpec((nseq, n_heads, HG_DK, HG_DK), lambda i, c: (i, 0, 0, 0))
    if has_state:
        in_specs.append(s_spec)
        args.append(s0)
    o, s_out = pl.pallas_call(
        functools.partial(_gla_kernel, t_sub=t_sub, nseq=nseq, n_ch=n_ch, levels=tuple(levels),
                          n_heads=n_heads, has_state=has_state),
        grid=(n_outer, n_inner),
        in_specs=in_specs,
        out_specs=[
            pl.BlockSpec((n_ch * r, hk), lambda i, c: (i * n_inner + c, 0)),
            s_spec,
        ],
        out_shape=[
            jax.ShapeDtypeStruct((n, hk), BF16),
            jax.ShapeDtypeStruct((b, n_heads, HG_DK, HG_DK), F32),
        ],
        scratch_shapes=[pltpu.VMEM((nseq, n_heads, HG_DK, HG_DK), F32)],
        compiler_params=_cparams("parallel", "arbitrary"),
        name="gla_scan",
    )(*args)
    return o, s_out


def _route_kernel(x_ref, g_ref, sc_ref, sh_ref, wr_ref, br_ref, tril_ref, hx_ref, meta_ref, cnt_ref,
                  *, n_groups, epg):
    d = x_ref.shape[1]
    neg = -jnp.inf
    far = float(LANES)

    @pl.when(pl.program_id(0) == 0)
    def _():
        cnt_ref[...] = jnp.zeros_like(cnt_ref)

    h = _rms_mod(x_ref[...], g_ref[...], sc_ref[0], sh_ref[0])
    logit = _dot(h.astype(BF16), wr_ref[...]) + br_ref[...]
    lane = lax.broadcasted_iota(jnp.int32, logit.shape, 1).astype(F32)
    gm = lane < n_groups
    gmax = jnp.max(jnp.where(gm, logit, neg), axis=-1, keepdims=True)
    gstar = jnp.min(jnp.where(gm, jnp.where(logit == gmax, lane, far), far), axis=-1, keepdims=True)
    psum = jnp.sum(jnp.where(gm, jnp.exp(logit - gmax), 0.0), axis=-1, keepdims=True)
    pstar = 1.0 / psum
    lo = n_groups + gstar * epg
    em = jnp.where(lane >= lo, jnp.where(lane < lo + epg, 1.0, 0.0), 0.0)
    l1 = jnp.where(em > 0.0, logit, neg)
    v1 = jnp.max(l1, axis=-1, keepdims=True)
    i1 = jnp.min(jnp.where(l1 == v1, lane, far), axis=-1, keepdims=True)
    l2 = jnp.where(lane == i1, neg, l1)
    v2 = jnp.max(l2, axis=-1, keepdims=True)
    i2 = jnp.min(jnp.where(l2 == v2, lane, far), axis=-1, keepdims=True)
    e2 = jnp.exp(v2 - v1)
    wt1 = pstar / (1.0 + e2)
    wt2 = pstar * e2 / (1.0 + e2)
    onehot = jnp.where(lane == gstar, 1.0, 0.0)
    within = _dot(tril_ref[...], onehot.astype(BF16))
    carry = cnt_ref[...]
    rank = jnp.sum(onehot * (within + carry), axis=-1, keepdims=True)
    cnt_ref[...] = carry + jnp.sum(onehot, axis=0, keepdims=True)
    meta = (jnp.where(lane == i1 - lo, wt1, 0.0) + jnp.where(lane == i2 - lo, wt2, 0.0)
            + jnp.where(lane == epg, gstar, 0.0) + jnp.where(lane == epg + 1, rank, 0.0))
    hx_ref[:, :d] = h
    hx_ref[:, d:] = meta
    meta_ref[...] = meta


def _route_call(x, g, sc, sh, wr, br, t_len, name):
    n, d = x.shape
    tm = _row_tile(n, t_len, 1024 if t_len % 1024 == 0 else 256)
    sc_op, sc_spec = _mod_operand(sc, t_len, tm)
    sh_op, sh_spec = _mod_operand(sh, t_len, tm)
    tril = jnp.asarray(np.tril(np.ones((tm, tm), np.float32), -1), BF16)
    return pl.pallas_call(
        functools.partial(_route_kernel, n_groups=MOE_GROUPS, epg=MOE_EPG),
        grid=(n // tm,),
        in_specs=[
            pl.BlockSpec((tm, d), lambda i: (i, 0)),
            pl.BlockSpec((1, d), lambda i: (0, 0)),
            sc_spec, sh_spec,
            pl.BlockSpec((d, LANES), lambda i: (0, 0)),
            pl.BlockSpec((1, LANES), lambda i: (0, 0)),
            pl.BlockSpec((tm, tm), lambda i: (0, 0)),
        ],
        out_specs=[
            pl.BlockSpec((tm, d + LANES), lambda i: (i, 0)),
            pl.BlockSpec((tm, LANES), lambda i: (i, 0)),
            pl.BlockSpec((1, LANES), lambda i: (0, 0)),
        ],
        out_shape=[
            jax.ShapeDtypeStruct((n, d + LANES), F32),
            jax.ShapeDtypeStruct((n, LANES), F32),
            jax.ShapeDtypeStruct((1, LANES), F32),
        ],
        compiler_params=_cparams("arbitrary"),
        name=name + "_route",
    )(x, g.reshape(1, d), sc_op, sh_op, wr, br, tril)


def _row_gather(idx_ref, src_hbm, buf, sem, tile, slot, start):
    groups = buf.shape[1]
    base = tile * (groups * SUBLANES)

    def run(s):
        def body(i, carry):
            for u in range(SUBLANES):
                idx = idx_ref[base + i * SUBLANES + u]
                cp = pltpu.make_async_copy(src_hbm.at[pl.ds(idx, 1)],
                                           buf.at[s, i, pl.ds(u, 1)], sem.at[s])
                if start:
                    cp.start(priority=u % 2)
                else:
                    cp.wait()
            return carry

        lax.fori_loop(0, groups, body, 0)

    for s in range(2):
        @pl.when(slot == s)
        def _():
            run(s)


def _expert_kernel(src_ref, tgrp_ref, tval_ref, hx_hbm, w1_ref, w3_ref, w2_ref, y_ref, hbuf, sem,
                   *, epg, d):
    j = pl.program_id(0)
    rows = y_ref.shape[0]

    @pl.when(j == 0)
    def _():
        @pl.when(tval_ref[0] == 1)
        def _():
            _row_gather(src_ref, hx_hbm, hbuf, sem, 0, 0, True)

    @pl.when(j + 1 < pl.num_programs(0))
    def _():
        @pl.when(tval_ref[j + 1] == 1)
        def _():
            _row_gather(src_ref, hx_hbm, hbuf, sem, j + 1, (j + 1) % 2, True)

    @pl.when(tval_ref[j] == 1)
    def _():
        slot = j % 2
        _row_gather(src_ref, hx_hbm, hbuf, sem, j, slot, False)
        tile = hbuf[slot].reshape(rows, hbuf.shape[-1])
        hb = tile[:, :d].astype(BF16)
        hids = []
        for e in range(epg):
            a = _dot(hb, w1_ref[0, e])
            b = _dot(hb, w3_ref[0, e])
            hids.append((_silu(a) * b * tile[:, d + e:d + e + 1]).astype(BF16))
        hid = jnp.concatenate(hids, axis=1)
        y_ref[...] = _dot(hid, w2_ref[0].reshape(hid.shape[1], d))

    @pl.when(tval_ref[j] == 0)
    def _():
        y_ref[...] = jnp.zeros_like(y_ref)


def _combine_kernel(*refs, final):
    if final:
        dest_ref, ys_hbm, x_ref, gate_ref, fg_ref, o_ref, gbuf, sem = refs
    else:
        dest_ref, ys_hbm, x_ref, gate_ref, o_ref, gbuf, sem = refs
    i = pl.program_id(0)
    rows = x_ref.shape[0]

    @pl.when(i == 0)
    def _():
        _row_gather(dest_ref, ys_hbm, gbuf, sem, 0, 0, True)

    @pl.when(i + 1 < pl.num_programs(0))
    def _():
        _row_gather(dest_ref, ys_hbm, gbuf, sem, i + 1, (i + 1) % 2, True)

    slot = i % 2
    _row_gather(dest_ref, ys_hbm, gbuf, sem, i, slot, False)
    y = x_ref[...] + gate_ref[0] * gbuf[slot].reshape(rows, gbuf.shape[-1])
    if final:
        y = y * lax.rsqrt(jnp.mean(y * y, axis=-1, keepdims=True) + EPS) * fg_ref[...]
    o_ref[...] = y


def _moe_call(x, g, sc, sh, gate, wr, br, w1, w3, w2, final_g, t_len, name):
    n, d = x.shape
    n_exp, _, dff = w1.shape
    epg = MOE_EPG
    n_grp = n_exp // epg
    hx, meta, cnt = _route_call(x, g, sc, sh, wr, br, t_len, name)

    tm2 = 512 if n >= 8192 else 128
    grp = meta[:, epg].astype(jnp.int32)
    rank = meta[:, epg + 1].astype(jnp.int32)
    counts = cnt[0, :n_grp].astype(jnp.int32)
    n_tiles_g = (counts + tm2 - 1) // tm2
    tile_end = jnp.cumsum(n_tiles_g)
    tile_start = tile_end - n_tiles_g
    dest = rank
    for gi in range(n_grp):
        dest = dest + jnp.where(grp == gi, tile_start[gi] * tm2, 0)
    n_tiles = n // tm2 + n_grp
    src = jnp.zeros((n_tiles * tm2,), jnp.int32).at[dest].set(jnp.arange(n, dtype=jnp.int32))
    jt = jnp.arange(n_tiles, dtype=jnp.int32)
    tval = (jt < tile_end[-1]).astype(jnp.int32)
    tgrp = jnp.sum((jt[:, None] >= tile_end[None, :]).astype(jnp.int32), axis=1)
    last_grp = jnp.sum((tile_end[-1] - 1 >= tile_end).astype(jnp.int32))
    tgrp = jnp.minimum(jnp.where(tval == 1, tgrp, last_grp), n_grp - 1)

    wspec = lambda shape: pl.BlockSpec(shape, lambda j, src, tgrp, tval: (tgrp[j], 0, 0, 0))
    ys = pl.pallas_call(
        functools.partial(_expert_kernel, epg=epg, d=d),
        grid_spec=pltpu.PrefetchScalarGridSpec(
            num_scalar_prefetch=3,
            grid=(n_tiles,),
            in_specs=[
                pl.BlockSpec(memory_space=pl.ANY),
                wspec((1, epg, d, dff)), wspec((1, epg, d, dff)), wspec((1, epg, dff, d)),
            ],
            out_specs=pl.BlockSpec((tm2, d), lambda j, src, tgrp, tval: (j, 0)),
            scratch_shapes=[pltpu.VMEM((2, tm2 // SUBLANES, SUBLANES, d + LANES), F32),
                            pltpu.SemaphoreType.DMA((2,))],
        ),
        out_shape=jax.ShapeDtypeStruct((n_tiles * tm2, d), F32),
        compiler_params=_cparams("arbitrary"),
        name=name + "_experts",
    )(src, tgrp, tval, hx, w1.reshape(n_grp, epg, d, dff), w3.reshape(n_grp, epg, d, dff),
      w2.reshape(n_grp, epg, dff, d))

    tm = _row_tile(n, t_len, 512 if t_len % 512 == 0 else 256)
    gate_op, gate_spec = _mod_operand(gate, t_len, tm)
    final = final_g is not None
    in_specs = [
        pl.BlockSpec(memory_space=pl.ANY),
        pl.BlockSpec((tm, d), lambda i, dest: (i, 0)),
        gate_spec,
    ]
    args = [dest, ys, x, gate_op]
    if final:
        in_specs.append(pl.BlockSpec((1, d), lambda i, dest: (0, 0)))
        args.append(final_g.reshape(1, d))
    return pl.pallas_call(
        functools.partial(_combine_kernel, final=final),
        grid_spec=pltpu.PrefetchScalarGridSpec(
            num_scalar_prefetch=1,
            grid=(n // tm,),
            in_specs=in_specs,
            out_specs=pl.BlockSpec((tm, d), lambda i, dest: (i, 0)),
            scratch_shapes=[pltpu.VMEM((2, tm // SUBLANES, SUBLANES, d), F32),
                            pltpu.SemaphoreType.DMA((2,))],
        ),
        out_shape=jax.ShapeDtypeStruct((n, d), F32),
        compiler_params=_cparams("arbitrary"),
        name=name + "_combine",
    )(*args)


def _conv_taps(xm, halo, conv_w, conv_b, row_in_seq, axis):
    acc = conv_b + xm * conv_w[ML_CONV - 1]
    for s in range(1, ML_CONV):
        shifted = pltpu.roll(xm, s, axis)
        fill = pltpu.roll(halo, (s + SUBLANES - (ML_CONV - 1)) % SUBLANES, axis)
        if axis == 0:
            top = jnp.where(row_in_seq < s, fill, shifted[0:SUBLANES])
            shifted = jnp.concatenate([top, shifted[SUBLANES:]], axis=0)
        else:
            shifted = jnp.where(row_in_seq < s, fill, shifted)
        acc = acc + shifted * conv_w[ML_CONV - 1 - s]
    return acc


def _conv_qkv_kernel(xm_ref, c0_ref, cw_ref, cb_ref, wqk_ref, wv_ref, wg_ref, bg_ref,
                     q_ref, k_ref, v_ref, xc_ref, gates_ref, carry_scr, *, short_seq, k_scale):
    conv_w = [cw_ref[i:i + 1, :] for i in range(ML_CONV)]
    conv_b = cb_ref[...]
    if short_seq:
        xm3 = xm_ref[...]
        t_idx = lax.broadcasted_iota(jnp.int32, xm3.shape, 1)
        conv = _conv_taps(xm3, c0_ref[...], conv_w, conv_b, t_idx, 1)
        rows = xm3.shape[0] * xm3.shape[1]
        conv = conv.reshape(rows, xm3.shape[2])
        xm = xm3.reshape(rows, xm3.shape[2])
    else:
        t = pl.program_id(1)

        @pl.when(t == 0)
        def _():
            carry_scr[...] = c0_ref[0]

        xm = xm_ref[...]
        row8 = lax.broadcasted_iota(jnp.int32, (SUBLANES, xm.shape[1]), 0)
        conv = _conv_taps(xm, carry_scr[...], conv_w, conv_b, row8, 0)
        carry_scr[...] = pltpu.roll(xm[xm.shape[0] - SUBLANES:], ML_CONV - 1, 0)
    xc = _silu(conv)
    xc_ref[...] = xc
    inner = xm.shape[1]
    gates = bg_ref[...]
    xcb = xc.astype(BF16)
    xmb = xm.astype(BF16)
    for i in range(inner // MXU_DIM):
        cs = slice(i * MXU_DIM, (i + 1) * MXU_DIM)
        qk = _dot(xcb[:, cs], wqk_ref[i])
        qi = qk[:, :MXU_DIM]
        ki = qk[:, MXU_DIM:]
        vi = _dot(xmb[:, cs], wv_ref[i])
        q_ref[:, cs] = qi.astype(q_ref.dtype)
        k_ref[:, cs] = (ki * k_scale).astype(k_ref.dtype)
        v_ref[:, cs] = vi.astype(v_ref.dtype)
        gates = gates + _dot(qi.astype(BF16), wg_ref[0, cs, :])
        gates = gates + _dot(ki.astype(BF16), wg_ref[1, cs, :])
        gates = gates + _dot(vi.astype(BF16), wg_ref[2, cs, :])
    lane = lax.broadcasted_iota(jnp.int32, gates.shape, 1)
    log_sig = jnp.minimum(gates, 0.0) - jnp.log1p(jnp.exp(-jnp.abs(gates)))
    gates_ref[...] = jnp.where(lane < ML_HEADS, gates, log_sig)


def _conv_qkv_call(up, conv0, conv_w, conv_b, wqk, wv, wg, bg, b, t_len):
    n = up.shape[0]
    inner = up.shape[1] // 2
    short_seq = t_len == SUBLANES
    dh = inner // ML_HEADS
    k_scale = dh ** -0.5
    qkv_dtype = F32 if short_seq else BF16
    halo = jnp.zeros((b, SUBLANES, inner), F32)
    if conv0 is not None:
        halo = halo.at[:, :ML_CONV - 1].set(conv0)
    if short_seq:
        nb = min(32, b)
        grid = (b // nb,)
        up_op = up.reshape(b, t_len, 2 * inner)
        xm_spec = pl.BlockSpec((nb, t_len, inner), lambda i: (i, 0, 0))
        c0_spec = pl.BlockSpec((nb, SUBLANES, inner), lambda i: (i, 0, 0))
        rows = nb * t_len
        row_spec = lambda w: pl.BlockSpec((rows, w), lambda i: (i, 0))
        const2 = lambda s: pl.BlockSpec(s, lambda i: (0, 0))
        const3 = lambda s: pl.BlockSpec(s, lambda i: (0, 0, 0))
        sem = ("parallel",)
    else:
        tm = math.gcd(256, t_len)
        nt = t_len // tm
        grid = (b, nt)
        up_op = up
        xm_spec = pl.BlockSpec((tm, inner), lambda i, t: (i * nt + t, 0))
        c0_spec = pl.BlockSpec((1, SUBLANES, inner), lambda i, t: (i, 0, 0))
        rows = tm
        row_spec = lambda w: pl.BlockSpec((rows, w), lambda i, t: (i * nt + t, 0))
        const2 = lambda s: pl.BlockSpec(s, lambda i, t: (0, 0))
        const3 = lambda s: pl.BlockSpec(s, lambda i, t: (0, 0, 0))
        sem = ("parallel", "arbitrary")
    return pl.pallas_call(
        functools.partial(_conv_qkv_kernel, short_seq=short_seq, k_scale=k_scale),
        grid=grid,
        in_specs=[
            xm_spec, c0_spec,
            const2((ML_CONV, inner)), const2((1, inner)),
            const3(wqk.shape), const3(wv.shape), const3(wg.shape), const2((1, LANES)),
        ],
        out_specs=[row_spec(inner), row_spec(inner), row_spec(inner), row_spec(inner), row_spec(LANES)],
        out_shape=[
            jax.ShapeDtypeStruct((n, inner), qkv_dtype),
            jax.ShapeDtypeStruct((n, inner), qkv_dtype),
            jax.ShapeDtypeStruct((n, inner), qkv_dtype),
            jax.ShapeDtypeStruct((n, inner), F32),
            jax.ShapeDtypeStruct((n, LANES), F32),
        ],
        scratch_shapes=[pltpu.VMEM((SUBLANES, inner), F32)],
        compiler_params=_cparams(*sem),
        name="mlstm_conv_qkv",
    )(up_op, halo, conv_w, conv_b.reshape(1, inner), wqk, wv, wg, bg)


def _mlstm_kernel(*refs, has_state):
    if has_state:
        (q_ref, k_ref, v_ref, gates_ref, xc_ref, z_ref, nw_ref, sk_ref, c0_ref, n0_ref, m0_ref,
         o_ref, c_ref, n_ref, m_ref) = refs
    else:
        (q_ref, k_ref, v_ref, gates_ref, xc_ref, z_ref, nw_ref, sk_ref,
         o_ref, c_ref, n_ref, m_ref) = refs
    ch = pl.program_id(1)
    length = q_ref.shape[0]
    dh = q_ref.shape[1] // ML_HEADS

    @pl.when(ch == 0)
    def _():
        if has_state:
            c_ref[...] = c0_ref[...]
            n_ref[...] = n0_ref[...]
            m_ref[...] = m0_ref[...]
        else:
            c_ref[...] = jnp.zeros_like(c_ref)
            n_ref[...] = jnp.zeros_like(n_ref)
            m_ref[...] = jnp.zeros_like(m_ref)

    gates = gates_ref[...]
    gates_t = gates.T
    t_idx = lax.broadcasted_iota(jnp.int32, (length, length), 0)
    s_idx = lax.broadcasted_iota(jnp.int32, (length, length), 1)
    causal = s_idx <= t_idx
    m_all = m_ref[0]
    m_new = m_all
    lane = lax.broadcasted_iota(jnp.int32, m_all.shape, 1)
    for h in range(ML_HEADS):
        hs = slice(h * dh, (h + 1) * dh)
        ig_col = gates[:, h:h + 1]
        lf_col = gates[:, ML_HEADS + h:ML_HEADS + h + 1]
        ig_row = gates_t[h:h + 1, :]
        lf_row = gates_t[ML_HEADS + h:ML_HEADS + h + 1, :]
        b_col = jnp.sum(jnp.where(causal, lf_row, 0.0), axis=1, keepdims=True)
        b_row = jnp.sum(jnp.where(t_idx <= s_idx, lf_col, 0.0), axis=0, keepdims=True)
        m_prev = m_all[:, h:h + 1]
        dm = jnp.where(causal, b_col - b_row + ig_row, -jnp.inf)
        a = b_col + m_prev
        m_t = jnp.maximum(a, jnp.max(dm, axis=1, keepdims=True))
        w_c = jnp.exp(a - m_t)
        w = jnp.exp(dm - m_t)
        qh = q_ref[:, hs].astype(BF16)
        kh = k_ref[:, hs]
        vh = v_ref[:, hs].astype(BF16)
        c_h = c_ref[0, h]
        n_h = n_ref[0, :, hs]
        s = _dot_nt(qh, kh.astype(BF16)) * w
        num = w_c * _dot(qh, c_h.astype(BF16)) + _dot(s.astype(BF16), vh)
        den = w_c * jnp.sum(qh.astype(F32) * n_h, axis=-1, keepdims=True) + jnp.sum(s, axis=-1, keepdims=True)
        hc = num / jnp.maximum(jnp.abs(den), jnp.exp(-m_t))
        m_last = m_t[length - 1:length]
        b_last = b_col[length - 1:length]
        wl_c = jnp.exp(a[length - 1:length] - m_last)
        wl_col = jnp.exp(b_last - b_col + ig_col - m_last)
        kw = kh.astype(F32) * wl_col
        c_ref[0, h] = wl_c * c_h + _dot_tn(kw.astype(BF16), vh)
        n_ref[0, :, hs] = wl_c * n_h + jnp.sum(kw, axis=0, keepdims=True)
        m_new = jnp.where(lane == h, m_last, m_new)
        mu = jnp.mean(hc, axis=-1, keepdims=True)
        dev = hc - mu
        var = jnp.mean(dev * dev, axis=-1, keepdims=True)
        hn = dev * lax.rsqrt(var + EPS) * nw_ref[:, hs]
        ho = (hn + sk_ref[:, hs] * xc_ref[:, hs]) * _silu(z_ref[:, hs])
        o_ref[:, hs] = ho.astype(o_ref.dtype)
    m_ref[0] = m_new


def _mlstm_call(q, k, v, gates, xc, up, norm_w, skip, c0, n0, m0, b, t_len):
    n, inner = q.shape
    dh = inner // ML_HEADS
    has_state = c0 is not None
    length = math.gcd(t_len, ML_CHUNK)
    nc = t_len // length
    row = lambda w, col=0: pl.BlockSpec((length, w), lambda i, c: (i * nc + c, col))
    const = lambda w: pl.BlockSpec((1, w), lambda i, c: (0, 0))
    c_spec = pl.BlockSpec((1, ML_HEADS, dh, dh), lambda i, c: (i, 0, 0, 0))
    n_spec = pl.BlockSpec((1, 1, inner), lambda i, c: (i, 0, 0))
    m_spec = pl.BlockSpec((1, 1, LANES), lambda i, c: (i, 0, 0))
    in_specs = [row(inner), row(inner), row(inner), row(LANES), row(inner), row(inner, 1),
                const(inner), const(inner)]
    args = [q, k, v, gates, xc, up, norm_w.reshape(1, inner), skip.reshape(1, inner)]
    if has_state:
        in_specs += [c_spec, n_spec, m_spec]
        m0_pad = jnp.zeros((b, 1, LANES), F32).at[:, 0, :ML_HEADS].set(m0)
        args += [c0, n0.reshape(b, 1, inner), m0_pad]
    ho, c_t, n_t, m_t = pl.pallas_call(
        functools.partial(_mlstm_kernel, has_state=has_state),
        grid=(b, nc),
        in_specs=in_specs,
        out_specs=[row(inner), c_spec, n_spec, m_spec],
        out_shape=[
            jax.ShapeDtypeStruct((n, inner), BF16 if length % 16 == 0 else F32),
            jax.ShapeDtypeStruct((b, ML_HEADS, dh, dh), F32),
            jax.ShapeDtypeStruct((b, 1, inner), F32),
            jax.ShapeDtypeStruct((b, 1, LANES), F32),
        ],
        compiler_params=_cparams("parallel", "arbitrary"),
        name="mlstm_scan",
    )(*args)
    return ho, c_t, n_t.reshape(b, ML_HEADS, dh), m_t[:, 0, :ML_HEADS]


def _block_diag_tiles(w):
    n_blk, blk, _ = w.shape
    per = MXU_DIM // blk
    wt = w.reshape(n_blk // per, per, blk, blk)
    eye = jnp.eye(per, dtype=w.dtype)
    dense = jnp.einsum('tnoi,nm->tnimo', wt, eye)
    return dense.reshape(n_blk // per, MXU_DIM, MXU_DIM)


def _prep_weights(p):
    w = {}
    w['hg_win'] = p['hg_win'].astype(BF16)
    w['hg_wo'] = p['hg_wo'].astype(BF16)
    w['ml_wup'] = p['ml_wup'].astype(BF16)
    w['ml_wdown'] = p['ml_wdown'].astype(BF16)
    w['moe_w1'] = p['moe_w1'].astype(BF16)
    w['moe_w3'] = p['moe_w3'].astype(BF16)
    w['moe_w2'] = p['moe_w2'].astype(BF16)
    depth, d, g = p['moe_wrg'].shape
    n_exp = g * p['moe_wre'].shape[-1]
    wr = jnp.zeros((depth, d, LANES), F32)
    wr = wr.at[:, :, :g].set(p['moe_wrg']).at[:, :, g:g + n_exp].set(p['moe_wre'].reshape(depth, d, n_exp))
    w['moe_wr'] = wr.astype(BF16)
    br = jnp.zeros((depth, 1, LANES), F32)
    br = br.at[:, 0, :g].set(p['moe_brg']).at[:, 0, g:g + n_exp].set(p['moe_bre'].reshape(depth, n_exp))
    w['moe_br'] = br
    n_b = p['ml_wq'].shape[0]
    wq = jnp.stack([_block_diag_tiles(p['ml_wq'][j]) for j in range(n_b)])
    wk = jnp.stack([_block_diag_tiles(p['ml_wk'][j]) for j in range(n_b)])
    wv = jnp.stack([_block_diag_tiles(p['ml_wv'][j]) for j in range(n_b)])
    w['ml_wqk'] = jnp.concatenate([wq, wk], axis=-1).astype(BF16)
    w['ml_wv'] = wv.astype(BF16)
    inner = p['ml_conv_b'].shape[-1]
    wg = jnp.zeros((n_b, 3 * inner, LANES), F32)
    wg = wg.at[:, :, :ML_HEADS].set(p['ml_wig']).at[:, :, ML_HEADS:2 * ML_HEADS].set(p['ml_wfg'])
    w['ml_wg'] = wg.reshape(n_b, 3, inner, LANES).astype(BF16)
    bg = jnp.zeros((n_b, 1, LANES), F32)
    bg = bg.at[:, 0, :ML_HEADS].set(p['ml_big']).at[:, 0, ML_HEADS:2 * ML_HEADS].set(p['ml_bfg'])
    w['ml_bg'] = bg
    return w


def _lb_kernel(lb_ref, o_ref):
    x = lb_ref[...]
    mx = jnp.max(x, axis=0, keepdims=True)
    ex = jnp.exp(x - mx)
    sm = ex / jnp.sum(ex, axis=0, keepdims=True)
    rows = []
    run = jnp.zeros_like(sm[0:1])
    for i in range(x.shape[0]):
        run = run + sm[i:i + 1]
        rows.append(run)
    o_ref[...] = jnp.concatenate(rows, axis=0)


def _lb_call(hg_lb):
    return pl.pallas_call(
        _lb_kernel,
        out_shape=jax.ShapeDtypeStruct(hg_lb.shape, F32),
        name="hgrn_lower_bound",
    )(hg_lb)


def _trunk(x3, mods, s_hg, s_c, s_n, s_m, s_conv, p, w, lb_all):
    b, t_len, d = x3.shape
    n = b * t_len
    x = x3.reshape(n, d)
    depth = p['norm_g'].shape[0]
    new_hg, new_c, new_n, new_m, new_conv = [], [], [], [], []
    for l in range(depth):
        sh1, sc1, g1, sh2, sc2, g2 = [mods[l][:, i * d:(i + 1) * d] for i in range(6)]
        if l % 2 == 0:
            a = l // 2
            proj = _norm_mm_call(x, p['norm_g'][l, 0], sc1, sh1, w['hg_win'][a], t_len, "hgrn_in_proj")
            o, s_t = _gla_call(proj, lb_all[l], p['hg_norm'][a], None if s_hg is None else s_hg[a], b, t_len)
            new_hg.append(s_t)
            x = _mm_res_call(o, w['hg_wo'][a], x, g1, t_len, "hgrn_out_proj")
        else:
            j = l // 2
            up = _norm_mm_call(x, p['norm_g'][l, 0], sc1, sh1, w['ml_wup'][j], t_len, "mlstm_up_proj")
            inner = up.shape[1] // 2
            q, k, v, xc, gates = _conv_qkv_call(
                up, None if s_conv is None else s_conv[j], p['ml_conv_w'][j], p['ml_conv_b'][j],
                w['ml_wqk'][j], w['ml_wv'][j], w['ml_wg'][j], w['ml_bg'][j], b, t_len)
            ho, c_t, n_t, m_t = _mlstm_call(
                q, k, v, gates, xc, up, p['ml_norm'][j], p['ml_skip'][j],
                None if s_c is None else s_c[j], None if s_n is None else s_n[j],
                None if s_m is None else s_m[j], b, t_len)
            new_c.append(c_t)
            new_n.append(n_t)
            new_m.append(m_t)
            new_conv.append(up.reshape(b, t_len, 2 * inner)[:, t_len - (ML_CONV - 1):, :inner])
            x = _mm_res_call(ho.astype(BF16), w['ml_wdown'][j], x, g1, t_len, "mlstm_down_proj")
        x = _moe_call(x, p['norm_g'][l, 1], sc2, sh2, g2, w['moe_wr'][l], w['moe_br'][l],
                      w['moe_w1'][l], w['moe_w3'][l], w['moe_w2'][l],
                      p['final_g'] if l == depth - 1 else None, t_len, "moe_layer%d" % l)
    return (x.reshape(b, t_len, d), jnp.stack(new_hg), jnp.stack(new_c), jnp.stack(new_n),
            jnp.stack(new_m), jnp.stack(new_conv))


def kernel(x_prompt, x_sample, c_prompt, c_sample, state_hgrn, state_mlstm_c, state_mlstm_n, state_mlstm_m, state_conv, w_ada, b_ada, norm_g, final_g, hg_win, hg_wo, hg_norm, hg_lb, ml_wup, ml_conv_w, ml_conv_b, ml_wq, ml_wk, ml_wv, ml_wig, ml_big, ml_wfg, ml_bfg, ml_norm, ml_skip, ml_wdown, moe_wrg, moe_brg, moe_wre, moe_bre, moe_w1, moe_w3, moe_w2):
    p = dict(w_ada=w_ada, b_ada=b_ada, norm_g=norm_g, final_g=final_g,
             hg_win=hg_win, hg_wo=hg_wo, hg_norm=hg_norm, hg_lb=hg_lb,
             ml_wup=ml_wup, ml_conv_w=ml_conv_w, ml_conv_b=ml_conv_b, ml_wq=ml_wq, ml_wk=ml_wk, ml_wv=ml_wv,
             ml_wig=ml_wig, ml_big=ml_big, ml_wfg=ml_wfg, ml_bfg=ml_bfg, ml_norm=ml_norm, ml_skip=ml_skip,
             ml_wdown=ml_wdown, moe_wrg=moe_wrg, moe_brg=moe_brg, moe_wre=moe_wre, moe_bre=moe_bre,
             moe_w1=moe_w1, moe_w3=moe_w3, moe_w2=moe_w2)
    w = _prep_weights(p)
    lb_all = _lb_call(hg_lb)
    bp = x_prompt.shape[0]
    c_all = jnp.concatenate([c_prompt, c_sample], axis=0)
    mod_all = _ada_call(c_all, w_ada, b_ada)
    mods_p = [mod_all[l, :bp] for l in range(mod_all.shape[0])]
    mods_s = [mod_all[l, bp:] for l in range(mod_all.shape[0])]
    y_p, hg_p, mc_p, mn_p, mm_p, conv_p = _trunk(x_prompt, mods_p, None, None, None, None, None, p, w, lb_all)
    y_s, hg_s, mc_s, mn_s, mm_s, conv_s = _trunk(x_sample, mods_s, state_hgrn, state_mlstm_c, state_mlstm_n,
                                                 state_mlstm_m, state_conv, p, w, lb_all)
    return (y_p, y_s, hg_p, mc_p, mn_p, mm_p, conv_p, hg_s, mc_s, mn_s, mm_s, conv_s)
```

```python
import functools
import math

import numpy as np
import jax
import jax.numpy as jnp
from jax import lax
from jax.experimental import pallas as pl
from jax.experimental.pallas import tpu as pltpu

F32 = jnp.float32
BF16 = jnp.bfloat16
EPS = 1e-6

HG_DK = 128
ML_HEADS = 4
ML_CONV = 4
ML_QKV_BLOCK = 4
MOE_GROUPS = 4
MOE_EPG = 4
CHUNK = 64
GLA_CHUNKS_PER_STEP = 4
ML_CHUNK = 256

LANES = 128
SUBLANES = 8
MXU_DIM = 256
VMEM_LIMIT_BYTES = 56 * 1024 * 1024


def _cparams(*sem):
    return pltpu.CompilerParams(dimension_semantics=sem, vmem_limit_bytes=VMEM_LIMIT_BYTES)


def _silu(x):
    return x * jax.nn.sigmoid(x)


def _dot(a, b):
    return jnp.dot(a, b, preferred_element_type=F32)


def _dot_nt(a, b):
    return lax.dot_general(a, b, (((1,), (1,)), ((), ())), preferred_element_type=F32)


def _dot_tn(a, b):
    return lax.dot_general(a, b, (((0,), (0,)), ((), ())), preferred_element_type=F32)


def _rms_mod(x, g, sc, sh):
    ms = jnp.mean(x * x, axis=-1, keepdims=True)
    h = x * lax.rsqrt(ms + EPS) * g
    return h * (1.0 + sc) + sh


def _ada_kernel(c_ref, w_ref, b_ref, o_ref):
    cm = _silu(c_ref[...]).astype(BF16)
    o_ref[0] = _dot(cm, w_ref[0].astype(BF16)) + b_ref[0]


def _ada_call(c_all, w_ada, b_ada):
    depth, d, n_out = w_ada.shape
    m = c_all.shape[0]
    tn = 512
    return pl.pallas_call(
        _ada_kernel,
        grid=(depth, n_out // tn),
        in_specs=[
            pl.BlockSpec((m, d), lambda l, j: (0, 0)),
            pl.BlockSpec((1, d, tn), lambda l, j: (l, 0, j)),
            pl.BlockSpec((1, 1, tn), lambda l, j: (l, 0, j)),
        ],
        out_specs=pl.BlockSpec((1, m, tn), lambda l, j: (l, 0, j)),
        out_shape=jax.ShapeDtypeStruct((depth, m, n_out), F32),
        compiler_params=_cparams("parallel", "parallel"),
        name="ada_mod",
    )(c_all, w_ada, b_ada.reshape(depth, 1, n_out))


def _mod_operand(m, t_len, tm):
    b, d = m.shape
    if t_len % tm == 0:
        per_b = t_len // tm
        return m.reshape(b, 1, d), pl.BlockSpec((1, 1, d), lambda i, *_: (i // per_b, 0, 0))
    assert tm % t_len == 0
    nb = tm // t_len
    return m.reshape(b // nb, nb, 1, d), pl.BlockSpec((1, nb, 1, d), lambda i, *_: (i, 0, 0, 0))


def _mod_rows(ref, rows):
    v = ref[0]
    if v.ndim == 2:
        return v
    nb, _, d = v.shape
    return jnp.broadcast_to(v, (nb, rows // nb, d)).reshape(rows, d)


def _row_tile(n, t_len, target):
    tm = min(target, n)
    while n % tm or (t_len % tm and tm % t_len):
        tm //= 2
    return tm


def _norm_mm_kernel(x_ref, g_ref, sc_ref, sh_ref, w_ref, o_ref, *, col_chunk):
    rows = x_ref.shape[0]
    hb = _rms_mod(x_ref[...], g_ref[...], _mod_rows(sc_ref, rows), _mod_rows(sh_ref, rows)).astype(BF16)
    for c0 in range(0, o_ref.shape[1], col_chunk):
        o_ref[:, c0:c0 + col_chunk] = _dot(hb, w_ref[:, c0:c0 + col_chunk])


def _norm_mm_call(x, g, sc, sh, w, t_len, name):
    n, d = x.shape
    n_out = w.shape[1]
    tm = _row_tile(n, t_len, 512)
    sc_op, sc_spec = _mod_operand(sc, t_len, tm)
    sh_op, sh_spec = _mod_operand(sh, t_len, tm)
    return pl.pallas_call(
        functools.partial(_norm_mm_kernel, col_chunk=512),
        grid=(n // tm,),
        in_specs=[
            pl.BlockSpec((tm, d), lambda i: (i, 0)),
            pl.BlockSpec((1, d), lambda i: (0, 0)),
            sc_spec, sh_spec,
            pl.BlockSpec((d, n_out), lambda i: (0, 0)),
        ],
        out_specs=pl.BlockSpec((tm, n_out), lambda i: (i, 0)),
        out_shape=jax.ShapeDtypeStruct((n, n_out), F32),
        compiler_params=_cparams("parallel"),
        name=name,
    )(x, g.reshape(1, d), sc_op, sh_op, w)


def _mm_res_kernel(a_ref, w_ref, x_ref, gate_ref, o_ref):
    o_ref[...] = x_ref[...] + _mod_rows(gate_ref, x_ref.shape[0]) * _dot(a_ref[...], w_ref[...])


def _mm_res_call(a, w, x, gate, t_len, name):
    n, k = a.shape
    d = w.shape[1]
    tm = _row_tile(n, t_len, 512)
    gate_op, gate_spec = _mod_operand(gate, t_len, tm)
    return pl.pallas_call(
        _mm_res_kernel,
        grid=(n // tm,),
        in_specs=[
            pl.BlockSpec((tm, k), lambda i: (i, 0)),
            pl.BlockSpec((k, d), lambda i: (0, 0)),
            pl.BlockSpec((tm, d), lambda i: (i, 0)),
            gate_spec,
        ],
        out_specs=pl.BlockSpec((tm, d), lambda i: (i, 0)),
        out_shape=jax.ShapeDtypeStruct((n, d), F32),
        compiler_params=_cparams("parallel"),
        name=name,
    )(a, w, x, gate_op)


def _gla_tables(t_sub, nseq):
    r = t_sub * nseq
    levels = []
    m = t_sub // 2
    while m >= 1:
        levels.append(m)
        m //= 2
    n_lev = len(levels)
    tril = np.zeros((r, r), np.float32)
    mask = np.zeros((n_lev + 1, r, r), np.float32)
    for li, m in enumerate(levels):
        for row in range(r):
            blk = (row // (2 * m)) * 2 * m
            if row - blk >= m:
                mask[li, row, blk:blk + m] = 1.0
    for row in range(r):
        s0 = (row // t_sub) * t_sub
        tril[row, s0:row + 1] = 1.0
        mask[n_lev, row, row] = 1.0
    return tril, mask, levels


def _bcast_block_row(b, block, row_in_block):
    parts = [jnp.broadcast_to(b[s + row_in_block:s + row_in_block + 1, :], (block, b.shape[1]))
             for s in range(0, b.shape[0], block)]
    return parts[0] if len(parts) == 1 else jnp.concatenate(parts, axis=0)


def _level_decay(b, m):
    r = b.shape[0]
    pos = lax.broadcasted_iota(jnp.int32, b.shape, 0) & (2 * m - 1)
    if 2 * m >= SUBLANES:
        b_mid = _bcast_block_row(b, 2 * m, m - 1)
    else:
        b_mid = b
        for p in range(2 * m):
            if p != m - 1:
                b_mid = jnp.where(pos == p, pltpu.roll(b, (p - (m - 1)) % r, 0), b_mid)
    return jnp.exp(jnp.where(pos >= m, b - b_mid, b_mid - b))


def _gla_kernel(*refs, t_sub, nseq, n_ch, levels, n_heads, has_state):
    if has_state:
        proj_ref, lb_ref, gn_ref, tril_ref, mask_ref, s0_ref, o_ref, sout_ref, st_ref = refs
    else:
        proj_ref, lb_ref, gn_ref, tril_ref, mask_ref, o_ref, sout_ref, st_ref = refs
    r = t_sub * nseq
    n_lev = len(levels)
    dk = HG_DK
    hk = n_heads * dk
    c = pl.program_id(1)

    @pl.when(c == 0)
    def _():
        if has_state:
            for j in range(nseq):
                for h in range(n_heads):
                    st_ref[j, h] = s0_ref[j, h].T
        else:
            st_ref[...] = jnp.zeros_like(st_ref)

    lb = lb_ref[...]
    gn = gn_ref[...]
    tril = tril_ref[...]
    for ci in range(n_ch):
        rows = slice(ci * r, (ci + 1) * r)
        zq = proj_ref[rows, 0:hk]
        zf = proj_ref[rows, hk:2 * hk]
        f = lb + (1.0 - lb) * jax.nn.sigmoid(zf)
        lf = jnp.log(f)
        q = _silu(zq)
        k = 1.0 - f

        p0 = lf.astype(BF16)
        r1 = lf - p0.astype(F32)
        p1 = r1.astype(BF16)
        p2 = (r1 - p1.astype(F32)).astype(BF16)
        b = _dot(tril, p0) + _dot(tril, p1) + _dot(tril, p2)
        e_cum = jnp.exp(b)
        e_end = jnp.exp(_bcast_block_row(b, t_sub, t_sub - 1) - b)
        zs = [_level_decay(b, m) for m in levels]

        for h in range(n_heads):
            hs = slice(h * dk, (h + 1) * dk)
            qh = q[:, hs]
            kh = k[:, hs]
            vh = proj_ref[rows, 2 * hk + h * dk:2 * hk + (h + 1) * dk]
            a = _dot_nt(qh.astype(BF16), kh.astype(BF16)) * mask_ref[n_lev]
            for li in range(n_lev):
                z = zs[li][:, hs]
                a = a + _dot_nt((qh * z).astype(BF16), (kh * z).astype(BF16)) * mask_ref[li]
            vb = vh.astype(BF16)
            o_intra = _dot(a.astype(BF16), vb)
            qd = (qh * e_cum[:, hs]).astype(BF16)
            kd = (kh * e_end[:, hs]).astype(BF16)
            o_parts = []
            for j in range(nseq):
                rs = slice(j * t_sub, (j + 1) * t_sub)
                st = st_ref[j, h]
                o_parts.append(_dot_nt(qd[rs], st.astype(BF16)))
                d_last = e_cum[(j + 1) * t_sub - 1:(j + 1) * t_sub, hs]
                st_ref[j, h] = st * d_last + _dot_tn(vb[rs], kd[rs])
            o_inter = o_parts[0] if nseq == 1 else jnp.concatenate(o_parts, axis=0)
            o = o_intra + o_inter
            o = o * lax.rsqrt(jnp.mean(o * o, axis=-1, keepdims=True) + EPS) * gn
            zg = proj_ref[rows, 3 * hk + h * dk:3 * hk + (h + 1) * dk]
            o_ref[rows, hs] = (o * _silu(zg)).astype(BF16)

    @pl.when(c == pl.num_programs(1) - 1)
    def _():
        for j in range(nseq):
            for h in range(n_heads):
                sout_ref[j, h] = st_ref[j, h].T


def _gla_call(proj, lb, gn, s0, b, t_len):
    n = proj.shape[0]
    hk = lb.shape[-1]
    n_heads = hk // HG_DK
    has_state = s0 is not None
    if t_len % CHUNK == 0:
        t_sub, nseq = CHUNK, 1
    else:
        t_sub, nseq = t_len, CHUNK // t_len
        assert t_sub * nseq == CHUNK and b % nseq == 0
    r = t_sub * nseq
    tril, mask, levels = _gla_tables(t_sub, nseq)
    n_ch = math.gcd(GLA_CHUNKS_PER_STEP, t_len // t_sub)
    n_outer = b // nseq
    n_inner = t_len // (t_sub * n_ch)
    in_specs = [
        pl.BlockSpec((n_ch * r, proj.shape[1]), lambda i, c: (i * n_inner + c, 0)),
        pl.BlockSpec((1, hk), lambda i, c: (0, 0)),
        pl.BlockSpec((1, HG_DK), lambda i, c: (0, 0)),
        pl.BlockSpec(tril.shape, lambda i, c: (0, 0)),
        pl.BlockSpec(mask.shape, lambda i, c: (0, 0, 0)),
    ]
    args = [proj, lb.reshape(1, hk), gn.reshape(1, HG_DK), jnp.asarray(tril, BF16), jnp.asarray(mask, F32)]
    s_spec = pl.BlockSpec((nseq, n_heads, HG_DK, HG_DK), lambda i, c: (i, 0, 0, 0))
    if has_state:
        in_specs.append(s_spec)
        args.append(s0)
    o, s_out = pl.pallas_call(
        functools.partial(_gla_kernel, t_sub=t_sub, nseq=nseq, n_ch=n_ch, levels=tuple(levels),
                          n_heads=n_heads, has_state=has_state),
        grid=(n_outer, n_inner),
        in_specs=in_specs,
        out_specs=[
            pl.BlockSpec((n_ch * r, hk), lambda i, c: (i * n_inner + c, 0)),
            s_spec,
        ],
        out_shape=[
            jax.ShapeDtypeStruct((n, hk), BF16),
            jax.ShapeDtypeStruct((b, n_heads, HG_DK, HG_DK), F32),
        ],
        scratch_shapes=[pltpu.VMEM((nseq, n_heads, HG_DK, HG_DK), F32)],
        compiler_params=_cparams("parallel", "arbitrary"),
        name="gla_scan",
    )(*args)
    return o, s_out


def _route_kernel(x_ref, g_ref, sc_ref, sh_ref, wr_ref, br_ref, tril_ref, hx_ref, meta_ref, cnt_ref,
                  *, n_groups, epg):
    d = x_ref.shape[1]
    neg = -jnp.inf
    far = float(LANES)

    @pl.when(pl.program_id(0) == 0)
    def _():
        cnt_ref[...] = jnp.zeros_like(cnt_ref)

    h = _rms_mod(x_ref[...], g_ref[...], _mod_rows(sc_ref, x_ref.shape[0]), _mod_rows(sh_ref, x_ref.shape[0]))
    logit = _dot(h.astype(BF16), wr_ref[...]) + br_ref[...]
    lane = lax.broadcasted_iota(jnp.int32, logit.shape, 1).astype(F32)
    gm = lane < n_groups
    gmax = jnp.max(jnp.where(gm, logit, neg), axis=-1, keepdims=True)
    gstar = jnp.min(jnp.where(gm, jnp.where(logit == gmax, lane, far), far), axis=-1, keepdims=True)
    psum = jnp.sum(jnp.where(gm, jnp.exp(logit - gmax), 0.0), axis=-1, keepdims=True)
    pstar = 1.0 / psum
    lo = n_groups + gstar * epg
    em = jnp.where(lane >= lo, jnp.where(lane < lo + epg, 1.0, 0.0), 0.0)
    l1 = jnp.where(em > 0.0, logit, neg)
    v1 = jnp.max(l1, axis=-1, keepdims=True)
    i1 = jnp.min(jnp.where(l1 == v1, lane, far), axis=-1, keepdims=True)
    l2 = jnp.where(lane == i1, neg, l1)
    v2 = jnp.max(l2, axis=-1, keepdims=True)
    i2 = jnp.min(jnp.where(l2 == v2, lane, far), axis=-1, keepdims=True)
    e2 = jnp.exp(v2 - v1)
    wt1 = pstar / (1.0 + e2)
    wt2 = pstar * e2 / (1.0 + e2)
    onehot = jnp.where(lane == gstar, 1.0, 0.0)
    within = _dot(tril_ref[...], onehot.astype(BF16))
    carry = cnt_ref[...]
    rank = jnp.sum(onehot * (within + carry), axis=-1, keepdims=True)
    cnt_ref[...] = carry + jnp.sum(onehot, axis=0, keepdims=True)
    meta = (jnp.where(lane == i1 - lo, wt1, 0.0) + jnp.where(lane == i2 - lo, wt2, 0.0)
            + jnp.where(lane == epg, gstar, 0.0) + jnp.where(lane == epg + 1, rank, 0.0))
    hx_ref[:, :d] = h
    hx_ref[:, d:] = meta
    meta_ref[...] = meta.T[0:SUBLANES]


def _route_call(x, g, sc, sh, wr, br, t_len, name):
    n, d = x.shape
    tm = _row_tile(n, t_len, 1024 if t_len % 1024 == 0 else 256)
    sc_op, sc_spec = _mod_operand(sc, t_len, tm)
    sh_op, sh_spec = _mod_operand(sh, t_len, tm)
    tril = jnp.asarray(np.tril(np.ones((tm, tm), np.float32), -1), BF16)
    return pl.pallas_call(
        functools.partial(_route_kernel, n_groups=MOE_GROUPS, epg=MOE_EPG),
        grid=(n // tm,),
        in_specs=[
            pl.BlockSpec((tm, d), lambda i: (i, 0)),
            pl.BlockSpec((1, d), lambda i: (0, 0)),
            sc_spec, sh_spec,
            pl.BlockSpec((d, LANES), lambda i: (0, 0)),
            pl.BlockSpec((1, LANES), lambda i: (0, 0)),
            pl.BlockSpec((tm, tm), lambda i: (0, 0)),
        ],
        out_specs=[
            pl.BlockSpec((tm, d + LANES), lambda i: (i, 0)),
            pl.BlockSpec((SUBLANES, tm), lambda i: (0, i)),
            pl.BlockSpec((1, LANES), lambda i: (0, 0)),
        ],
        out_shape=[
            jax.ShapeDtypeStruct((n, d + LANES), F32),
            jax.ShapeDtypeStruct((SUBLANES, n), F32),
            jax.ShapeDtypeStruct((1, LANES), F32),
        ],
        compiler_params=_cparams("arbitrary"),
        name=name + "_route",
    )(x, g.reshape(1, d), sc_op, sh_op, wr, br, tril)


def _row_gather(idx_ref, src_hbm, buf, sem, tile, slot, start, unrolled=False):
    groups = buf.shape[1]
    base = tile * (groups * SUBLANES)

    def run(s):
        def body(i, carry):
            for u in range(SUBLANES):
                idx = idx_ref[base + i * SUBLANES + u]
                cp = pltpu.make_async_copy(src_hbm.at[pl.ds(idx, 1)],
                                           buf.at[s, i, pl.ds(u, 1)], sem.at[s])
                if start:
                    cp.start(priority=u % 2)
                else:
                    cp.wait()
            return carry

        if unrolled:
            for i in range(groups):
                body(i, 0)
        else:
            lax.fori_loop(0, groups, body, 0)

    for s in range(2):
        @pl.when(slot == s)
        def _():
            run(s)


def _expert_kernel(src_ref, tgrp_ref, tval_ref, hx_hbm, w1_ref, w3_ref, w2_ref, y_ref, hbuf, sem,
                   *, epg, d):
    j = pl.program_id(0)
    rows = y_ref.shape[0]

    @pl.when(j == 0)
    def _():
        @pl.when(tval_ref[0] == 1)
        def _():
            _row_gather(src_ref, hx_hbm, hbuf, sem, 0, 0, True)

    @pl.when(j + 1 < pl.num_programs(0))
    def _():
        @pl.when(tval_ref[j + 1] == 1)
        def _():
            _row_gather(src_ref, hx_hbm, hbuf, sem, j + 1, (j + 1) % 2, True, unrolled=True)

    @pl.when(tval_ref[j] == 1)
    def _():
        slot = j % 2
        _row_gather(src_ref, hx_hbm, hbuf, sem, j, slot, False)
        tile = hbuf[slot].reshape(rows, hbuf.shape[-1])
        hb = tile[:, :d].astype(BF16)
        hids = []
        for e in range(epg):
            a = _dot(hb, w1_ref[0, e])
            b = _dot(hb, w3_ref[0, e])
            hids.append((_silu(a) * b * tile[:, d + e:d + e + 1]).astype(BF16))
        hid = jnp.concatenate(hids, axis=1)
        y_ref[...] = _dot(hid, w2_ref[0].reshape(hid.shape[1], d))

    @pl.when(tval_ref[j] == 0)
    def _():
        y_ref[...] = jnp.zeros_like(y_ref)


def _combine_kernel(*refs, final):
    if final:
        dest_ref, ys_hbm, x_ref, gate_ref, fg_ref, o_ref, gbuf, sem = refs
    else:
        dest_ref, ys_hbm, x_ref, gate_ref, o_ref, gbuf, sem = refs
    i = pl.program_id(0)
    rows = x_ref.shape[0]

    @pl.when(i == 0)
    def _():
        _row_gather(dest_ref, ys_hbm, gbuf, sem, 0, 0, True)

    @pl.when(i + 1 < pl.num_programs(0))
    def _():
        _row_gather(dest_ref, ys_hbm, gbuf, sem, i + 1, (i + 1) % 2, True, unrolled=True)

    slot = i % 2
    _row_gather(dest_ref, ys_hbm, gbuf, sem, i, slot, False)
    y = x_ref[...] + _mod_rows(gate_ref, rows) * gbuf[slot].reshape(rows, gbuf.shape[-1])
    if final:
        y = y * lax.rsqrt(jnp.mean(y * y, axis=-1, keepdims=True) + EPS) * fg_ref[...]
    o_ref[...] = y


def _invperm_kernel(dest_ref, src_ref):
    def clear(i, carry):
        src_ref[i] = 0
        return carry

    def place(i, carry):
        src_ref[dest_ref[i]] = i
        return carry

    lax.fori_loop(0, src_ref.shape[0], clear, 0, unroll=8)
    lax.fori_loop(0, dest_ref.shape[0], place, 0, unroll=8)


def _moe_call(x, g, sc, sh, gate, wr, br, w1, w3, w2, layer, final_g, t_len, name):
    n, d = x.shape
    epg, dff = w1.shape[1], w1.shape[3]
    n_grp = MOE_GROUPS
    hx, meta, cnt = _route_call(x, g, sc, sh, wr, br, t_len, name)

    tm2 = 512 if n >= 8192 else 128
    grp = meta[epg].astype(jnp.int32)
    rank = meta[epg + 1].astype(jnp.int32)
    counts = cnt[0, :n_grp].astype(jnp.int32)
    n_tiles_g = (counts + tm2 - 1) // tm2
    tile_end = jnp.cumsum(n_tiles_g)
    tile_start = tile_end - n_tiles_g
    dest = rank
    for gi in range(n_grp):
        dest = dest + jnp.where(grp == gi, tile_start[gi] * tm2, 0)
    n_tiles = n // tm2 + n_grp
    src = pl.pallas_call(
        _invperm_kernel,
        in_specs=[pl.BlockSpec(memory_space=pltpu.SMEM)],
        out_specs=pl.BlockSpec(memory_space=pltpu.SMEM),
        out_shape=jax.ShapeDtypeStruct((n_tiles * tm2,), jnp.int32),
        name=name + "_invperm",
    )(dest)
    jt = jnp.arange(n_tiles, dtype=jnp.int32)
    tval = (jt < tile_end[-1]).astype(jnp.int32)
    tgrp = jnp.sum((jt[:, None] >= tile_end[None, :]).astype(jnp.int32), axis=1)
    last_grp = jnp.sum((tile_end[-1] - 1 >= tile_end).astype(jnp.int32))
    tgrp = layer * n_grp + jnp.minimum(jnp.where(tval == 1, tgrp, last_grp), n_grp - 1)

    wspec = lambda shape: pl.BlockSpec(shape, lambda j, src, tgrp, tval: (tgrp[j], 0, 0, 0))
    ys = pl.pallas_call(
        functools.partial(_expert_kernel, epg=epg, d=d),
        grid_spec=pltpu.PrefetchScalarGridSpec(
            num_scalar_prefetch=3,
            grid=(n_tiles,),
            in_specs=[
                pl.BlockSpec(memory_space=pl.ANY),
                wspec((1, epg, d, dff)), wspec((1, epg, d, dff)), wspec((1, epg, dff, d)),
            ],
            out_specs=pl.BlockSpec((tm2, d), lambda j, src, tgrp, tval: (j, 0)),
            scratch_shapes=[pltpu.VMEM((2, tm2 // SUBLANES, SUBLANES, d + LANES), F32),
                            pltpu.SemaphoreType.DMA((2,))],
        ),
        out_shape=jax.ShapeDtypeStruct((n_tiles * tm2, d), F32),
        compiler_params=_cparams("arbitrary"),
        name=name + "_experts",
    )(src, tgrp, tval, hx, w1, w3, w2)

    tm = _row_tile(n, t_len, 512 if t_len % 512 == 0 else 256)
    gate_op, gate_spec = _mod_operand(gate, t_len, tm)
    final = final_g is not None
    in_specs = [
        pl.BlockSpec(memory_space=pl.ANY),
        pl.BlockSpec((tm, d), lambda i, dest: (i, 0)),
        gate_spec,
    ]
    args = [dest, ys, x, gate_op]
    if final:
        in_specs.append(pl.BlockSpec((1, d), lambda i, dest: (0, 0)))
        args.append(final_g.reshape(1, d))
    return pl.pallas_call(
        functools.partial(_combine_kernel, final=final),
        grid_spec=pltpu.PrefetchScalarGridSpec(
            num_scalar_prefetch=1,
            grid=(n // tm,),
            in_specs=in_specs,
            out_specs=pl.BlockSpec((tm, d), lambda i, dest: (i, 0)),
            scratch_shapes=[pltpu.VMEM((2, tm // SUBLANES, SUBLANES, d), F32),
                            pltpu.SemaphoreType.DMA((2,))],
        ),
        out_shape=jax.ShapeDtypeStruct((n, d), F32),
        compiler_params=_cparams("arbitrary"),
        name=name + "_combine",
    )(*args)


def _conv_taps(xm, halo, conv_w, conv_b, row_in_seq, axis):
    acc = conv_b + xm * conv_w[ML_CONV - 1]
    for s in range(1, ML_CONV):
        shifted = pltpu.roll(xm, s, axis)
        fill = pltpu.roll(halo, (s + SUBLANES - (ML_CONV - 1)) % SUBLANES, axis)
        if axis == 0:
            top = jnp.where(row_in_seq < s, fill, shifted[0:SUBLANES])
            shifted = jnp.concatenate([top, shifted[SUBLANES:]], axis=0)
        else:
            shifted = jnp.where(row_in_seq < s, fill, shifted)
        acc = acc + shifted * conv_w[ML_CONV - 1 - s]
    return acc


def _conv_qkv_kernel(xm_ref, c0_ref, cw_ref, cb_ref, wqk_ref, wv_ref, wg_ref, bg_ref,
                     q_ref, k_ref, v_ref, xc_ref, gates_ref, carry_scr, *, short_seq, k_scale):
    conv_w = [cw_ref[i:i + 1, :] for i in range(ML_CONV)]
    conv_b = cb_ref[...]
    if short_seq:
        xm3 = xm_ref[...]
        t_idx = lax.broadcasted_iota(jnp.int32, xm3.shape, 1)
        conv = _conv_taps(xm3, c0_ref[...], conv_w, conv_b, t_idx, 1)
        rows = xm3.shape[0] * xm3.shape[1]
        conv = conv.reshape(rows, xm3.shape[2])
        xm = xm3.reshape(rows, xm3.shape[2])
    else:
        t = pl.program_id(1)

        @pl.when(t == 0)
        def _():
            carry_scr[...] = c0_ref[0]

        xm = xm_ref[...]
        row8 = lax.broadcasted_iota(jnp.int32, (SUBLANES, xm.shape[1]), 0)
        conv = _conv_taps(xm, carry_scr[...], conv_w, conv_b, row8, 0)
        carry_scr[...] = pltpu.roll(xm[xm.shape[0] - SUBLANES:], ML_CONV - 1, 0)
    xc = _silu(conv)
    xc_ref[...] = xc
    inner = xm.shape[1]
    gates = bg_ref[...]
    xcb = xc.astype(BF16)
    xmb = xm.astype(BF16)
    for i in range(inner // MXU_DIM):
        cs = slice(i * MXU_DIM, (i + 1) * MXU_DIM)
        qk = _dot(xcb[:, cs], wqk_ref[i])
        qi = qk[:, :MXU_DIM]
        ki = qk[:, MXU_DIM:]
        vi = _dot(xmb[:, cs], wv_ref[i])
        q_ref[:, cs] = qi.astype(q_ref.dtype)
        k_ref[:, cs] = (ki * k_scale).astype(k_ref.dtype)
        v_ref[:, cs] = vi.astype(v_ref.dtype)
        gates = gates + _dot(qi.astype(BF16), wg_ref[0, cs, :])
        gates = gates + _dot(ki.astype(BF16), wg_ref[1, cs, :])
        gates = gates + _dot(vi.astype(BF16), wg_ref[2, cs, :])
    lane = lax.broadcasted_iota(jnp.int32, gates.shape, 1)
    log_sig = jnp.minimum(gates, 0.0) - jnp.log1p(jnp.exp(-jnp.abs(gates)))
    gates_ref[...] = jnp.where(lane < ML_HEADS, gates, log_sig)


def _conv_qkv_call(up, conv0, conv_w, conv_b, wqk, wv, wg, bg, b, t_len):
    n = up.shape[0]
    inner = up.shape[1] // 2
    short_seq = t_len == SUBLANES
    dh = inner // ML_HEADS
    k_scale = dh ** -0.5
    qkv_dtype = F32 if short_seq else BF16
    halo = jnp.zeros((b, SUBLANES, inner), F32)
    if conv0 is not None:
        halo = halo.at[:, :ML_CONV - 1].set(conv0)
    if short_seq:
        nb = min(32, b)
        grid = (b // nb,)
        up_op = up.reshape(b, t_len, 2 * inner)
        xm_spec = pl.BlockSpec((nb, t_len, inner), lambda i: (i, 0, 0))
        c0_spec = pl.BlockSpec((nb, SUBLANES, inner), lambda i: (i, 0, 0))
        rows = nb * t_len
        row_spec = lambda w: pl.BlockSpec((rows, w), lambda i: (i, 0))
        const2 = lambda s: pl.BlockSpec(s, lambda i: (0, 0))
        const3 = lambda s: pl.BlockSpec(s, lambda i: (0, 0, 0))
        sem = ("parallel",)
    else:
        tm = math.gcd(256, t_len)
        nt = t_len // tm
        grid = (b, nt)
        up_op = up
        xm_spec = pl.BlockSpec((tm, inner), lambda i, t: (i * nt + t, 0))
        c0_spec = pl.BlockSpec((1, SUBLANES, inner), lambda i, t: (i, 0, 0))
        rows = tm
        row_spec = lambda w: pl.BlockSpec((rows, w), lambda i, t: (i * nt + t, 0))
        const2 = lambda s: pl.BlockSpec(s, lambda i, t: (0, 0))
        const3 = lambda s: pl.BlockSpec(s, lambda i, t: (0, 0, 0))
        sem = ("parallel", "arbitrary")
    return pl.pallas_call(
        functools.partial(_conv_qkv_kernel, short_seq=short_seq, k_scale=k_scale),
        grid=grid,
        in_specs=[
            xm_spec, c0_spec,
            const2((ML_CONV, inner)), const2((1, inner)),
            const3(wqk.shape), const3(wv.shape), const3(wg.shape), const2((1, LANES)),
        ],
        out_specs=[row_spec(inner), row_spec(inner), row_spec(inner), row_spec(inner), row_spec(LANES)],
        out_shape=[
            jax.ShapeDtypeStruct((n, inner), qkv_dtype),
            jax.ShapeDtypeStruct((n, inner), qkv_dtype),
            jax.ShapeDtypeStruct((n, inner), qkv_dtype),
            jax.ShapeDtypeStruct((n, inner), F32),
            jax.ShapeDtypeStruct((n, LANES), F32),
        ],
        scratch_shapes=[pltpu.VMEM((SUBLANES, inner), F32)],
        compiler_params=_cparams(*sem),
        name="mlstm_conv_qkv",
    )(up_op, halo, conv_w, conv_b.reshape(1, inner), wqk, wv, wg, bg)


def _mlstm_kernel(*refs, has_state):
    if has_state:
        (q_ref, k_ref, v_ref, gates_ref, xc_ref, z_ref, nw_ref, sk_ref, c0_ref, n0_ref, m0_ref,
         o_ref, c_ref, n_ref, m_ref) = refs
    else:
        (q_ref, k_ref, v_ref, gates_ref, xc_ref, z_ref, nw_ref, sk_ref,
         o_ref, c_ref, n_ref, m_ref) = refs
    ch = pl.program_id(1)
    length = q_ref.shape[0]
    dh = q_ref.shape[1] // ML_HEADS

    @pl.when(ch == 0)
    def _():
        if has_state:
            c_ref[...] = c0_ref[...]
            n_ref[...] = n0_ref[...]
            m_ref[...] = m0_ref[...]
        else:
            c_ref[...] = jnp.zeros_like(c_ref)
            n_ref[...] = jnp.zeros_like(n_ref)
            m_ref[...] = jnp.zeros_like(m_ref)

    gates = gates_ref[...]
    gates_t = gates.T
    t_idx = lax.broadcasted_iota(jnp.int32, (length, length), 0)
    s_idx = lax.broadcasted_iota(jnp.int32, (length, length), 1)
    causal = s_idx <= t_idx
    m_all = m_ref[0]
    m_new = m_all
    lane = lax.broadcasted_iota(jnp.int32, m_all.shape, 1)
    for h in range(ML_HEADS):
        hs = slice(h * dh, (h + 1) * dh)
        ig_col = gates[:, h:h + 1]
        lf_col = gates[:, ML_HEADS + h:ML_HEADS + h + 1]
        ig_row = gates_t[h:h + 1, :]
        lf_row = gates_t[ML_HEADS + h:ML_HEADS + h + 1, :]
        b_col = jnp.sum(jnp.where(causal, lf_row, 0.0), axis=1, keepdims=True)
        b_row = jnp.sum(jnp.where(t_idx <= s_idx, lf_col, 0.0), axis=0, keepdims=True)
        m_prev = m_all[:, h:h + 1]
        dm = jnp.where(causal, b_col - b_row + ig_row, -jnp.inf)
        a = b_col + m_prev
        m_t = jnp.maximum(a, jnp.max(dm, axis=1, keepdims=True))
        w_c = jnp.exp(a - m_t)
        w = jnp.exp(dm - m_t)
        qh = q_ref[:, hs].astype(BF16)
        kh = k_ref[:, hs]
        vh = v_ref[:, hs].astype(BF16)
        c_h = c_ref[0, h]
        n_h = n_ref[0, :, hs]
        s = _dot_nt(qh, kh.astype(BF16)) * w
        num = w_c * _dot(qh, c_h.astype(BF16)) + _dot(s.astype(BF16), vh)
        den = w_c * jnp.sum(qh.astype(F32) * n_h, axis=-1, keepdims=True) + jnp.sum(s, axis=-1, keepdims=True)
        hc = num / jnp.maximum(jnp.abs(den), jnp.exp(-m_t))
        m_last = m_t[length - 1:length]
        b_last = b_col[length - 1:length]
        wl_c = jnp.exp(a[length - 1:length] - m_last)
        wl_col = jnp.exp(b_last - b_col + ig_col - m_last)
        kw = kh.astype(F32) * wl_col
        c_ref[0, h] = wl_c * c_h + _dot_tn(kw.astype(BF16), vh)
        n_ref[0, :, hs] = wl_c * n_h + jnp.sum(kw, axis=0, keepdims=True)
        m_new = jnp.where(lane == h, m_last, m_new)
        mu = jnp.mean(hc, axis=-1, keepdims=True)
        dev = hc - mu
        var = jnp.mean(dev * dev, axis=-1, keepdims=True)
        hn = dev * lax.rsqrt(var + EPS) * nw_ref[:, hs]
        ho = (hn + sk_ref[:, hs] * xc_ref[:, hs]) * _silu(z_ref[:, hs])
        o_ref[:, hs] = ho.astype(o_ref.dtype)
    m_ref[0] = m_new


def _mlstm_call(q, k, v, gates, xc, up, norm_w, skip, c0, n0, m0, b, t_len):
    n, inner = q.shape
    dh = inner // ML_HEADS
    has_state = c0 is not None
    length = math.gcd(t_len, ML_CHUNK)
    nc = t_len // length
    row = lambda w, col=0: pl.BlockSpec((length, w), lambda i, c: (i * nc + c, col))
    const = lambda w: pl.BlockSpec((1, w), lambda i, c: (0, 0))
    c_spec = pl.BlockSpec((1, ML_HEADS, dh, dh), lambda i, c: (i, 0, 0, 0))
    n_spec = pl.BlockSpec((1, 1, inner), lambda i, c: (i, 0, 0))
    m_spec = pl.BlockSpec((1, 1, LANES), lambda i, c: (i, 0, 0))
    in_specs = [row(inner), row(inner), row(inner), row(LANES), row(inner), row(inner, 1),
                const(inner), const(inner)]
    args = [q, k, v, gates, xc, up, norm_w.reshape(1, inner), skip.reshape(1, inner)]
    if has_state:
        in_specs += [c_spec, n_spec, m_spec]
        m0_pad = jnp.zeros((b, 1, LANES), F32).at[:, 0, :ML_HEADS].set(m0)
        args += [c0, n0.reshape(b, 1, inner), m0_pad]
    ho, c_t, n_t, m_t = pl.pallas_call(
        functools.partial(_mlstm_kernel, has_state=has_state),
        grid=(b, nc),
        in_specs=in_specs,
        out_specs=[row(inner), c_spec, n_spec, m_spec],
        out_shape=[
            jax.ShapeDtypeStruct((n, inner), BF16 if length % 16 == 0 else F32),
            jax.ShapeDtypeStruct((b, ML_HEADS, dh, dh), F32),
            jax.ShapeDtypeStruct((b, 1, inner), F32),
            jax.ShapeDtypeStruct((b, 1, LANES), F32),
        ],
        compiler_params=_cparams("parallel", "arbitrary"),
        name="mlstm_scan",
    )(*args)
    return ho, c_t, n_t.reshape(b, ML_HEADS, dh), m_t[:, 0, :ML_HEADS]


def _block_diag_tiles(w):
    n_blk, blk, _ = w.shape
    per = MXU_DIM // blk
    rows = w.reshape(n_blk // per, per, blk, blk).transpose(0, 1, 3, 2).reshape(n_blk // per, MXU_DIM, blk)
    r_blk = np.arange(MXU_DIM)[:, None] // blk
    c_blk = np.arange(MXU_DIM)[None, :] // blk
    same_block = jnp.asarray((r_blk == c_blk).astype(np.float32))
    return jnp.tile(rows, (1, 1, per)) * same_block


def _prep_weights(p):
    w = {}
    w['hg_win'] = p['hg_win'].astype(BF16)
    w['hg_wo'] = p['hg_wo'].astype(BF16)
    w['ml_wup'] = p['ml_wup'].astype(BF16)
    w['ml_wdown'] = p['ml_wdown'].astype(BF16)
    dep, n_exp, d_model, dff = p['moe_w1'].shape
    n_stack = dep * n_exp // MOE_EPG
    w['moe_w1'] = p['moe_w1'].astype(BF16).reshape(n_stack, MOE_EPG, d_model, dff)
    w['moe_w3'] = p['moe_w3'].astype(BF16).reshape(n_stack, MOE_EPG, d_model, dff)
    w['moe_w2'] = p['moe_w2'].astype(BF16).reshape(n_stack, MOE_EPG, dff, d_model)
    depth, d, g = p['moe_wrg'].shape
    n_exp = g * p['moe_wre'].shape[-1]
    wr = jnp.zeros((depth, d, LANES), F32)
    wr = wr.at[:, :, :g].set(p['moe_wrg']).at[:, :, g:g + n_exp].set(p['moe_wre'].reshape(depth, d, n_exp))
    w['moe_wr'] = wr.astype(BF16)
    br = jnp.zeros((depth, 1, LANES), F32)
    br = br.at[:, 0, :g].set(p['moe_brg']).at[:, 0, g:g + n_exp].set(p['moe_bre'].reshape(depth, n_exp))
    w['moe_br'] = br
    n_b = p['ml_wq'].shape[0]
    wq = jnp.stack([_block_diag_tiles(p['ml_wq'][j]) for j in range(n_b)])
    wk = jnp.stack([_block_diag_tiles(p['ml_wk'][j]) for j in range(n_b)])
    wv = jnp.stack([_block_diag_tiles(p['ml_wv'][j]) for j in range(n_b)])
    w['ml_wqk'] = jnp.concatenate([wq, wk], axis=-1).astype(BF16)
    w['ml_wv'] = wv.astype(BF16)
    inner = p['ml_conv_b'].shape[-1]
    wg = jnp.zeros((n_b, 3 * inner, LANES), F32)
    wg = wg.at[:, :, :ML_HEADS].set(p['ml_wig']).at[:, :, ML_HEADS:2 * ML_HEADS].set(p['ml_wfg'])
    w['ml_wg'] = wg.reshape(n_b, 3, inner, LANES).astype(BF16)
    bg = jnp.zeros((n_b, 1, LANES), F32)
    bg = bg.at[:, 0, :ML_HEADS].set(p['ml_big']).at[:, 0, ML_HEADS:2 * ML_HEADS].set(p['ml_bfg'])
    w['ml_bg'] = bg
    return w


def _lb_kernel(lb_ref, o_ref):
    x = lb_ref[...]
    mx = jnp.max(x, axis=0, keepdims=True)
    ex = jnp.exp(x - mx)
    sm = ex / jnp.sum(ex, axis=0, keepdims=True)
    rows = []
    run = jnp.zeros_like(sm[0:1])
    for i in range(x.shape[0]):
        run = run + sm[i:i + 1]
        rows.append(run)
    o_ref[...] = jnp.concatenate(rows, axis=0)


def _lb_call(hg_lb):
    return pl.pallas_call(
        _lb_kernel,
        out_shape=jax.ShapeDtypeStruct(hg_lb.shape, F32),
        name="hgrn_lower_bound",
    )(hg_lb)


def _trunk(x3, mods, s_hg, s_c, s_n, s_m, s_conv, p, w, lb_all):
    b, t_len, d = x3.shape
    n = b * t_len
    x = x3.reshape(n, d)
    depth = p['norm_g'].shape[0]
    new_hg, new_c, new_n, new_m, new_conv = [], [], [], [], []
    for l in range(depth):
        sh1, sc1, g1, sh2, sc2, g2 = [mods[l][:, i * d:(i + 1) * d] for i in range(6)]
        if l % 2 == 0:
            a = l // 2
            proj = _norm_mm_call(x, p['norm_g'][l, 0], sc1, sh1, w['hg_win'][a], t_len, "hgrn_in_proj")
            o, s_t = _gla_call(proj, lb_all[l], p['hg_norm'][a], None if s_hg is None else s_hg[a], b, t_len)
            new_hg.append(s_t)
            x = _mm_res_call(o, w['hg_wo'][a], x, g1, t_len, "hgrn_out_proj")
        else:
            j = l // 2
            up = _norm_mm_call(x, p['norm_g'][l, 0], sc1, sh1, w['ml_wup'][j], t_len, "mlstm_up_proj")
            inner = up.shape[1] // 2
            q, k, v, xc, gates = _conv_qkv_call(
                up, None if s_conv is None else s_conv[j], p['ml_conv_w'][j], p['ml_conv_b'][j],
                w['ml_wqk'][j], w['ml_wv'][j], w['ml_wg'][j], w['ml_bg'][j], b, t_len)
            ho, c_t, n_t, m_t = _mlstm_call(
                q, k, v, gates, xc, up, p['ml_norm'][j], p['ml_skip'][j],
                None if s_c is None else s_c[j], None if s_n is None else s_n[j],
                None if s_m is None else s_m[j], b, t_len)
            new_c.append(c_t)
            new_n.append(n_t)
            new_m.append(m_t)
            new_conv.append(up.reshape(b, t_len, 2 * inner)[:, t_len - (ML_CONV - 1):, :inner])
            x = _mm_res_call(ho.astype(BF16), w['ml_wdown'][j], x, g1, t_len, "mlstm_down_proj")
        x = _moe_call(x, p['norm_g'][l, 1], sc2, sh2, g2, w['moe_wr'][l], w['moe_br'][l],
                      w['moe_w1'], w['moe_w3'], w['moe_w2'], l,
                      p['final_g'] if l == depth - 1 else None, t_len, "moe_layer%d" % l)
    return (x.reshape(b, t_len, d), jnp.stack(new_hg), jnp.stack(new_c), jnp.stack(new_n),
            jnp.stack(new_m), jnp.stack(new_conv))


def kernel(x_prompt, x_sample, c_prompt, c_sample, state_hgrn, state_mlstm_c, state_mlstm_n, state_mlstm_m, state_conv, w_ada, b_ada, norm_g, final_g, hg_win, hg_wo, hg_norm, hg_lb, ml_wup, ml_conv_w, ml_conv_b, ml_wq, ml_wk, ml_wv, ml_wig, ml_big, ml_wfg, ml_bfg, ml_norm, ml_skip, ml_wdown, moe_wrg, moe_brg, moe_wre, moe_bre, moe_w1, moe_w3, moe_w2):
    p = dict(w_ada=w_ada, b_ada=b_ada, norm_g=norm_g, final_g=final_g,
             hg_win=hg_win, hg_wo=hg_wo, hg_norm=hg_norm, hg_lb=hg_lb,
             ml_wup=ml_wup, ml_conv_w=ml_conv_w, ml_conv_b=ml_conv_b, ml_wq=ml_wq, ml_wk=ml_wk, ml_wv=ml_wv,
             ml_wig=ml_wig, ml_big=ml_big, ml_wfg=ml_wfg, ml_bfg=ml_bfg, ml_norm=ml_norm, ml_skip=ml_skip,
             ml_wdown=ml_wdown, moe_wrg=moe_wrg, moe_brg=moe_brg, moe_wre=moe_wre, moe_bre=moe_bre,
             moe_w1=moe_w1, moe_w3=moe_w3, moe_w2=moe_w2)
    w = _prep_weights(p)
    lb_all = _lb_call(hg_lb)
    bp = x_prompt.shape[0]
    c_all = jnp.concatenate([c_prompt, c_sample], axis=0)
    mod_all = _ada_call(c_all, w_ada, b_ada)
    mods_p = [mod_all[l, :bp] for l in range(mod_all.shape[0])]
    mods_s = [mod_all[l, bp:] for l in range(mod_all.shape[0])]
    y_p, hg_p, mc_p, mn_p, mm_p, conv_p = _trunk(x_prompt, mods_p, None, None, None, None, None, p, w, lb_all)
    y_s, hg_s, mc_s, mn_s, mm_s, conv_s = _trunk(x_sample, mods_s, state_hgrn, state_mlstm_c, state_mlstm_n,
                                                 state_mlstm_m, state_conv, p, w, lb_all)
    return (y_p, y_s, hg_p, mc_p, mn_p, mm_p, conv_p, hg_s, mc_s, mn_s, mm_s, conv_s)
```

```python
import functools
import math

import numpy as np
import jax
import jax.numpy as jnp
from jax import lax
from jax.experimental import pallas as pl
from jax.experimental.pallas import tpu as pltpu

F32 = jnp.float32
BF16 = jnp.bfloat16
EPS = 1e-6

HG_DK = 128
ML_HEADS = 4
ML_CONV = 4
ML_QKV_BLOCK = 4
MOE_GROUPS = 4
MOE_EPG = 4
CHUNK = 64
GLA_CHUNKS_PER_STEP = 8
GLA_HEAD_GROUP = 8
ML_CHUNK = 256

LANES = 128
SUBLANES = 8
MXU_DIM = 256
VMEM_LIMIT_BYTES = 56 * 1024 * 1024


def _cparams(*sem):
    return pltpu.CompilerParams(dimension_semantics=sem, vmem_limit_bytes=VMEM_LIMIT_BYTES)


def _silu(x):
    return x * jax.nn.sigmoid(x)


def _dot(a, b):
    return jnp.dot(a, b, preferred_element_type=F32)


def _dot_nt(a, b):
    return lax.dot_general(a, b, (((1,), (1,)), ((), ())), preferred_element_type=F32)


def _dot_tn(a, b):
    return lax.dot_general(a, b, (((0,), (0,)), ((), ())), preferred_element_type=F32)


def _rms_mod(x, g, sc, sh):
    ms = jnp.mean(x * x, axis=-1, keepdims=True)
    h = x * lax.rsqrt(ms + EPS) * g
    return h * (1.0 + sc) + sh


def _ada_kernel(c_ref, w_ref, b_ref, o_ref):
    cm = _silu(c_ref[...]).astype(BF16)
    o_ref[0] = _dot(cm, w_ref[0].astype(BF16)) + b_ref[0]


def _ada_call(c_all, w_ada, b_ada):
    depth, d, n_out = w_ada.shape
    m = c_all.shape[0]
    tn = 512
    return pl.pallas_call(
        _ada_kernel,
        grid=(depth, n_out // tn),
        in_specs=[
            pl.BlockSpec((m, d), lambda l, j: (0, 0)),
            pl.BlockSpec((1, d, tn), lambda l, j: (l, 0, j)),
            pl.BlockSpec((1, 1, tn), lambda l, j: (l, 0, j)),
        ],
        out_specs=pl.BlockSpec((1, m, tn), lambda l, j: (l, 0, j)),
        out_shape=jax.ShapeDtypeStruct((depth, m, n_out), F32),
        compiler_params=_cparams("parallel", "parallel"),
        name="ada_mod",
    )(c_all, w_ada, b_ada.reshape(depth, 1, n_out))


def _mod_operand(m, t_len, tm):
    b, d = m.shape
    if t_len % tm == 0:
        per_b = t_len // tm
        return m.reshape(b, 1, d), pl.BlockSpec((1, 1, d), lambda i, *_: (i // per_b, 0, 0))
    assert tm % t_len == 0
    nb = tm // t_len
    return m.reshape(b // nb, nb, 1, d), pl.BlockSpec((1, nb, 1, d), lambda i, *_: (i, 0, 0, 0))


def _mod_rows(ref, rows):
    v = ref[0]
    if v.ndim == 2:
        return v
    nb, _, d = v.shape
    return jnp.broadcast_to(v, (nb, rows // nb, d)).reshape(rows, d)


def _row_tile(n, t_len, target):
    tm = min(target, n)
    while n % tm or (t_len % tm and tm % t_len):
        tm //= 2
    return tm


def _norm_mm_kernel(x_ref, g_ref, sc_ref, sh_ref, w_ref, o_ref, *, col_chunk):
    rows = x_ref.shape[0]
    hb = _rms_mod(x_ref[...], g_ref[...], _mod_rows(sc_ref, rows), _mod_rows(sh_ref, rows)).astype(BF16)
    for c0 in range(0, o_ref.shape[1], col_chunk):
        o_ref[:, c0:c0 + col_chunk] = _dot(hb, w_ref[:, c0:c0 + col_chunk])


def _norm_mm_call(x, g, sc, sh, w, t_len, name):
    n, d = x.shape
    n_out = w.shape[1]
    tm = _row_tile(n, t_len, 512)
    sc_op, sc_spec = _mod_operand(sc, t_len, tm)
    sh_op, sh_spec = _mod_operand(sh, t_len, tm)
    return pl.pallas_call(
        functools.partial(_norm_mm_kernel, col_chunk=512),
        grid=(n // tm,),
        in_specs=[
            pl.BlockSpec((tm, d), lambda i: (i, 0)),
            pl.BlockSpec((1, d), lambda i: (0, 0)),
            sc_spec, sh_spec,
            pl.BlockSpec((d, n_out), lambda i: (0, 0)),
        ],
        out_specs=pl.BlockSpec((tm, n_out), lambda i: (i, 0)),
        out_shape=jax.ShapeDtypeStruct((n, n_out), F32),
        compiler_params=_cparams("parallel"),
        name=name,
    )(x, g.reshape(1, d), sc_op, sh_op, w)


def _mm_res_kernel(a_ref, w_ref, x_ref, gate_ref, o_ref):
    o_ref[...] = x_ref[...] + _mod_rows(gate_ref, x_ref.shape[0]) * _dot(a_ref[...], w_ref[...])


def _mm_res_call(a, w, x, gate, t_len, name):
    n, k = a.shape
    d = w.shape[1]
    tm = _row_tile(n, t_len, 512)
    gate_op, gate_spec = _mod_operand(gate, t_len, tm)
    return pl.pallas_call(
        _mm_res_kernel,
        grid=(n // tm,),
        in_specs=[
            pl.BlockSpec((tm, k), lambda i: (i, 0)),
            pl.BlockSpec((k, d), lambda i: (0, 0)),
            pl.BlockSpec((tm, d), lambda i: (i, 0)),
            gate_spec,
        ],
        out_specs=pl.BlockSpec((tm, d), lambda i: (i, 0)),
        out_shape=jax.ShapeDtypeStruct((n, d), F32),
        compiler_params=_cparams("parallel"),
        name=name,
    )(a, w, x, gate_op)


def _gla_tables(t_sub, nseq):
    r = t_sub * nseq
    levels = []
    m = t_sub // 2
    while m >= 1:
        levels.append(m)
        m //= 2
    n_lev = len(levels)
    tril = np.zeros((r, r), np.float32)
    mask = np.zeros((n_lev + 1, r, r), np.float32)
    for li, m in enumerate(levels):
        for row in range(r):
            blk = (row // (2 * m)) * 2 * m
            if row - blk >= m:
                mask[li, row, blk:blk + m] = 1.0
    for row in range(r):
        s0 = (row // t_sub) * t_sub
        tril[row, s0:row + 1] = 1.0
        mask[n_lev, row, row] = 1.0
    return tril, mask, levels


def _bcast_block_row(b, block, row_in_block):
    parts = [jnp.broadcast_to(b[s + row_in_block:s + row_in_block + 1, :], (block, b.shape[1]))
             for s in range(0, b.shape[0], block)]
    return parts[0] if len(parts) == 1 else jnp.concatenate(parts, axis=0)


def _level_decay(b, m):
    r = b.shape[0]
    pos = lax.broadcasted_iota(jnp.int32, b.shape, 0) & (2 * m - 1)
    if 2 * m >= SUBLANES:
        b_mid = _bcast_block_row(b, 2 * m, m - 1)
    else:
        b_mid = b
        for p in range(2 * m):
            if p != m - 1:
                b_mid = jnp.where(pos == p, pltpu.roll(b, (p - (m - 1)) % r, 0), b_mid)
    return jnp.exp2(jnp.where(pos >= m, b - b_mid, b_mid - b))


def _gla_kernel(*refs, t_sub, nseq, n_ch, levels, n_heads, has_state):
    if has_state:
        proj_ref, lb_ref, gn_ref, tril_ref, mask_ref, s0_ref, o_ref, sout_ref, st_ref = refs
    else:
        proj_ref, lb_ref, gn_ref, tril_ref, mask_ref, o_ref, sout_ref, st_ref = refs
    r = t_sub * nseq
    n_lev = len(levels)
    dk = HG_DK
    hk = n_heads * dk
    c = pl.program_id(1)

    @pl.when(c == 0)
    def _():
        if has_state:
            for j in range(nseq):
                for h in range(n_heads):
                    st_ref[j, h] = s0_ref[j, h].T
        else:
            st_ref[...] = jnp.zeros_like(st_ref)

    gn = gn_ref[...]
    tril = tril_ref[...]
    for ci in range(n_ch):
        rows = slice(ci * r, (ci + 1) * r)
        for g0 in range(0, n_heads, GLA_HEAD_GROUP):
            gw = GLA_HEAD_GROUP * dk
            gs = slice(g0 * dk, g0 * dk + gw)
            lb = lb_ref[:, gs]
            zq = proj_ref[rows, g0 * dk:g0 * dk + gw]
            zf = proj_ref[rows, hk + g0 * dk:hk + g0 * dk + gw]
            f = lb + (1.0 - lb) * jax.nn.sigmoid(zf)
            lf = jnp.log2(f)
            q = _silu(zq).astype(BF16)
            k = (1.0 - f).astype(BF16)

            p0 = lf.astype(BF16)
            r1 = lf - p0.astype(F32)
            p1 = r1.astype(BF16)
            p2 = (r1 - p1.astype(F32)).astype(BF16)
            b = _dot(tril, p0) + _dot(tril, p1) + _dot(tril, p2)
            e_cum = jnp.exp2(b)
            e_cum_b = e_cum.astype(BF16)
            e_end_b = jnp.exp2(_bcast_block_row(b, t_sub, t_sub - 1) - b).astype(BF16)
            zs = [_level_decay(b, m).astype(BF16) for m in levels]

            for hh in range(GLA_HEAD_GROUP):
                h = g0 + hh
                hs = slice(h * dk, (h + 1) * dk)
                ls = slice(hh * dk, (hh + 1) * dk)
                qh = q[:, ls]
                kh = k[:, ls]
                vh = proj_ref[rows, 2 * hk + h * dk:2 * hk + (h + 1) * dk]
                a = _dot_nt(qh, kh) * mask_ref[n_lev]
                for li in range(n_lev):
                    z = zs[li][:, ls]
                    a = a + _dot_nt(qh * z, kh * z) * mask_ref[li]
                vb = vh.astype(BF16)
                o_intra = _dot(a.astype(BF16), vb)
                qd = qh * e_cum_b[:, ls]
                kd = kh * e_end_b[:, ls]
                o_parts = []
                for j in range(nseq):
                    rs = slice(j * t_sub, (j + 1) * t_sub)
                    st = st_ref[j, h]
                    o_parts.append(_dot_nt(qd[rs], st.astype(BF16)))
                    d_last = e_cum[(j + 1) * t_sub - 1:(j + 1) * t_sub, ls]
                    st_ref[j, h] = st * d_last + _dot_tn(vb[rs], kd[rs])
                o_inter = o_parts[0] if nseq == 1 else jnp.concatenate(o_parts, axis=0)
                o = o_intra + o_inter
                o = o * lax.rsqrt(jnp.mean(o * o, axis=-1, keepdims=True) + EPS) * gn
                zg = proj_ref[rows, 3 * hk + h * dk:3 * hk + (h + 1) * dk]
                o_ref[rows, hs] = (o * _silu(zg)).astype(BF16)

    @pl.when(c == pl.num_programs(1) - 1)
    def _():
        for j in range(nseq):
            for h in range(n_heads):
                sout_ref[j, h] = st_ref[j, h].T


def _gla_call(proj, lb, gn, s0, b, t_len):
    n = proj.shape[0]
    hk = lb.shape[-1]
    n_heads = hk // HG_DK
    has_state = s0 is not None
    if t_len % CHUNK == 0:
        t_sub, nseq = CHUNK, 1
    else:
        t_sub, nseq = t_len, CHUNK // t_len
        assert t_sub * nseq == CHUNK and b % nseq == 0
    r = t_sub * nseq
    tril, mask, levels = _gla_tables(t_sub, nseq)
    n_ch = math.gcd(GLA_CHUNKS_PER_STEP, t_len // t_sub)
    n_outer = b // nseq
    n_inner = t_len // (t_sub * n_ch)
    in_specs = [
        pl.BlockSpec((n_ch * r, proj.shape[1]), lambda i, c: (i * n_inner + c, 0)),
        pl.BlockSpec((1, hk), lambda i, c: (0, 0)),
        pl.BlockSpec((1, HG_DK), lambda i, c: (0, 0)),
        pl.BlockSpec(tril.shape, lambda i, c: (0, 0)),
        pl.BlockSpec(mask.shape, lambda i, c: (0, 0, 0)),
    ]
    args = [proj, lb.reshape(1, hk), gn.reshape(1, HG_DK), jnp.asarray(tril, BF16), jnp.asarray(mask, F32)]
    s_spec = pl.BlockSpec((nseq, n_heads, HG_DK, HG_DK), lambda i, c: (i, 0, 0, 0))
    if has_state:
        in_specs.append(s_spec)
        args.append(s0)
    o, s_out = pl.pallas_call(
        functools.partial(_gla_kernel, t_sub=t_sub, nseq=nseq, n_ch=n_ch, levels=tuple(levels),
                          n_heads=n_heads, has_state=has_state),
        grid=(n_outer, n_inner),
        in_specs=in_specs,
        out_specs=[
            pl.BlockSpec((n_ch * r, hk), lambda i, c: (i * n_inner + c, 0)),
            s_spec,
        ],
        out_shape=[
            jax.ShapeDtypeStruct((n, hk), BF16),
            jax.ShapeDtypeStruct((b, n_heads, HG_DK, HG_DK), F32),
        ],
        scratch_shapes=[pltpu.VMEM((nseq, n_heads, HG_DK, HG_DK), F32)],
        compiler_params=_cparams("parallel", "arbitrary"),
        name="gla_scan",
    )(*args)
    return o, s_out


def _route_kernel(x_ref, g_ref, sc_ref, sh_ref, wr_ref, br_ref, tril_ref, hx_ref, meta_ref, cnt_ref,
                  *, n_groups, epg):
    d = x_ref.shape[1]
    neg = -jnp.inf
    far = float(LANES)

    @pl.when(pl.program_id(0) == 0)
    def _():
        cnt_ref[...] = jnp.zeros_like(cnt_ref)

    h = _rms_mod(x_ref[...], g_ref[...], _mod_rows(sc_ref, x_ref.shape[0]), _mod_rows(sh_ref, x_ref.shape[0]))
    logit = _dot(h.astype(BF16), wr_ref[...]) + br_ref[...]
    lane = lax.broadcasted_iota(jnp.int32, logit.shape, 1).astype(F32)
    gm = lane < n_groups
    gmax = jnp.max(jnp.where(gm, logit, neg), axis=-1, keepdims=True)
    gstar = jnp.min(jnp.where(gm, jnp.where(logit == gmax, lane, far), far), axis=-1, keepdims=True)
    psum = jnp.sum(jnp.where(gm, jnp.exp(logit - gmax), 0.0), axis=-1, keepdims=True)
    pstar = 1.0 / psum
    lo = n_groups + gstar * epg
    em = jnp.where(lane >= lo, jnp.where(lane < lo + epg, 1.0, 0.0), 0.0)
    l1 = jnp.where(em > 0.0, logit, neg)
    v1 = jnp.max(l1, axis=-1, keepdims=True)
    i1 = jnp.min(jnp.where(l1 == v1, lane, far), axis=-1, keepdims=True)
    l2 = jnp.where(lane == i1, neg, l1)
    v2 = jnp.max(l2, axis=-1, keepdims=True)
    i2 = jnp.min(jnp.where(l2 == v2, lane, far), axis=-1, keepdims=True)
    e2 = jnp.exp(v2 - v1)
    wt1 = pstar / (1.0 + e2)
    wt2 = pstar * e2 / (1.0 + e2)
    onehot = jnp.where(lane == gstar, 1.0, 0.0)
    within = _dot(tril_ref[...], onehot.astype(BF16))
    carry = cnt_ref[...]
    rank = jnp.sum(onehot * (within + carry), axis=-1, keepdims=True)
    cnt_ref[...] = carry + jnp.sum(onehot, axis=0, keepdims=True)
    meta = (jnp.where(lane == i1 - lo, wt1, 0.0) + jnp.where(lane == i2 - lo, wt2, 0.0)
            + jnp.where(lane == epg, gstar, 0.0) + jnp.where(lane == epg + 1, rank, 0.0))
    hx_ref[:, :d] = h
    hx_ref[:, d:] = meta
    meta_ref[...] = meta.T[0:SUBLANES]


def _route_call(x, g, sc, sh, wr, br, t_len, name):
    n, d = x.shape
    tm = _row_tile(n, t_len, 1024 if t_len % 1024 == 0 else 256)
    sc_op, sc_spec = _mod_operand(sc, t_len, tm)
    sh_op, sh_spec = _mod_operand(sh, t_len, tm)
    tril = jnp.asarray(np.tril(np.ones((tm, tm), np.float32), -1), BF16)
    return pl.pallas_call(
        functools.partial(_route_kernel, n_groups=MOE_GROUPS, epg=MOE_EPG),
        grid=(n // tm,),
        in_specs=[
            pl.BlockSpec((tm, d), lambda i: (i, 0)),
            pl.BlockSpec((1, d), lambda i: (0, 0)),
            sc_spec, sh_spec,
            pl.BlockSpec((d, LANES), lambda i: (0, 0)),
            pl.BlockSpec((1, LANES), lambda i: (0, 0)),
            pl.BlockSpec((tm, tm), lambda i: (0, 0)),
        ],
        out_specs=[
            pl.BlockSpec((tm, d + LANES), lambda i: (i, 0)),
            pl.BlockSpec((SUBLANES, tm), lambda i: (0, i)),
            pl.BlockSpec((1, LANES), lambda i: (0, 0)),
        ],
        out_shape=[
            jax.ShapeDtypeStruct((n, d + LANES), F32),
            jax.ShapeDtypeStruct((SUBLANES, n), F32),
            jax.ShapeDtypeStruct((1, LANES), F32),
        ],
        compiler_params=_cparams("arbitrary"),
        name=name + "_route",
    )(x, g.reshape(1, d), sc_op, sh_op, wr, br, tril)


def _row_gather(idx_ref, src_hbm, buf, sem, tile, slot, start, unrolled=False):
    groups = buf.shape[1]
    base = tile * (groups * SUBLANES)

    def run(s):
        def body(i, carry):
            for u in range(SUBLANES):
                idx = idx_ref[base + i * SUBLANES + u]
                cp = pltpu.make_async_copy(src_hbm.at[pl.ds(idx, 1)],
                                           buf.at[s, i, pl.ds(u, 1)], sem.at[s])
                if start:
                    cp.start(priority=u % 2)
                else:
                    cp.wait()
            return carry

        if unrolled:
            for i in range(groups):
                body(i, 0)
        else:
            lax.fori_loop(0, groups, body, 0)

    for s in range(2):
        @pl.when(slot == s)
        def _():
            run(s)


def _expert_kernel(src_ref, tgrp_ref, tval_ref, hx_hbm, w1_ref, w3_ref, w2_ref, y_ref, hbuf, sem,
                   wb1, wb3, wb2, *, epg, d):
    j = pl.program_id(0)
    rows = y_ref.shape[0]

    @pl.when(jnp.logical_or(j == 0, tgrp_ref[j] != tgrp_ref[jnp.maximum(j - 1, 0)]))
    def _():
        for e in range(epg):
            wb1[e] = w1_ref[0, e].astype(BF16)
            wb3[e] = w3_ref[0, e].astype(BF16)
            wb2[e] = w2_ref[0, e].astype(BF16)

    @pl.when(j == 0)
    def _():
        @pl.when(tval_ref[0] == 1)
        def _():
            _row_gather(src_ref, hx_hbm, hbuf, sem, 0, 0, True)

    @pl.when(j + 1 < pl.num_programs(0))
    def _():
        @pl.when(tval_ref[j + 1] == 1)
        def _():
            _row_gather(src_ref, hx_hbm, hbuf, sem, j + 1, (j + 1) % 2, True, unrolled=True)

    @pl.when(tval_ref[j] == 1)
    def _():
        slot = j % 2
        _row_gather(src_ref, hx_hbm, hbuf, sem, j, slot, False)
        tile = hbuf[slot].reshape(rows, hbuf.shape[-1])
        hb = tile[:, :d].astype(BF16)
        hids = []
        for e in range(epg):
            a = _dot(hb, wb1[e])
            b = _dot(hb, wb3[e])
            hids.append((_silu(a) * b * tile[:, d + e:d + e + 1]).astype(BF16))
        hid = jnp.concatenate(hids, axis=1)
        y_ref[...] = _dot(hid, wb2[...].reshape(hid.shape[1], d))

    @pl.when(tval_ref[j] == 0)
    def _():
        y_ref[...] = jnp.zeros_like(y_ref)


def _combine_kernel(*refs, final):
    if final:
        dest_ref, ys_hbm, x_ref, gate_ref, fg_ref, o_ref, gbuf, sem = refs
    else:
        dest_ref, ys_hbm, x_ref, gate_ref, o_ref, gbuf, sem = refs
    i = pl.program_id(0)
    rows = x_ref.shape[0]

    @pl.when(i == 0)
    def _():
        _row_gather(dest_ref, ys_hbm, gbuf, sem, 0, 0, True)

    @pl.when(i + 1 < pl.num_programs(0))
    def _():
        _row_gather(dest_ref, ys_hbm, gbuf, sem, i + 1, (i + 1) % 2, True, unrolled=True)

    slot = i % 2
    _row_gather(dest_ref, ys_hbm, gbuf, sem, i, slot, False)
    y = x_ref[...] + _mod_rows(gate_ref, rows) * gbuf[slot].reshape(rows, gbuf.shape[-1])
    if final:
        y = y * lax.rsqrt(jnp.mean(y * y, axis=-1, keepdims=True) + EPS) * fg_ref[...]
    o_ref[...] = y


def _invperm_kernel(dest_ref, src_ref):
    def clear(i, carry):
        src_ref[i] = 0
        return carry

    def place(i, carry):
        src_ref[dest_ref[i]] = i
        return carry

    lax.fori_loop(0, src_ref.shape[0], clear, 0, unroll=8)
    lax.fori_loop(0, dest_ref.shape[0], place, 0, unroll=8)


def _moe_call(x, g, sc, sh, gate, wr, br, w1, w3, w2, layer, final_g, t_len, name):
    n, d = x.shape
    epg, dff = w1.shape[1], w1.shape[3]
    n_grp = MOE_GROUPS
    hx, meta, cnt = _route_call(x, g, sc, sh, wr, br, t_len, name)

    tm2 = 512 if n >= 8192 else 128
    grp = meta[epg].astype(jnp.int32)
    rank = meta[epg + 1].astype(jnp.int32)
    counts = cnt[0, :n_grp].astype(jnp.int32)
    n_tiles_g = (counts + tm2 - 1) // tm2
    tile_end = jnp.cumsum(n_tiles_g)
    tile_start = tile_end - n_tiles_g
    dest = rank
    for gi in range(n_grp):
        dest = dest + jnp.where(grp == gi, tile_start[gi] * tm2, 0)
    n_tiles = n // tm2 + n_grp
    src = pl.pallas_call(
        _invperm_kernel,
        in_specs=[pl.BlockSpec(memory_space=pltpu.SMEM)],
        out_specs=pl.BlockSpec(memory_space=pltpu.SMEM),
        out_shape=jax.ShapeDtypeStruct((n_tiles * tm2,), jnp.int32),
        name=name + "_invperm",
    )(dest)
    jt = jnp.arange(n_tiles, dtype=jnp.int32)
    tval = (jt < tile_end[-1]).astype(jnp.int32)
    tgrp = jnp.sum((jt[:, None] >= tile_end[None, :]).astype(jnp.int32), axis=1)
    last_grp = jnp.sum((tile_end[-1] - 1 >= tile_end).astype(jnp.int32))
    tgrp = layer * n_grp + jnp.minimum(jnp.where(tval == 1, tgrp, last_grp), n_grp - 1)

    wspec = lambda shape: pl.BlockSpec(shape, lambda j, src, tgrp, tval: (tgrp[j], 0, 0, 0))
    ys = pl.pallas_call(
        functools.partial(_expert_kernel, epg=epg, d=d),
        grid_spec=pltpu.PrefetchScalarGridSpec(
            num_scalar_prefetch=3,
            grid=(n_tiles,),
            in_specs=[
                pl.BlockSpec(memory_space=pl.ANY),
                wspec((1, epg, d, dff)), wspec((1, epg, d, dff)), wspec((1, epg, dff, d)),
            ],
            out_specs=pl.BlockSpec((tm2, d), lambda j, src, tgrp, tval: (j, 0)),
            scratch_shapes=[pltpu.VMEM((2, tm2 // SUBLANES, SUBLANES, d + LANES), F32),
                            pltpu.SemaphoreType.DMA((2,)),
                            pltpu.VMEM((epg, d, dff), BF16), pltpu.VMEM((epg, d, dff), BF16),
                            pltpu.VMEM((epg, dff, d), BF16)],
        ),
        out_shape=jax.ShapeDtypeStruct((n_tiles * tm2, d), F32),
        compiler_params=_cparams("arbitrary"),
        name=name + "_experts",
    )(src, tgrp, tval, hx, w1, w3, w2)

    tm = _row_tile(n, t_len, 512 if t_len % 512 == 0 else 256)
    gate_op, gate_spec = _mod_operand(gate, t_len, tm)
    final = final_g is not None
    in_specs = [
        pl.BlockSpec(memory_space=pl.ANY),
        pl.BlockSpec((tm, d), lambda i, dest: (i, 0)),
        gate_spec,
    ]
    args = [dest, ys, x, gate_op]
    if final:
        in_specs.append(pl.BlockSpec((1, d), lambda i, dest: (0, 0)))
        args.append(final_g.reshape(1, d))
    return pl.pallas_call(
        functools.partial(_combine_kernel, final=final),
        grid_spec=pltpu.PrefetchScalarGridSpec(
            num_scalar_prefetch=1,
            grid=(n // tm,),
            in_specs=in_specs,
            out_specs=pl.BlockSpec((tm, d), lambda i, dest: (i, 0)),
            scratch_shapes=[pltpu.VMEM((2, tm // SUBLANES, SUBLANES, d), F32),
                            pltpu.SemaphoreType.DMA((2,))],
        ),
        out_shape=jax.ShapeDtypeStruct((n, d), F32),
        compiler_params=_cparams("arbitrary"),
        name=name + "_combine",
    )(*args)


def _conv_taps(xm, halo, conv_w, conv_b, row_in_seq, axis):
    acc = conv_b + xm * conv_w[ML_CONV - 1]
    for s in range(1, ML_CONV):
        shifted = pltpu.roll(xm, s, axis)
        fill = pltpu.roll(halo, (s + SUBLANES - (ML_CONV - 1)) % SUBLANES, axis)
        if axis == 0:
            top = jnp.where(row_in_seq < s, fill, shifted[0:SUBLANES])
            shifted = jnp.concatenate([top, shifted[SUBLANES:]], axis=0)
        else:
            shifted = jnp.where(row_in_seq < s, fill, shifted)
        acc = acc + shifted * conv_w[ML_CONV - 1 - s]
    return acc


def _up_conv_qkv_kernel(x_ref, g_ref, sc_ref, sh_ref, wup_ref, c0_ref, cw_ref, cb_ref,
                        wqk_ref, wv_ref, wg_ref, bg_ref,
                        q_ref, k_ref, v_ref, xc_ref, z_ref, gates_ref, tail_ref, carry_scr,
                        *, short_seq, steps_per_seq, k_scale):
    rows = x_ref.shape[0]
    inner = z_ref.shape[1]
    hb = _rms_mod(x_ref[...], g_ref[...], _mod_rows(sc_ref, rows), _mod_rows(sh_ref, rows)).astype(BF16)
    if not short_seq:
        @pl.when(pl.program_id(0) % steps_per_seq == 0)
        def _():
            carry_scr[...] = c0_ref[0]

    gates = bg_ref[...]
    width = 2 * MXU_DIM
    for c0 in range(0, inner, width):
        cs = slice(c0, c0 + width)
        z_ref[:, cs] = _dot(hb, wup_ref[:, inner + c0:inner + c0 + width])
        xm = _dot(hb, wup_ref[:, cs])
        conv_w = [cw_ref[i:i + 1, cs] for i in range(ML_CONV)]
        conv_b = cb_ref[:, cs]
        if short_seq:
            xm3 = xm.reshape(rows // SUBLANES, SUBLANES, width)
            t_idx = lax.broadcasted_iota(jnp.int32, xm3.shape, 1)
            conv = _conv_taps(xm3, c0_ref[:, :, cs], conv_w, conv_b, t_idx, 1).reshape(rows, width)
            tail_ref[:, :, cs] = pltpu.roll(xm3, ML_CONV - 1, 1)
        else:
            row8 = lax.broadcasted_iota(jnp.int32, (SUBLANES, width), 0)
            conv = _conv_taps(xm, carry_scr[:, cs], conv_w, conv_b, row8, 0)
            tail = pltpu.roll(xm[rows - SUBLANES:], ML_CONV - 1, 0)
            carry_scr[:, cs] = tail
            tail_ref[0, :, cs] = tail
        xc = _silu(conv)
        xc_ref[:, cs] = xc
        xcb = xc.astype(BF16)
        xmb = xm.astype(BF16)
        for i in range(width // MXU_DIM):
            ls = slice(i * MXU_DIM, (i + 1) * MXU_DIM)
            ti = c0 // MXU_DIM + i
            os_ = slice(ti * MXU_DIM, (ti + 1) * MXU_DIM)
            qk = _dot(xcb[:, ls], wqk_ref[ti])
            qi = qk[:, :MXU_DIM]
            ki = qk[:, MXU_DIM:]
            vi = _dot(xmb[:, ls], wv_ref[ti])
            q_ref[:, os_] = qi.astype(q_ref.dtype)
            k_ref[:, os_] = (ki * k_scale).astype(k_ref.dtype)
            v_ref[:, os_] = vi.astype(v_ref.dtype)
            gates = gates + _dot(qi.astype(BF16), wg_ref[0, os_, :])
            gates = gates + _dot(ki.astype(BF16), wg_ref[1, os_, :])
            gates = gates + _dot(vi.astype(BF16), wg_ref[2, os_, :])
    lane = lax.broadcasted_iota(jnp.int32, gates.shape, 1)
    log_sig = jnp.minimum(gates, 0.0) - jnp.log1p(jnp.exp(-jnp.abs(gates)))
    gates_ref[...] = jnp.where(lane < ML_HEADS, gates, log_sig)


def _up_conv_qkv_call(x, g, sc, sh, wup, conv0, conv_w, conv_b, wqk, wv, wg, bg, b, t_len):
    n, d = x.shape
    inner = wup.shape[1] // 2
    short_seq = t_len == SUBLANES
    dh = inner // ML_HEADS
    k_scale = dh ** -0.5
    qkv_dtype = F32 if short_seq else BF16
    halo = jnp.zeros((b, SUBLANES, inner), F32)
    if conv0 is not None:
        halo = halo.at[:, :ML_CONV - 1].set(conv0)
    tm = min(256, n) if short_seq else math.gcd(512, t_len)
    seq_per_step = tm // t_len if short_seq else 1
    steps_per_seq = 1 if short_seq else t_len // tm
    sc_op, sc_spec = _mod_operand(sc, t_len, tm)
    sh_op, sh_spec = _mod_operand(sh, t_len, tm)
    seq_spec = pl.BlockSpec((seq_per_step, SUBLANES, inner), lambda i: (i // steps_per_seq, 0, 0))
    row_spec = lambda w: pl.BlockSpec((tm, w), lambda i: (i, 0))
    const2 = lambda s: pl.BlockSpec(s, lambda i: (0, 0), pipeline_mode=pl.Buffered(1))
    const3 = lambda s: pl.BlockSpec(s, lambda i: (0, 0, 0), pipeline_mode=pl.Buffered(1))
    q, k, v, xc, z, gates, tail = pl.pallas_call(
        functools.partial(_up_conv_qkv_kernel, short_seq=short_seq, steps_per_seq=steps_per_seq,
                          k_scale=k_scale),
        grid=(n // tm,),
        in_specs=[
            row_spec(d), const2((1, d)), sc_spec, sh_spec, const2(wup.shape),
            seq_spec, const2((ML_CONV, inner)), const2((1, inner)),
            const3(wqk.shape), const3(wv.shape), const3(wg.shape), const2((1, LANES)),
        ],
        out_specs=[row_spec(inner), row_spec(inner), row_spec(inner), row_spec(inner), row_spec(inner),
                   row_spec(LANES), seq_spec],
        out_shape=[
            jax.ShapeDtypeStruct((n, inner), qkv_dtype),
            jax.ShapeDtypeStruct((n, inner), qkv_dtype),
            jax.ShapeDtypeStruct((n, inner), qkv_dtype),
            jax.ShapeDtypeStruct((n, inner), F32),
            jax.ShapeDtypeStruct((n, inner), F32),
            jax.ShapeDtypeStruct((n, LANES), F32),
            jax.ShapeDtypeStruct((b, SUBLANES, inner), F32),
        ],
        scratch_shapes=[pltpu.VMEM((SUBLANES, inner), F32)],
        compiler_params=_cparams("arbitrary"),
        name="mlstm_up_conv_qkv",
    )(x, g.reshape(1, d), sc_op, sh_op, wup, halo, conv_w, conv_b.reshape(1, inner), wqk, wv, wg, bg)
    return q, k, v, xc, z, gates, tail[:, :ML_CONV - 1]


def _mlstm_kernel(*refs, has_state):
    if has_state:
        (q_ref, k_ref, v_ref, gates_ref, xc_ref, z_ref, nw_ref, sk_ref, c0_ref, n0_ref, m0_ref,
         o_ref, c_ref, n_ref, m_ref) = refs
    else:
        (q_ref, k_ref, v_ref, gates_ref, xc_ref, z_ref, nw_ref, sk_ref,
         o_ref, c_ref, n_ref, m_ref) = refs
    ch = pl.program_id(1)
    length = q_ref.shape[0]
    dh = q_ref.shape[1] // ML_HEADS

    @pl.when(ch == 0)
    def _():
        if has_state:
            c_ref[...] = c0_ref[...]
            n_ref[...] = n0_ref[...]
            m_ref[...] = m0_ref[...]
        else:
            c_ref[...] = jnp.zeros_like(c_ref)
            n_ref[...] = jnp.zeros_like(n_ref)
            m_ref[...] = jnp.zeros_like(m_ref)

    gates = gates_ref[...]
    gates_t = gates.T
    t_idx = lax.broadcasted_iota(jnp.int32, (length, length), 0)
    s_idx = lax.broadcasted_iota(jnp.int32, (length, length), 1)
    causal = s_idx <= t_idx
    m_all = m_ref[0]
    m_new = m_all
    lane = lax.broadcasted_iota(jnp.int32, m_all.shape, 1)
    for h in range(ML_HEADS):
        hs = slice(h * dh, (h + 1) * dh)
        ig_col = gates[:, h:h + 1]
        lf_col = gates[:, ML_HEADS + h:ML_HEADS + h + 1]
        ig_row = gates_t[h:h + 1, :]
        lf_row = gates_t[ML_HEADS + h:ML_HEADS + h + 1, :]
        b_col = jnp.sum(jnp.where(causal, lf_row, 0.0), axis=1, keepdims=True)
        b_row = jnp.sum(jnp.where(t_idx <= s_idx, lf_col, 0.0), axis=0, keepdims=True)
        m_prev = m_all[:, h:h + 1]
        dm = jnp.where(causal, b_col - b_row + ig_row, -jnp.inf)
        a = b_col + m_prev
        m_t = jnp.maximum(a, jnp.max(dm, axis=1, keepdims=True))
        w_c = jnp.exp(a - m_t)
        w = jnp.exp(dm - m_t)
        qh = q_ref[:, hs].astype(BF16)
        kh = k_ref[:, hs]
        vh = v_ref[:, hs].astype(BF16)
        c_h = c_ref[0, h]
        n_h = n_ref[0, :, hs]
        s = _dot_nt(qh, kh.astype(BF16)) * w
        num = w_c * _dot(qh, c_h.astype(BF16)) + _dot(s.astype(BF16), vh)
        den = w_c * jnp.sum(qh.astype(F32) * n_h, axis=-1, keepdims=True) + jnp.sum(s, axis=-1, keepdims=True)
        hc = num / jnp.maximum(jnp.abs(den), jnp.exp(-m_t))
        m_last = m_t[length - 1:length]
        b_last = b_col[length - 1:length]
        wl_c = jnp.exp(a[length - 1:length] - m_last)
        wl_col = jnp.exp(b_last - b_col + ig_col - m_last)
        kw = kh.astype(F32) * wl_col
        c_ref[0, h] = wl_c * c_h + _dot_tn(kw.astype(BF16), vh)
        n_ref[0, :, hs] = wl_c * n_h + jnp.sum(kw, axis=0, keepdims=True)
        m_new = jnp.where(lane == h, m_last, m_new)
        mu = jnp.mean(hc, axis=-1, keepdims=True)
        dev = hc - mu
        var = jnp.mean(dev * dev, axis=-1, keepdims=True)
        hn = dev * lax.rsqrt(var + EPS) * nw_ref[:, hs]
        ho = (hn + sk_ref[:, hs] * xc_ref[:, hs]) * _silu(z_ref[:, hs])
        o_ref[:, hs] = ho.astype(o_ref.dtype)
    m_ref[0] = m_new


def _mlstm_call(q, k, v, gates, xc, z, norm_w, skip, c0, n0, m0, b, t_len):
    n, inner = q.shape
    dh = inner // ML_HEADS
    has_state = c0 is not None
    length = math.gcd(t_len, ML_CHUNK)
    nc = t_len // length
    row = lambda w: pl.BlockSpec((length, w), lambda i, c: (i * nc + c, 0))
    const = lambda w: pl.BlockSpec((1, w), lambda i, c: (0, 0))
    c_spec = pl.BlockSpec((1, ML_HEADS, dh, dh), lambda i, c: (i, 0, 0, 0))
    n_spec = pl.BlockSpec((1, 1, inner), lambda i, c: (i, 0, 0))
    m_spec = pl.BlockSpec((1, 1, LANES), lambda i, c: (i, 0, 0))
    in_specs = [row(inner), row(inner), row(inner), row(LANES), row(inner), row(inner),
                const(inner), const(inner)]
    args = [q, k, v, gates, xc, z, norm_w.reshape(1, inner), skip.reshape(1, inner)]
    if has_state:
        in_specs += [c_spec, n_spec, m_spec]
        m0_pad = jnp.zeros((b, 1, LANES), F32).at[:, 0, :ML_HEADS].set(m0)
        args += [c0, n0.reshape(b, 1, inner), m0_pad]
    ho, c_t, n_t, m_t = pl.pallas_call(
        functools.partial(_mlstm_kernel, has_state=has_state),
        grid=(b, nc),
        in_specs=in_specs,
        out_specs=[row(inner), c_spec, n_spec, m_spec],
        out_shape=[
            jax.ShapeDtypeStruct((n, inner), BF16 if length % 16 == 0 else F32),
            jax.ShapeDtypeStruct((b, ML_HEADS, dh, dh), F32),
            jax.ShapeDtypeStruct((b, 1, inner), F32),
            jax.ShapeDtypeStruct((b, 1, LANES), F32),
        ],
        compiler_params=_cparams("parallel", "arbitrary"),
        name="mlstm_scan",
    )(*args)
    return ho, c_t, n_t.reshape(b, ML_HEADS, dh), m_t[:, 0, :ML_HEADS]


def _block_diag_tiles(w):
    n_blk, blk, _ = w.shape
    per = MXU_DIM // blk
    rows = w.reshape(n_blk // per, per, blk, blk).transpose(0, 1, 3, 2).reshape(n_blk // per, MXU_DIM, blk)
    r_blk = np.arange(MXU_DIM)[:, None] // blk
    c_blk = np.arange(MXU_DIM)[None, :] // blk
    same_block = jnp.asarray((r_blk == c_blk).astype(np.float32))
    return jnp.tile(rows, (1, 1, per)) * same_block


def _prep_weights(p):
    w = {}
    w['hg_win'] = p['hg_win'].astype(BF16)
    w['hg_wo'] = p['hg_wo'].astype(BF16)
    w['ml_wup'] = p['ml_wup'].astype(BF16)
    w['ml_wdown'] = p['ml_wdown'].astype(BF16)
    dep, n_exp, d_model, dff = p['moe_w1'].shape
    n_stack = dep * n_exp // MOE_EPG
    w['moe_w1'] = p['moe_w1'].reshape(n_stack, MOE_EPG, d_model, dff)
    w['moe_w3'] = p['moe_w3'].reshape(n_stack, MOE_EPG, d_model, dff)
    w['moe_w2'] = p['moe_w2'].reshape(n_stack, MOE_EPG, dff, d_model)
    depth, d, g = p['moe_wrg'].shape
    n_exp = g * p['moe_wre'].shape[-1]
    wr = jnp.zeros((depth, d, LANES), F32)
    wr = wr.at[:, :, :g].set(p['moe_wrg']).at[:, :, g:g + n_exp].set(p['moe_wre'].reshape(depth, d, n_exp))
    w['moe_wr'] = wr.astype(BF16)
    br = jnp.zeros((depth, 1, LANES), F32)
    br = br.at[:, 0, :g].set(p['moe_brg']).at[:, 0, g:g + n_exp].set(p['moe_bre'].reshape(depth, n_exp))
    w['moe_br'] = br
    n_b = p['ml_wq'].shape[0]
    wq = jnp.stack([_block_diag_tiles(p['ml_wq'][j]) for j in range(n_b)])
    wk = jnp.stack([_block_diag_tiles(p['ml_wk'][j]) for j in range(n_b)])
    wv = jnp.stack([_block_diag_tiles(p['ml_wv'][j]) for j in range(n_b)])
    w['ml_wqk'] = jnp.concatenate([wq, wk], axis=-1).astype(BF16)
    w['ml_wv'] = wv.astype(BF16)
    inner = p['ml_conv_b'].shape[-1]
    wg = jnp.zeros((n_b, 3 * inner, LANES), F32)
    wg = wg.at[:, :, :ML_HEADS].set(p['ml_wig']).at[:, :, ML_HEADS:2 * ML_HEADS].set(p['ml_wfg'])
    w['ml_wg'] = wg.reshape(n_b, 3, inner, LANES).astype(BF16)
    bg = jnp.zeros((n_b, 1, LANES), F32)
    bg = bg.at[:, 0, :ML_HEADS].set(p['ml_big']).at[:, 0, ML_HEADS:2 * ML_HEADS].set(p['ml_bfg'])
    w['ml_bg'] = bg
    return w


def _lb_kernel(lb_ref, o_ref):
    x = lb_ref[...]
    mx = jnp.max(x, axis=0, keepdims=True)
    ex = jnp.exp(x - mx)
    sm = ex / jnp.sum(ex, axis=0, keepdims=True)
    rows = []
    run = jnp.zeros_like(sm[0:1])
    for i in range(x.shape[0]):
        run = run + sm[i:i + 1]
        rows.append(run)
    o_ref[...] = jnp.concatenate(rows, axis=0)


def _lb_call(hg_lb):
    return pl.pallas_call(
        _lb_kernel,
        out_shape=jax.ShapeDtypeStruct(hg_lb.shape, F32),
        name="hgrn_lower_bound",
    )(hg_lb)


def _trunk(x3, mods, s_hg, s_c, s_n, s_m, s_conv, p, w, lb_all):
    b, t_len, d = x3.shape
    n = b * t_len
    x = x3.reshape(n, d)
    depth = p['norm_g'].shape[0]
    new_hg, new_c, new_n, new_m, new_conv = [], [], [], [], []
    for l in range(depth):
        sh1, sc1, g1, sh2, sc2, g2 = [mods[l][:, i * d:(i + 1) * d] for i in range(6)]
        if l % 2 == 0:
            a = l // 2
            proj = _norm_mm_call(x, p['norm_g'][l, 0], sc1, sh1, w['hg_win'][a], t_len, "hgrn_in_proj")
            o, s_t = _gla_call(proj, lb_all[l], p['hg_norm'][a], None if s_hg is None else s_hg[a], b, t_len)
            new_hg.append(s_t)
            x = _mm_res_call(o, w['hg_wo'][a], x, g1, t_len, "hgrn_out_proj")
        else:
            j = l // 2
            q, k, v, xc, z, gates, conv_tail = _up_conv_qkv_call(
                x, p['norm_g'][l, 0], sc1, sh1, w['ml_wup'][j],
                None if s_conv is None else s_conv[j], p['ml_conv_w'][j], p['ml_conv_b'][j],
                w['ml_wqk'][j], w['ml_wv'][j], w['ml_wg'][j], w['ml_bg'][j], b, t_len)
            ho, c_t, n_t, m_t = _mlstm_call(
                q, k, v, gates, xc, z, p['ml_norm'][j], p['ml_skip'][j],
                None if s_c is None else s_c[j], None if s_n is None else s_n[j],
                None if s_m is None else s_m[j], b, t_len)
            new_c.append(c_t)
            new_n.append(n_t)
            new_m.append(m_t)
            new_conv.append(conv_tail)
            x = _mm_res_call(ho.astype(BF16), w['ml_wdown'][j], x, g1, t_len, "mlstm_down_proj")
        x = _moe_call(x, p['norm_g'][l, 1], sc2, sh2, g2, w['moe_wr'][l], w['moe_br'][l],
                      w['moe_w1'], w['moe_w3'], w['moe_w2'], l,
                      p['final_g'] if l == depth - 1 else None, t_len, "moe_layer%d" % l)
    return (x.reshape(b, t_len, d), jnp.stack(new_hg), jnp.stack(new_c), jnp.stack(new_n),
            jnp.stack(new_m), jnp.stack(new_conv))


def kernel(x_prompt, x_sample, c_prompt, c_sample, state_hgrn, state_mlstm_c, state_mlstm_n, state_mlstm_m, state_conv, w_ada, b_ada, norm_g, final_g, hg_win, hg_wo, hg_norm, hg_lb, ml_wup, ml_conv_w, ml_conv_b, ml_wq, ml_wk, ml_wv, ml_wig, ml_big, ml_wfg, ml_bfg, ml_norm, ml_skip, ml_wdown, moe_wrg, moe_brg, moe_wre, moe_bre, moe_w1, moe_w3, moe_w2):
    p = dict(w_ada=w_ada, b_ada=b_ada, norm_g=norm_g, final_g=final_g,
             hg_win=hg_win, hg_wo=hg_wo, hg_norm=hg_norm, hg_lb=hg_lb,
             ml_wup=ml_wup, ml_conv_w=ml_conv_w, ml_conv_b=ml_conv_b, ml_wq=ml_wq, ml_wk=ml_wk, ml_wv=ml_wv,
             ml_wig=ml_wig, ml_big=ml_big, ml_wfg=ml_wfg, ml_bfg=ml_bfg, ml_norm=ml_norm, ml_skip=ml_skip,
             ml_wdown=ml_wdown, moe_wrg=moe_wrg, moe_brg=moe_brg, moe_wre=moe_wre, moe_bre=moe_bre,
             moe_w1=moe_w1, moe_w3=moe_w3, moe_w2=moe_w2)
    w = _prep_weights(p)
    lb_all = _lb_call(hg_lb)
    bp = x_prompt.shape[0]
    c_all = jnp.concatenate([c_prompt, c_sample], axis=0)
    mod_all = _ada_call(c_all, w_ada, b_ada)
    mods_p = [mod_all[l, :bp] for l in range(mod_all.shape[0])]
    mods_s = [mod_all[l, bp:] for l in range(mod_all.shape[0])]
    y_p, hg_p, mc_p, mn_p, mm_p, conv_p = _trunk(x_prompt, mods_p, None, None, None, None, None, p, w, lb_all)
    y_s, hg_s, mc_s, mn_s, mm_s, conv_s = _trunk(x_sample, mods_s, state_hgrn, state_mlstm_c, state_mlstm_n,
                                                 state_mlstm_m, state_conv, p, w, lb_all)
    return (y_p, y_s, hg_p, mc_p, mn_p, mm_p, conv_p, hg_s, mc_s, mn_s, mm_s, conv_s)
```

```python
import functools
import math

import numpy as np
import jax
import jax.numpy as jnp
from jax import lax
from jax.experimental import pallas as pl
from jax.experimental.pallas import tpu as pltpu

F32 = jnp.float32
BF16 = jnp.bfloat16
EPS = 1e-6

HG_DK = 128
ML_HEADS = 4
ML_CONV = 4
ML_QKV_BLOCK = 4
MOE_GROUPS = 4
MOE_EPG = 4
CHUNK = 64
GLA_CHUNKS_PER_STEP = 8
ML_CHUNK = 256

LANES = 128
SUBLANES = 8
MXU_DIM = 256
VMEM_LIMIT_BYTES = 56 * 1024 * 1024


def _cparams(*sem):
    return pltpu.CompilerParams(dimension_semantics=sem, vmem_limit_bytes=VMEM_LIMIT_BYTES)


def _silu(x):
    return x * jax.nn.sigmoid(x)


def _dot(a, b):
    return jnp.dot(a, b, preferred_element_type=F32)


def _dot_nt(a, b):
    return lax.dot_general(a, b, (((1,), (1,)), ((), ())), preferred_element_type=F32)


def _dot_tn(a, b):
    return lax.dot_general(a, b, (((0,), (0,)), ((), ())), preferred_element_type=F32)


def _rms_mod(x, g, sc, sh):
    ms = jnp.mean(x * x, axis=-1, keepdims=True)
    h = x * lax.rsqrt(ms + EPS) * g
    return h * (1.0 + sc) + sh


def _ada_kernel(c_ref, w_ref, b_ref, o_ref):
    cm = _silu(c_ref[...]).astype(BF16)
    o_ref[0] = _dot(cm, w_ref[0].astype(BF16)) + b_ref[0]


def _ada_call(c_all, w_ada, b_ada):
    depth, d, n_out = w_ada.shape
    m = c_all.shape[0]
    tn = 512
    return pl.pallas_call(
        _ada_kernel,
        grid=(depth, n_out // tn),
        in_specs=[
            pl.BlockSpec((m, d), lambda l, j: (0, 0)),
            pl.BlockSpec((1, d, tn), lambda l, j: (l, 0, j)),
            pl.BlockSpec((1, 1, tn), lambda l, j: (l, 0, j)),
        ],
        out_specs=pl.BlockSpec((1, m, tn), lambda l, j: (l, 0, j)),
        out_shape=jax.ShapeDtypeStruct((depth, m, n_out), F32),
        compiler_params=_cparams("parallel", "parallel"),
        name="ada_mod",
    )(c_all, w_ada, b_ada.reshape(depth, 1, n_out))


def _mod_operand(m, t_len, tm):
    b, d = m.shape
    if t_len % tm == 0:
        per_b = t_len // tm
        return m.reshape(b, 1, d), pl.BlockSpec((1, 1, d), lambda i, *_: (i // per_b, 0, 0))
    assert tm % t_len == 0
    nb = tm // t_len
    return m.reshape(b // nb, nb, 1, d), pl.BlockSpec((1, nb, 1, d), lambda i, *_: (i, 0, 0, 0))


def _mod_rows(ref, rows):
    v = ref[0]
    if v.ndim == 2:
        return v
    nb, _, d = v.shape
    return jnp.broadcast_to(v, (nb, rows // nb, d)).reshape(rows, d)


def _row_tile(n, t_len, target):
    tm = min(target, n)
    while n % tm or (t_len % tm and tm % t_len):
        tm //= 2
    return tm


def _norm_mm_kernel(x_ref, g_ref, sc_ref, sh_ref, w_ref, o_ref, *, col_chunk):
    rows = x_ref.shape[0]
    hb = _rms_mod(x_ref[...], g_ref[...], _mod_rows(sc_ref, rows), _mod_rows(sh_ref, rows)).astype(BF16)
    for c0 in range(0, o_ref.shape[1], col_chunk):
        o_ref[:, c0:c0 + col_chunk] = _dot(hb, w_ref[:, c0:c0 + col_chunk])


def _norm_mm_call(x, g, sc, sh, w, t_len, name):
    n, d = x.shape
    n_out = w.shape[1]
    tm = _row_tile(n, t_len, 512)
    sc_op, sc_spec = _mod_operand(sc, t_len, tm)
    sh_op, sh_spec = _mod_operand(sh, t_len, tm)
    return pl.pallas_call(
        functools.partial(_norm_mm_kernel, col_chunk=512),
        grid=(n // tm,),
        in_specs=[
            pl.BlockSpec((tm, d), lambda i: (i, 0)),
            pl.BlockSpec((1, d), lambda i: (0, 0)),
            sc_spec, sh_spec,
            pl.BlockSpec((d, n_out), lambda i: (0, 0)),
        ],
        out_specs=pl.BlockSpec((tm, n_out), lambda i: (i, 0)),
        out_shape=jax.ShapeDtypeStruct((n, n_out), F32),
        compiler_params=_cparams("parallel"),
        name=name,
    )(x, g.reshape(1, d), sc_op, sh_op, w)


def _mm_res_kernel(a_ref, w_ref, x_ref, gate_ref, o_ref):
    o_ref[...] = x_ref[...] + _mod_rows(gate_ref, x_ref.shape[0]) * _dot(a_ref[...], w_ref[...])


def _mm_res_call(a, w, x, gate, t_len, name):
    n, k = a.shape
    d = w.shape[1]
    tm = _row_tile(n, t_len, 512)
    gate_op, gate_spec = _mod_operand(gate, t_len, tm)
    return pl.pallas_call(
        _mm_res_kernel,
        grid=(n // tm,),
        in_specs=[
            pl.BlockSpec((tm, k), lambda i: (i, 0)),
            pl.BlockSpec((k, d), lambda i: (0, 0)),
            pl.BlockSpec((tm, d), lambda i: (i, 0)),
            gate_spec,
        ],
        out_specs=pl.BlockSpec((tm, d), lambda i: (i, 0)),
        out_shape=jax.ShapeDtypeStruct((n, d), F32),
        compiler_params=_cparams("parallel"),
        name=name,
    )(a, w, x, gate_op)


def _gla_tables(t_sub, nseq):
    r = t_sub * nseq
    levels = []
    m = t_sub // 2
    while m >= 1:
        levels.append(m)
        m //= 2
    n_lev = len(levels)
    tril = np.zeros((r, r), np.float32)
    mask = np.zeros((n_lev + 1, r, r), np.float32)
    for li, m in enumerate(levels):
        for row in range(r):
            blk = (row // (2 * m)) * 2 * m
            if row - blk >= m:
                mask[li, row, blk:blk + m] = 1.0
    for row in range(r):
        s0 = (row // t_sub) * t_sub
        tril[row, s0:row + 1] = 1.0
        mask[n_lev, row, row] = 1.0
    return tril, mask, levels


def _bcast_block_row(b, block, row_in_block):
    parts = [jnp.broadcast_to(b[s + row_in_block:s + row_in_block + 1, :], (block, b.shape[1]))
             for s in range(0, b.shape[0], block)]
    return parts[0] if len(parts) == 1 else jnp.concatenate(parts, axis=0)


def _level_decay(b, m):
    r = b.shape[0]
    pos = lax.broadcasted_iota(jnp.int32, b.shape, 0) & (2 * m - 1)
    if 2 * m >= SUBLANES:
        b_mid = _bcast_block_row(b, 2 * m, m - 1)
    else:
        b_mid = b
        for p in range(2 * m):
            if p != m - 1:
                b_mid = jnp.where(pos == p, pltpu.roll(b, (p - (m - 1)) % r, 0), b_mid)
    return jnp.exp2(jnp.where(pos >= m, b - b_mid, b_mid - b))


def _gla_kernel(*refs, t_sub, nseq, n_ch, levels, n_heads, has_state, steps_per_seq):
    if has_state:
        proj_ref, lb_ref, gn_ref, tril_ref, mask_ref, s0_ref, o_ref, sout_ref, st_ref = refs
    else:
        proj_ref, lb_ref, gn_ref, tril_ref, mask_ref, o_ref, sout_ref, st_ref = refs
    r = t_sub * nseq
    n_lev = len(levels)
    dk = HG_DK
    hk = n_heads * dk
    c = lax.rem(pl.program_id(0), steps_per_seq)

    @pl.when(c == 0)
    def _():
        if has_state:
            for j in range(nseq):
                for h in range(n_heads):
                    st_ref[j, h] = s0_ref[j, h].T
        else:
            st_ref[...] = jnp.zeros_like(st_ref)

    gn = gn_ref[...]
    tril = tril_ref[...]

    def finish_head(rows, h, a, qh, kh, e_cum_h, e_end_h, d_last):
        hs = slice(h * dk, (h + 1) * dk)
        vb = proj_ref[rows, 2 * hk + h * dk:2 * hk + (h + 1) * dk].astype(BF16)
        o_intra = _dot(a.astype(BF16), vb)
        qd = qh * e_cum_h
        kd = kh * e_end_h
        o_parts = []
        for j in range(nseq):
            rs = slice(j * t_sub, (j + 1) * t_sub)
            st = st_ref[j, h]
            o_parts.append(_dot_nt(qd[rs], st.astype(BF16)))
            st_ref[j, h] = st * d_last[j] + _dot_tn(vb[rs], kd[rs])
        o_inter = o_parts[0] if nseq == 1 else jnp.concatenate(o_parts, axis=0)
        o = o_intra + o_inter
        o = o * lax.rsqrt(jnp.mean(o * o, axis=-1, keepdims=True) + EPS) * gn
        zg = proj_ref[rows, 3 * hk + h * dk:3 * hk + (h + 1) * dk]
        o_ref[rows, hs] = (o * _silu(zg)).astype(BF16)

    pending = None
    head_group = n_heads if nseq == 1 else 1
    for ci in range(n_ch):
        rows = slice(ci * r, (ci + 1) * r)
        for g0 in range(0, n_heads, head_group):
            gw = head_group * dk
            gs = slice(g0 * dk, g0 * dk + gw)
            lb = lb_ref[:, gs]
            zq = proj_ref[rows, g0 * dk:g0 * dk + gw]
            zf = proj_ref[rows, hk + g0 * dk:hk + g0 * dk + gw]
            f = lb + (1.0 - lb) * jax.nn.sigmoid(zf)
            lf = jnp.log2(f)
            q = _silu(zq).astype(BF16)
            k = (1.0 - f).astype(BF16)

            p0 = lf.astype(BF16)
            r1 = lf - p0.astype(F32)
            p1 = r1.astype(BF16)
            p2 = (r1 - p1.astype(F32)).astype(BF16)
            b = _dot(tril, p0) + _dot(tril, p1) + _dot(tril, p2)
            e_cum = jnp.exp2(b)
            e_cum_b = e_cum.astype(BF16)
            e_end_b = jnp.exp2(_bcast_block_row(b, t_sub, t_sub - 1) - b).astype(BF16)
            zs = [_level_decay(b, m).astype(BF16) for m in levels]

            for hh in range(head_group):
                h = g0 + hh
                ls = slice(hh * dk, (hh + 1) * dk)
                qh = q[:, ls]
                kh = k[:, ls]
                a = _dot_nt(qh, kh) * mask_ref[n_lev]
                for li in range(n_lev):
                    z = zs[li][:, ls]
                    a = a + _dot_nt(qh * z, kh * z) * mask_ref[li]
                item = (rows, h, a, qh, kh, e_cum_b[:, ls], e_end_b[:, ls],
                        [e_cum[(j + 1) * t_sub - 1:(j + 1) * t_sub, ls] for j in range(nseq)])
                if nseq > 1:
                    finish_head(*item)
                else:
                    if pending is not None:
                        finish_head(*pending)
                    pending = item
    if pending is not None:
        finish_head(*pending)

    @pl.when(c == steps_per_seq - 1)
    def _():
        for j in range(nseq):
            for h in range(n_heads):
                sout_ref[j, h] = st_ref[j, h].T


def _gla_call(proj, lb, gn, s0, b, t_len):
    n = proj.shape[0]
    hk = lb.shape[-1]
    n_heads = hk // HG_DK
    has_state = s0 is not None
    if t_len % CHUNK == 0:
        t_sub, nseq = CHUNK, 1
    else:
        t_sub, nseq = t_len, CHUNK // t_len
        assert t_sub * nseq == CHUNK and b % nseq == 0
    r = t_sub * nseq
    tril, mask, levels = _gla_tables(t_sub, nseq)
    n_ch = math.gcd(GLA_CHUNKS_PER_STEP, t_len // t_sub)
    rows = n_ch * r
    steps_per_seq = t_len // (t_sub * n_ch)
    n_steps = n // rows
    const2 = lambda s: pl.BlockSpec(s, lambda i: (0, 0))
    in_specs = [pl.BlockSpec((rows, proj.shape[1]), lambda i: (i, 0)),
                const2((1, hk)), const2((1, HG_DK)), const2(tril.shape),
                pl.BlockSpec(mask.shape, lambda i: (0, 0, 0))]
    args = [proj, lb.reshape(1, hk), gn.reshape(1, HG_DK), jnp.asarray(tril, BF16), jnp.asarray(mask, F32)]
    s_spec = pl.BlockSpec((nseq, n_heads, HG_DK, HG_DK), lambda i: (i // steps_per_seq, 0, 0, 0))
    if has_state:
        in_specs.append(s_spec)
        args.append(s0)
    o, s_out = pl.pallas_call(
        functools.partial(_gla_kernel, t_sub=t_sub, nseq=nseq, n_ch=n_ch, levels=tuple(levels),
                          n_heads=n_heads, has_state=has_state, steps_per_seq=steps_per_seq),
        grid=(n_steps,),
        in_specs=in_specs,
        out_specs=[
            pl.BlockSpec((rows, hk), lambda i: (i, 0)),
            s_spec,
        ],
        out_shape=[
            jax.ShapeDtypeStruct((n, hk), BF16),
            jax.ShapeDtypeStruct((b, n_heads, HG_DK, HG_DK), F32),
        ],
        scratch_shapes=[pltpu.VMEM((nseq, n_heads, HG_DK, HG_DK), F32)],
        compiler_params=_cparams("arbitrary"),
        name="gla_scan",
    )(*args)
    return o, s_out


def _route_kernel(x_ref, g_ref, sc_ref, sh_ref, wr_ref, br_ref, tril_ref, hx_ref, meta_ref, cnt_ref,
                  *, n_groups, epg):
    d = x_ref.shape[1]
    neg = -jnp.inf
    far = float(LANES)

    @pl.when(pl.program_id(0) == 0)
    def _():
        cnt_ref[...] = jnp.zeros_like(cnt_ref)

    h = _rms_mod(x_ref[...], g_ref[...], _mod_rows(sc_ref, x_ref.shape[0]), _mod_rows(sh_ref, x_ref.shape[0]))
    logit = _dot(h.astype(BF16), wr_ref[...]) + br_ref[...]
    lane = lax.broadcasted_iota(jnp.int32, logit.shape, 1).astype(F32)
    gm = lane < n_groups
    gmax = jnp.max(jnp.where(gm, logit, neg), axis=-1, keepdims=True)
    gstar = jnp.min(jnp.where(gm, jnp.where(logit == gmax, lane, far), far), axis=-1, keepdims=True)
    psum = jnp.sum(jnp.where(gm, jnp.exp(logit - gmax), 0.0), axis=-1, keepdims=True)
    pstar = 1.0 / psum
    lo = n_groups + gstar * epg
    em = jnp.where(lane >= lo, jnp.where(lane < lo + epg, 1.0, 0.0), 0.0)
    l1 = jnp.where(em > 0.0, logit, neg)
    v1 = jnp.max(l1, axis=-1, keepdims=True)
    i1 = jnp.min(jnp.where(l1 == v1, lane, far), axis=-1, keepdims=True)
    l2 = jnp.where(lane == i1, neg, l1)
    v2 = jnp.max(l2, axis=-1, keepdims=True)
    i2 = jnp.min(jnp.where(l2 == v2, lane, far), axis=-1, keepdims=True)
    e2 = jnp.exp(v2 - v1)
    wt1 = pstar / (1.0 + e2)
    wt2 = pstar * e2 / (1.0 + e2)
    onehot = jnp.where(lane == gstar, 1.0, 0.0)
    within = _dot(tril_ref[...], onehot.astype(BF16))
    carry = cnt_ref[...]
    rank = jnp.sum(onehot * (within + carry), axis=-1, keepdims=True)
    cnt_ref[...] = carry + jnp.sum(onehot, axis=0, keepdims=True)
    meta = (jnp.where(lane == i1 - lo, wt1, 0.0) + jnp.where(lane == i2 - lo, wt2, 0.0)
            + jnp.where(lane == epg, gstar, 0.0) + jnp.where(lane == epg + 1, rank, 0.0))
    hx_ref[:, :d] = h
    hx_ref[:, d:] = meta
    meta_ref[...] = meta.T[0:SUBLANES]


def _route_call(x, g, sc, sh, wr, br, t_len, name):
    n, d = x.shape
    tm = _row_tile(n, t_len, 1024 if t_len % 1024 == 0 else 256)
    sc_op, sc_spec = _mod_operand(sc, t_len, tm)
    sh_op, sh_spec = _mod_operand(sh, t_len, tm)
    tril = jnp.asarray(np.tril(np.ones((tm, tm), np.float32), -1), BF16)
    return pl.pallas_call(
        functools.partial(_route_kernel, n_groups=MOE_GROUPS, epg=MOE_EPG),
        grid=(n // tm,),
        in_specs=[
            pl.BlockSpec((tm, d), lambda i: (i, 0)),
            pl.BlockSpec((1, d), lambda i: (0, 0)),
            sc_spec, sh_spec,
            pl.BlockSpec((d, LANES), lambda i: (0, 0)),
            pl.BlockSpec((1, LANES), lambda i: (0, 0)),
            pl.BlockSpec((tm, tm), lambda i: (0, 0)),
        ],
        out_specs=[
            pl.BlockSpec((tm, d + LANES), lambda i: (i, 0)),
            pl.BlockSpec((SUBLANES, tm), lambda i: (0, i)),
            pl.BlockSpec((1, LANES), lambda i: (0, 0)),
        ],
        out_shape=[
            jax.ShapeDtypeStruct((n, d + LANES), F32),
            jax.ShapeDtypeStruct((SUBLANES, n), F32),
            jax.ShapeDtypeStruct((1, LANES), F32),
        ],
        compiler_params=_cparams("arbitrary"),
        name=name + "_route",
    )(x, g.reshape(1, d), sc_op, sh_op, wr, br, tril)


def _row_gather(idx_ref, src_hbm, buf, sem, tile, slot, start, unrolled=False):
    groups = buf.shape[1]
    base = tile * (groups * SUBLANES)

    def run(s):
        def body(i, carry):
            for u in range(SUBLANES):
                idx = idx_ref[base + i * SUBLANES + u]
                cp = pltpu.make_async_copy(src_hbm.at[pl.ds(idx, 1)],
                                           buf.at[s, i, pl.ds(u, 1)], sem.at[s])
                if start:
                    cp.start(priority=u % 2)
                else:
                    cp.wait()
            return carry

        if unrolled:
            for i in range(groups):
                body(i, 0)
        else:
            lax.fori_loop(0, groups, body, 0)

    for s in range(2):
        @pl.when(slot == s)
        def _():
            run(s)


def _expert_kernel(src_ref, tgrp_ref, tval_ref, hx_hbm, w1_ref, w3_ref, w2_ref, y_ref, hbuf, sem,
                   wb1, wb3, wb2, *, epg, d):
    j = pl.program_id(0)
    rows = y_ref.shape[0]

    @pl.when(jnp.logical_or(j == 0, tgrp_ref[j] != tgrp_ref[jnp.maximum(j - 1, 0)]))
    def _():
        for e in range(epg):
            wb1[e] = w1_ref[0, e].astype(BF16)
            wb3[e] = w3_ref[0, e].astype(BF16)
            wb2[e] = w2_ref[0, e].astype(BF16)

    @pl.when(j == 0)
    def _():
        @pl.when(tval_ref[0] == 1)
        def _():
            _row_gather(src_ref, hx_hbm, hbuf, sem, 0, 0, True)

    @pl.when(j + 1 < pl.num_programs(0))
    def _():
        @pl.when(tval_ref[j + 1] == 1)
        def _():
            _row_gather(src_ref, hx_hbm, hbuf, sem, j + 1, (j + 1) % 2, True, unrolled=True)

    @pl.when(tval_ref[j] == 1)
    def _():
        slot = j % 2
        _row_gather(src_ref, hx_hbm, hbuf, sem, j, slot, False)
        tile = hbuf[slot].reshape(rows, hbuf.shape[-1])
        hb = tile[:, :d].astype(BF16)
        hids = []
        for e in range(epg):
            a = _dot(hb, wb1[e])
            b = _dot(hb, wb3[e])
            hids.append((_silu(a) * b * tile[:, d + e:d + e + 1]).astype(BF16))
        hid = jnp.concatenate(hids, axis=1)
        y_ref[...] = _dot(hid, wb2[...].reshape(hid.shape[1], d))

    @pl.when(tval_ref[j] == 0)
    def _():
        y_ref[...] = jnp.zeros_like(y_ref)


def _combine_kernel(*refs, final):
    if final:
        dest_ref, ys_hbm, x_ref, gate_ref, fg_ref, o_ref, gbuf, sem = refs
    else:
        dest_ref, ys_hbm, x_ref, gate_ref, o_ref, gbuf, sem = refs
    i = pl.program_id(0)
    rows = x_ref.shape[0]

    @pl.when(i == 0)
    def _():
        _row_gather(dest_ref, ys_hbm, gbuf, sem, 0, 0, True)

    @pl.when(i + 1 < pl.num_programs(0))
    def _():
        _row_gather(dest_ref, ys_hbm, gbuf, sem, i + 1, (i + 1) % 2, True, unrolled=True)

    slot = i % 2
    _row_gather(dest_ref, ys_hbm, gbuf, sem, i, slot, False)
    y = x_ref[...] + _mod_rows(gate_ref, rows) * gbuf[slot].reshape(rows, gbuf.shape[-1])
    if final:
        y = y * lax.rsqrt(jnp.mean(y * y, axis=-1, keepdims=True) + EPS) * fg_ref[...]
    o_ref[...] = y


def _invperm_kernel(dest_ref, src_ref):
    def clear(i, carry):
        src_ref[i] = 0
        return carry

    def place(i, carry):
        src_ref[dest_ref[i]] = i
        return carry

    lax.fori_loop(0, src_ref.shape[0], clear, 0, unroll=8)
    lax.fori_loop(0, dest_ref.shape[0], place, 0, unroll=8)


def _moe_call(x, g, sc, sh, gate, wr, br, w1, w3, w2, layer, final_g, t_len, name):
    n, d = x.shape
    epg, dff = w1.shape[1], w1.shape[3]
    n_grp = MOE_GROUPS
    hx, meta, cnt = _route_call(x, g, sc, sh, wr, br, t_len, name)

    tm2 = 512 if n >= 8192 else 128
    grp = meta[epg].astype(jnp.int32)
    rank = meta[epg + 1].astype(jnp.int32)
    counts = cnt[0, :n_grp].astype(jnp.int32)
    n_tiles_g = (counts + tm2 - 1) // tm2
    tile_end = jnp.cumsum(n_tiles_g)
    tile_start = tile_end - n_tiles_g
    dest = rank
    for gi in range(n_grp):
        dest = dest + jnp.where(grp == gi, tile_start[gi] * tm2, 0)
    n_tiles = n // tm2 + n_grp
    src = pl.pallas_call(
        _invperm_kernel,
        in_specs=[pl.BlockSpec(memory_space=pltpu.SMEM)],
        out_specs=pl.BlockSpec(memory_space=pltpu.SMEM),
        out_shape=jax.ShapeDtypeStruct((n_tiles * tm2,), jnp.int32),
        name=name + "_invperm",
    )(dest)
    jt = jnp.arange(n_tiles, dtype=jnp.int32)
    tval = (jt < tile_end[-1]).astype(jnp.int32)
    tgrp = jnp.sum((jt[:, None] >= tile_end[None, :]).astype(jnp.int32), axis=1)
    last_grp = jnp.sum((tile_end[-1] - 1 >= tile_end).astype(jnp.int32))
    tgrp = layer * n_grp + jnp.minimum(jnp.where(tval == 1, tgrp, last_grp), n_grp - 1)

    wspec = lambda shape: pl.BlockSpec(shape, lambda j, src, tgrp, tval: (tgrp[j], 0, 0, 0))
    ys = pl.pallas_call(
        functools.partial(_expert_kernel, epg=epg, d=d),
        grid_spec=pltpu.PrefetchScalarGridSpec(
            num_scalar_prefetch=3,
            grid=(n_tiles,),
            in_specs=[
                pl.BlockSpec(memory_space=pl.ANY),
                wspec((1, epg, d, dff)), wspec((1, epg, d, dff)), wspec((1, epg, dff, d)),
            ],
            out_specs=pl.BlockSpec((tm2, d), lambda j, src, tgrp, tval: (j, 0)),
            scratch_shapes=[pltpu.VMEM((2, tm2 // SUBLANES, SUBLANES, d + LANES), F32),
                            pltpu.SemaphoreType.DMA((2,)),
                            pltpu.VMEM((epg, d, dff), BF16), pltpu.VMEM((epg, d, dff), BF16),
                            pltpu.VMEM((epg, dff, d), BF16)],
        ),
        out_shape=jax.ShapeDtypeStruct((n_tiles * tm2, d), F32),
        compiler_params=_cparams("arbitrary"),
        name=name + "_experts",
    )(src, tgrp, tval, hx, w1, w3, w2)

    tm = _row_tile(n, t_len, 512 if t_len % 512 == 0 else 256)
    gate_op, gate_spec = _mod_operand(gate, t_len, tm)
    final = final_g is not None
    in_specs = [
        pl.BlockSpec(memory_space=pl.ANY),
        pl.BlockSpec((tm, d), lambda i, dest: (i, 0)),
        gate_spec,
    ]
    args = [dest, ys, x, gate_op]
    if final:
        in_specs.append(pl.BlockSpec((1, d), lambda i, dest: (0, 0)))
        args.append(final_g.reshape(1, d))
    return pl.pallas_call(
        functools.partial(_combine_kernel, final=final),
        grid_spec=pltpu.PrefetchScalarGridSpec(
            num_scalar_prefetch=1,
            grid=(n // tm,),
            in_specs=in_specs,
            out_specs=pl.BlockSpec((tm, d), lambda i, dest: (i, 0)),
            scratch_shapes=[pltpu.VMEM((2, tm // SUBLANES, SUBLANES, d), F32),
                            pltpu.SemaphoreType.DMA((2,))],
        ),
        out_shape=jax.ShapeDtypeStruct((n, d), F32),
        compiler_params=_cparams("arbitrary"),
        name=name + "_combine",
    )(*args)


def _conv_taps(xm, halo, conv_w, conv_b, row_in_seq, axis):
    acc = conv_b + xm * conv_w[ML_CONV - 1]
    for s in range(1, ML_CONV):
        shifted = pltpu.roll(xm, s, axis)
        fill = pltpu.roll(halo, (s + SUBLANES - (ML_CONV - 1)) % SUBLANES, axis)
        if axis == 0:
            top = jnp.where(row_in_seq < s, fill, shifted[0:SUBLANES])
            shifted = jnp.concatenate([top, shifted[SUBLANES:]], axis=0)
        else:
            shifted = jnp.where(row_in_seq < s, fill, shifted)
        acc = acc + shifted * conv_w[ML_CONV - 1 - s]
    return acc


def _up_conv_qkv_kernel(x_ref, g_ref, sc_ref, sh_ref, wup_ref, c0_ref, cw_ref, cb_ref,
                        wqk_ref, wv_ref, wg_ref, bg_ref,
                        q_ref, k_ref, v_ref, xc_ref, z_ref, gates_ref, tail_ref, carry_scr,
                        *, short_seq, steps_per_seq, k_scale):
    rows = x_ref.shape[0]
    inner = z_ref.shape[1]
    hb = _rms_mod(x_ref[...], g_ref[...], _mod_rows(sc_ref, rows), _mod_rows(sh_ref, rows)).astype(BF16)
    if not short_seq:
        @pl.when(pl.program_id(0) % steps_per_seq == 0)
        def _():
            carry_scr[...] = c0_ref[0]

    width = 2 * MXU_DIM

    def conv_qkv(c0, xm):
        cs = slice(c0, c0 + width)
        conv_w = [cw_ref[i:i + 1, cs] for i in range(ML_CONV)]
        conv_b = cb_ref[:, cs]
        if short_seq:
            xm3 = xm.reshape(rows // SUBLANES, SUBLANES, width)
            t_idx = lax.broadcasted_iota(jnp.int32, xm3.shape, 1)
            conv = _conv_taps(xm3, c0_ref[:, :, cs], conv_w, conv_b, t_idx, 1).reshape(rows, width)
            tail_ref[:, :, cs] = pltpu.roll(xm3, ML_CONV - 1, 1)
        else:
            row8 = lax.broadcasted_iota(jnp.int32, (SUBLANES, width), 0)
            conv = _conv_taps(xm, carry_scr[:, cs], conv_w, conv_b, row8, 0)
            tail = pltpu.roll(xm[rows - SUBLANES:], ML_CONV - 1, 0)
            carry_scr[:, cs] = tail
            tail_ref[0, :, cs] = tail
        xc = _silu(conv)
        xc_ref[:, cs] = xc
        xcb = xc.astype(BF16)
        xmb = xm.astype(BF16)
        qkv = []
        for i in range(width // MXU_DIM):
            ls = slice(i * MXU_DIM, (i + 1) * MXU_DIM)
            ti = c0 // MXU_DIM + i
            os_ = slice(ti * MXU_DIM, (ti + 1) * MXU_DIM)
            qk = _dot(xcb[:, ls], wqk_ref[ti])
            qi = qk[:, :MXU_DIM]
            ki = qk[:, MXU_DIM:]
            vi = _dot(xmb[:, ls], wv_ref[ti])
            q_ref[:, os_] = qi.astype(q_ref.dtype)
            k_ref[:, os_] = (ki * k_scale).astype(k_ref.dtype)
            v_ref[:, os_] = vi.astype(v_ref.dtype)
            qkv.append((os_, qi.astype(BF16), ki.astype(BF16), vi.astype(BF16)))
        return qkv

    def gate_logits(acc, qkv):
        for os_, qi, ki, vi in qkv:
            acc = acc + _dot(qi, wg_ref[0, os_, :])
            acc = acc + _dot(ki, wg_ref[1, os_, :])
            acc = acc + _dot(vi, wg_ref[2, os_, :])
        return acc

    gates = bg_ref[...]
    xm_prev = None
    qkv_prev = None
    for c0 in range(0, inner + 2 * width, width):
        xm = None
        if c0 < inner:
            z_ref[:, c0:c0 + width] = _dot(hb, wup_ref[:, inner + c0:inner + c0 + width])
            xm = _dot(hb, wup_ref[:, c0:c0 + width])
        qkv = None
        if xm_prev is not None:
            qkv = conv_qkv(c0 - width, xm_prev)
        if qkv_prev is not None:
            gates = gate_logits(gates, qkv_prev)
        xm_prev, qkv_prev = xm, qkv
    lane = lax.broadcasted_iota(jnp.int32, gates.shape, 1)
    log_sig = jnp.minimum(gates, 0.0) - jnp.log1p(jnp.exp(-jnp.abs(gates)))
    gates_ref[...] = jnp.where(lane < ML_HEADS, gates, log_sig)


def _up_conv_qkv_call(x, g, sc, sh, wup, conv0, conv_w, conv_b, wqk, wv, wg, bg, b, t_len):
    n, d = x.shape
    inner = wup.shape[1] // 2
    short_seq = t_len == SUBLANES
    dh = inner // ML_HEADS
    k_scale = dh ** -0.5
    qkv_dtype = F32 if short_seq else BF16
    halo = jnp.zeros((b, SUBLANES, inner), F32)
    if conv0 is not None:
        halo = halo.at[:, :ML_CONV - 1].set(conv0)
    tm = min(256, n) if short_seq else math.gcd(512, t_len)
    seq_per_step = tm // t_len if short_seq else 1
    steps_per_seq = 1 if short_seq else t_len // tm
    sc_op, sc_spec = _mod_operand(sc, t_len, tm)
    sh_op, sh_spec = _mod_operand(sh, t_len, tm)
    seq_spec = pl.BlockSpec((seq_per_step, SUBLANES, inner), lambda i: (i // steps_per_seq, 0, 0))
    row_spec = lambda w: pl.BlockSpec((tm, w), lambda i: (i, 0))
    const2 = lambda s: pl.BlockSpec(s, lambda i: (0, 0), pipeline_mode=pl.Buffered(1))
    const3 = lambda s: pl.BlockSpec(s, lambda i: (0, 0, 0), pipeline_mode=pl.Buffered(1))
    q, k, v, xc, z, gates, tail = pl.pallas_call(
        functools.partial(_up_conv_qkv_kernel, short_seq=short_seq, steps_per_seq=steps_per_seq,
                          k_scale=k_scale),
        grid=(n // tm,),
        in_specs=[
            row_spec(d), const2((1, d)), sc_spec, sh_spec, const2(wup.shape),
            seq_spec, const2((ML_CONV, inner)), const2((1, inner)),
            const3(wqk.shape), const3(wv.shape), const3(wg.shape), const2((1, LANES)),
        ],
        out_specs=[row_spec(inner), row_spec(inner), row_spec(inner), row_spec(inner), row_spec(inner),
                   row_spec(LANES), seq_spec],
        out_shape=[
            jax.ShapeDtypeStruct((n, inner), qkv_dtype),
            jax.ShapeDtypeStruct((n, inner), qkv_dtype),
            jax.ShapeDtypeStruct((n, inner), qkv_dtype),
            jax.ShapeDtypeStruct((n, inner), F32),
            jax.ShapeDtypeStruct((n, inner), F32),
            jax.ShapeDtypeStruct((n, LANES), F32),
            jax.ShapeDtypeStruct((b, SUBLANES, inner), F32),
        ],
        scratch_shapes=[pltpu.VMEM((SUBLANES, inner), F32)],
        compiler_params=_cparams("arbitrary"),
        name="mlstm_up_conv_qkv",
    )(x, g.reshape(1, d), sc_op, sh_op, wup, halo, conv_w, conv_b.reshape(1, inner), wqk, wv, wg, bg)
    return q, k, v, xc, z, gates, tail[:, :ML_CONV - 1]


def _mlstm_kernel(*refs, has_state):
    if has_state:
        (q_ref, k_ref, v_ref, gates_ref, xc_ref, z_ref, nw_ref, sk_ref, c0_ref, n0_ref, m0_ref,
         o_ref, c_ref, n_ref, m_ref) = refs
    else:
        (q_ref, k_ref, v_ref, gates_ref, xc_ref, z_ref, nw_ref, sk_ref,
         o_ref, c_ref, n_ref, m_ref) = refs
    ch = pl.program_id(1)
    length = q_ref.shape[0]
    dh = q_ref.shape[1] // ML_HEADS

    @pl.when(ch == 0)
    def _():
        if has_state:
            c_ref[...] = c0_ref[...]
            n_ref[...] = n0_ref[...]
            m_ref[...] = m0_ref[...]
        else:
            c_ref[...] = jnp.zeros_like(c_ref)
            n_ref[...] = jnp.zeros_like(n_ref)
            m_ref[...] = jnp.zeros_like(m_ref)

    gates = gates_ref[...]
    gates_t = gates.T
    t_idx = lax.broadcasted_iota(jnp.int32, (length, length), 0)
    s_idx = lax.broadcasted_iota(jnp.int32, (length, length), 1)
    causal = s_idx <= t_idx
    m_all = m_ref[0]
    m_new = m_all
    lane = lax.broadcasted_iota(jnp.int32, m_all.shape, 1)

    def finish_head(h, s_raw, q_c, q_n, w, w_c, m_t):
        hs = slice(h * dh, (h + 1) * dh)
        s = s_raw * w
        num = w_c * q_c + _dot(s.astype(BF16), v_ref[:, hs].astype(BF16))
        den = w_c * q_n + jnp.sum(s, axis=-1, keepdims=True)
        hc = num / jnp.maximum(jnp.abs(den), jnp.exp(-m_t))
        mu = jnp.mean(hc, axis=-1, keepdims=True)
        dev = hc - mu
        var = jnp.mean(dev * dev, axis=-1, keepdims=True)
        hn = dev * lax.rsqrt(var + EPS) * nw_ref[:, hs]
        ho = (hn + sk_ref[:, hs] * xc_ref[:, hs]) * _silu(z_ref[:, hs])
        o_ref[:, hs] = ho.astype(o_ref.dtype)

    pending = None
    for h in range(ML_HEADS):
        hs = slice(h * dh, (h + 1) * dh)
        ig_col = gates[:, h:h + 1]
        lf_col = gates[:, ML_HEADS + h:ML_HEADS + h + 1]
        ig_row = gates_t[h:h + 1, :]
        lf_row = gates_t[ML_HEADS + h:ML_HEADS + h + 1, :]
        b_col = jnp.sum(jnp.where(causal, lf_row, 0.0), axis=1, keepdims=True)
        b_row = jnp.sum(jnp.where(t_idx <= s_idx, lf_col, 0.0), axis=0, keepdims=True)
        m_prev = m_all[:, h:h + 1]
        dm = jnp.where(causal, b_col - b_row + ig_row, -jnp.inf)
        a = b_col + m_prev
        m_t = jnp.maximum(a, jnp.max(dm, axis=1, keepdims=True))
        w_c = jnp.exp(a - m_t)
        w = jnp.exp(dm - m_t)
        qh = q_ref[:, hs].astype(BF16)
        kh = k_ref[:, hs]
        vh = v_ref[:, hs].astype(BF16)
        c_h = c_ref[0, h]
        n_h = n_ref[0, :, hs]
        s_raw = _dot_nt(qh, kh.astype(BF16))
        q_c = _dot(qh, c_h.astype(BF16))
        q_n = jnp.sum(qh.astype(F32) * n_h, axis=-1, keepdims=True)
        m_last = m_t[length - 1:length]
        b_last = b_col[length - 1:length]
        wl_c = jnp.exp(a[length - 1:length] - m_last)
        wl_col = jnp.exp(b_last - b_col + ig_col - m_last)
        kw = kh.astype(F32) * wl_col
        c_ref[0, h] = wl_c * c_h + _dot_tn(kw.astype(BF16), vh)
        n_ref[0, :, hs] = wl_c * n_h + jnp.sum(kw, axis=0, keepdims=True)
        m_new = jnp.where(lane == h, m_last, m_new)
        if pending is not None:
            finish_head(*pending)
        pending = (h, s_raw, q_c, q_n, w, w_c, m_t)
    finish_head(*pending)
    m_ref[0] = m_new


def _mlstm_call(q, k, v, gates, xc, z, norm_w, skip, c0, n0, m0, b, t_len):
    n, inner = q.shape
    dh = inner // ML_HEADS
    has_state = c0 is not None
    length = math.gcd(t_len, ML_CHUNK)
    nc = t_len // length
    row = lambda w: pl.BlockSpec((length, w), lambda i, c: (i * nc + c, 0))
    const = lambda w: pl.BlockSpec((1, w), lambda i, c: (0, 0))
    c_spec = pl.BlockSpec((1, ML_HEADS, dh, dh), lambda i, c: (i, 0, 0, 0))
    n_spec = pl.BlockSpec((1, 1, inner), lambda i, c: (i, 0, 0))
    m_spec = pl.BlockSpec((1, 1, LANES), lambda i, c: (i, 0, 0))
    in_specs = [row(inner), row(inner), row(inner), row(LANES), row(inner), row(inner),
                const(inner), const(inner)]
    args = [q, k, v, gates, xc, z, norm_w.reshape(1, inner), skip.reshape(1, inner)]
    if has_state:
        in_specs += [c_spec, n_spec, m_spec]
        m0_pad = jnp.zeros((b, 1, LANES), F32).at[:, 0, :ML_HEADS].set(m0)
        args += [c0, n0.reshape(b, 1, inner), m0_pad]
    ho, c_t, n_t, m_t = pl.pallas_call(
        functools.partial(_mlstm_kernel, has_state=has_state),
        grid=(b, nc),
        in_specs=in_specs,
        out_specs=[row(inner), c_spec, n_spec, m_spec],
        out_shape=[
            jax.ShapeDtypeStruct((n, inner), BF16 if length % 16 == 0 else F32),
            jax.ShapeDtypeStruct((b, ML_HEADS, dh, dh), F32),
            jax.ShapeDtypeStruct((b, 1, inner), F32),
            jax.ShapeDtypeStruct((b, 1, LANES), F32),
        ],
        compiler_params=_cparams("parallel", "arbitrary"),
        name="mlstm_scan",
    )(*args)
    return ho, c_t, n_t.reshape(b, ML_HEADS, dh), m_t[:, 0, :ML_HEADS]


def _block_diag_tiles(w):
    n_blk, blk, _ = w.shape
    per = MXU_DIM // blk
    rows = w.reshape(n_blk // per, per, blk, blk).transpose(0, 1, 3, 2).reshape(n_blk // per, MXU_DIM, blk)
    r_blk = np.arange(MXU_DIM)[:, None] // blk
    c_blk = np.arange(MXU_DIM)[None, :] // blk
    same_block = jnp.asarray((r_blk == c_blk).astype(np.float32))
    return jnp.tile(rows, (1, 1, per)) * same_block


def _prep_weights(p):
    w = {}
    w['hg_win'] = p['hg_win'].astype(BF16)
    w['hg_wo'] = p['hg_wo'].astype(BF16)
    w['ml_wup'] = p['ml_wup'].astype(BF16)
    w['ml_wdown'] = p['ml_wdown'].astype(BF16)
    dep, n_exp, d_model, dff = p['moe_w1'].shape
    n_stack = dep * n_exp // MOE_EPG
    w['moe_w1'] = p['moe_w1'].reshape(n_stack, MOE_EPG, d_model, dff)
    w['moe_w3'] = p['moe_w3'].reshape(n_stack, MOE_EPG, d_model, dff)
    w['moe_w2'] = p['moe_w2'].reshape(n_stack, MOE_EPG, dff, d_model)
    depth, d, g = p['moe_wrg'].shape
    n_exp = g * p['moe_wre'].shape[-1]
    wr = jnp.zeros((depth, d, LANES), F32)
    wr = wr.at[:, :, :g].set(p['moe_wrg']).at[:, :, g:g + n_exp].set(p['moe_wre'].reshape(depth, d, n_exp))
    w['moe_wr'] = wr.astype(BF16)
    br = jnp.zeros((depth, 1, LANES), F32)
    br = br.at[:, 0, :g].set(p['moe_brg']).at[:, 0, g:g + n_exp].set(p['moe_bre'].reshape(depth, n_exp))
    w['moe_br'] = br
    n_b = p['ml_wq'].shape[0]
    wq = jnp.stack([_block_diag_tiles(p['ml_wq'][j]) for j in range(n_b)])
    wk = jnp.stack([_block_diag_tiles(p['ml_wk'][j]) for j in range(n_b)])
    wv = jnp.stack([_block_diag_tiles(p['ml_wv'][j]) for j in range(n_b)])
    w['ml_wqk'] = jnp.concatenate([wq, wk], axis=-1).astype(BF16)
    w['ml_wv'] = wv.astype(BF16)
    inner = p['ml_conv_b'].shape[-1]
    wg = jnp.zeros((n_b, 3 * inner, LANES), F32)
    wg = wg.at[:, :, :ML_HEADS].set(p['ml_wig']).at[:, :, ML_HEADS:2 * ML_HEADS].set(p['ml_wfg'])
    w['ml_wg'] = wg.reshape(n_b, 3, inner, LANES).astype(BF16)
    bg = jnp.zeros((n_b, 1, LANES), F32)
    bg = bg.at[:, 0, :ML_HEADS].set(p['ml_big']).at[:, 0, ML_HEADS:2 * ML_HEADS].set(p['ml_bfg'])
    w['ml_bg'] = bg
    return w


def _lb_kernel(lb_ref, o_ref):
    x = lb_ref[...]
    mx = jnp.max(x, axis=0, keepdims=True)
    ex = jnp.exp(x - mx)
    sm = ex / jnp.sum(ex, axis=0, keepdims=True)
    rows = []
    run = jnp.zeros_like(sm[0:1])
    for i in range(x.shape[0]):
        run = run + sm[i:i + 1]
        rows.append(run)
    o_ref[...] = jnp.concatenate(rows, axis=0)


def _lb_call(hg_lb):
    return pl.pallas_call(
        _lb_kernel,
        out_shape=jax.ShapeDtypeStruct(hg_lb.shape, F32),
        name="hgrn_lower_bound",
    )(hg_lb)


def _trunk(x3, mods, s_hg, s_c, s_n, s_m, s_conv, p, w, lb_all):
    b, t_len, d = x3.shape
    n = b * t_len
    x = x3.reshape(n, d)
    depth = p['norm_g'].shape[0]
    new_hg, new_c, new_n, new_m, new_conv = [], [], [], [], []
    for l in range(depth):
        sh1, sc1, g1, sh2, sc2, g2 = [mods[l][:, i * d:(i + 1) * d] for i in range(6)]
        if l % 2 == 0:
            a = l // 2
            proj = _norm_mm_call(x, p['norm_g'][l, 0], sc1, sh1, w['hg_win'][a], t_len, "hgrn_in_proj")
            o, s_t = _gla_call(proj, lb_all[l], p['hg_norm'][a], None if s_hg is None else s_hg[a], b, t_len)
            new_hg.append(s_t)
            x = _mm_res_call(o, w['hg_wo'][a], x, g1, t_len, "hgrn_out_proj")
        else:
            j = l // 2
            q, k, v, xc, z, gates, conv_tail = _up_conv_qkv_call(
                x, p['norm_g'][l, 0], sc1, sh1, w['ml_wup'][j],
                None if s_conv is None else s_conv[j], p['ml_conv_w'][j], p['ml_conv_b'][j],
                w['ml_wqk'][j], w['ml_wv'][j], w['ml_wg'][j], w['ml_bg'][j], b, t_len)
            ho, c_t, n_t, m_t = _mlstm_call(
                q, k, v, gates, xc, z, p['ml_norm'][j], p['ml_skip'][j],
                None if s_c is None else s_c[j], None if s_n is None else s_n[j],
                None if s_m is None else s_m[j], b, t_len)
            new_c.append(c_t)
            new_n.append(n_t)
            new_m.append(m_t)
            new_conv.append(conv_tail)
            x = _mm_res_call(ho.astype(BF16), w['ml_wdown'][j], x, g1, t_len, "mlstm_down_proj")
        x = _moe_call(x, p['norm_g'][l, 1], sc2, sh2, g2, w['moe_wr'][l], w['moe_br'][l],
                      w['moe_w1'], w['moe_w3'], w['moe_w2'], l,
                      p['final_g'] if l == depth - 1 else None, t_len, "moe_layer%d" % l)
    return (x.reshape(b, t_len, d), jnp.stack(new_hg), jnp.stack(new_c), jnp.stack(new_n),
            jnp.stack(new_m), jnp.stack(new_conv))


def kernel(x_prompt, x_sample, c_prompt, c_sample, state_hgrn, state_mlstm_c, state_mlstm_n, state_mlstm_m, state_conv, w_ada, b_ada, norm_g, final_g, hg_win, hg_wo, hg_norm, hg_lb, ml_wup, ml_conv_w, ml_conv_b, ml_wq, ml_wk, ml_wv, ml_wig, ml_big, ml_wfg, ml_bfg, ml_norm, ml_skip, ml_wdown, moe_wrg, moe_brg, moe_wre, moe_bre, moe_w1, moe_w3, moe_w2):
    p = dict(w_ada=w_ada, b_ada=b_ada, norm_g=norm_g, final_g=final_g,
             hg_win=hg_win, hg_wo=hg_wo, hg_norm=hg_norm, hg_lb=hg_lb,
             ml_wup=ml_wup, ml_conv_w=ml_conv_w, ml_conv_b=ml_conv_b, ml_wq=ml_wq, ml_wk=ml_wk, ml_wv=ml_wv,
             ml_wig=ml_wig, ml_big=ml_big, ml_wfg=ml_wfg, ml_bfg=ml_bfg, ml_norm=ml_norm, ml_skip=ml_skip,
             ml_wdown=ml_wdown, moe_wrg=moe_wrg, moe_brg=moe_brg, moe_wre=moe_wre, moe_bre=moe_bre,
             moe_w1=moe_w1, moe_w3=moe_w3, moe_w2=moe_w2)
    w = _prep_weights(p)
    lb_all = _lb_call(hg_lb)
    bp = x_prompt.shape[0]
    c_all = jnp.concatenate([c_prompt, c_sample], axis=0)
    mod_all = _ada_call(c_all, w_ada, b_ada)
    mods_p = [mod_all[l, :bp] for l in range(mod_all.shape[0])]
    mods_s = [mod_all[l, bp:] for l in range(mod_all.shape[0])]
    y_p, hg_p, mc_p, mn_p, mm_p, conv_p = _trunk(x_prompt, mods_p, None, None, None, None, None, p, w, lb_all)
    y_s, hg_s, mc_s, mn_s, mm_s, conv_s = _trunk(x_sample, mods_s, state_hgrn, state_mlstm_c, state_mlstm_n,
                                                 state_mlstm_m, state_conv, p, w, lb_all)
    return (y_p, y_s, hg_p, mc_p, mn_p, mm_p, conv_p, hg_s, mc_s, mn_s, mm_s, conv_s)
```

```python
import functools
import math

import numpy as np
import jax
import jax.numpy as jnp
from jax import lax
from jax.experimental import pallas as pl
from jax.experimental.pallas import tpu as pltpu

F32 = jnp.float32
BF16 = jnp.bfloat16
EPS = 1e-6

HG_DK = 128
ML_HEADS = 4
ML_CONV = 4
ML_QKV_BLOCK = 4
MOE_GROUPS = 4
MOE_EPG = 4
CHUNK = 64
GLA_CHUNKS_PER_STEP = 8
ML_CHUNK = 256

LANES = 128
SUBLANES = 8
MXU_DIM = 256
VMEM_LIMIT_BYTES = 56 * 1024 * 1024


def _cparams(*sem):
    return pltpu.CompilerParams(dimension_semantics=sem, vmem_limit_bytes=VMEM_LIMIT_BYTES)


def _silu(x):
    return x * jax.nn.sigmoid(x)


def _dot(a, b):
    return jnp.dot(a, b, preferred_element_type=F32)


def _dot_nt(a, b):
    return lax.dot_general(a, b, (((1,), (1,)), ((), ())), preferred_element_type=F32)


def _dot_tn(a, b):
    return lax.dot_general(a, b, (((0,), (0,)), ((), ())), preferred_element_type=F32)


def _rms_mod(x, g, sc, sh):
    ms = jnp.mean(x * x, axis=-1, keepdims=True)
    h = x * lax.rsqrt(ms + EPS) * g
    return h * (1.0 + sc) + sh


def _ada_kernel(c_ref, w_ref, b_ref, o_ref):
    cm = _silu(c_ref[...]).astype(BF16)
    o_ref[0] = _dot(cm, w_ref[0].astype(BF16)) + b_ref[0]


def _ada_call(c_all, w_ada, b_ada):
    depth, d, n_out = w_ada.shape
    m = c_all.shape[0]
    tn = 512
    return pl.pallas_call(
        _ada_kernel,
        grid=(depth, n_out // tn),
        in_specs=[
            pl.BlockSpec((m, d), lambda l, j: (0, 0)),
            pl.BlockSpec((1, d, tn), lambda l, j: (l, 0, j)),
            pl.BlockSpec((1, 1, tn), lambda l, j: (l, 0, j)),
        ],
        out_specs=pl.BlockSpec((1, m, tn), lambda l, j: (l, 0, j)),
        out_shape=jax.ShapeDtypeStruct((depth, m, n_out), F32),
        compiler_params=_cparams("parallel", "parallel"),
        name="ada_mod",
    )(c_all, w_ada, b_ada.reshape(depth, 1, n_out))


def _mod_operand(m, t_len, tm):
    b, d = m.shape
    if t_len % tm == 0:
        per_b = t_len // tm
        return m.reshape(b, 1, d), pl.BlockSpec((1, 1, d), lambda i, *_: (i // per_b, 0, 0))
    assert tm % t_len == 0
    nb = tm // t_len
    return m.reshape(b // nb, nb, 1, d), pl.BlockSpec((1, nb, 1, d), lambda i, *_: (i, 0, 0, 0))


def _mod_rows(ref, rows):
    v = ref[0]
    if v.ndim == 2:
        return v
    nb, _, d = v.shape
    return jnp.broadcast_to(v, (nb, rows // nb, d)).reshape(rows, d)


def _row_tile(n, t_len, target):
    tm = min(target, n)
    while n % tm or (t_len % tm and tm % t_len):
        tm //= 2
    return tm


def _norm_mm_kernel(x_ref, g_ref, sc_ref, sh_ref, w_ref, o_ref, *, col_chunk):
    rows = x_ref.shape[0]
    hb = _rms_mod(x_ref[...], g_ref[...], _mod_rows(sc_ref, rows), _mod_rows(sh_ref, rows)).astype(BF16)
    for c0 in range(0, o_ref.shape[1], col_chunk):
        o_ref[:, c0:c0 + col_chunk] = _dot(hb, w_ref[:, c0:c0 + col_chunk])


def _norm_mm_call(x, g, sc, sh, w, t_len, name):
    n, d = x.shape
    n_out = w.shape[1]
    tm = _row_tile(n, t_len, 512)
    sc_op, sc_spec = _mod_operand(sc, t_len, tm)
    sh_op, sh_spec = _mod_operand(sh, t_len, tm)
    return pl.pallas_call(
        functools.partial(_norm_mm_kernel, col_chunk=512),
        grid=(n // tm,),
        in_specs=[
            pl.BlockSpec((tm, d), lambda i: (i, 0)),
            pl.BlockSpec((1, d), lambda i: (0, 0)),
            sc_spec, sh_spec,
            pl.BlockSpec((d, n_out), lambda i: (0, 0)),
        ],
        out_specs=pl.BlockSpec((tm, n_out), lambda i: (i, 0)),
        out_shape=jax.ShapeDtypeStruct((n, n_out), F32),
        compiler_params=_cparams("parallel"),
        name=name,
    )(x, g.reshape(1, d), sc_op, sh_op, w)


def _gla_tables(t_sub, nseq):
    r = t_sub * nseq
    levels = []
    m = t_sub // 2
    while m >= 1:
        levels.append(m)
        m //= 2
    n_lev = len(levels)
    tril = np.zeros((r, r), np.float32)
    mask = np.zeros((n_lev + 1, r, r), np.float32)
    for li, m in enumerate(levels):
        for row in range(r):
            blk = (row // (2 * m)) * 2 * m
            if row - blk >= m:
                mask[li, row, blk:blk + m] = 1.0
    for row in range(r):
        s0 = (row // t_sub) * t_sub
        tril[row, s0:row + 1] = 1.0
        mask[n_lev, row, row] = 1.0
    return tril, mask, levels


def _bcast_block_row(b, block, row_in_block):
    parts = [jnp.broadcast_to(b[s + row_in_block:s + row_in_block + 1, :], (block, b.shape[1]))
             for s in range(0, b.shape[0], block)]
    return parts[0] if len(parts) == 1 else jnp.concatenate(parts, axis=0)


def _level_decay(b, m):
    r = b.shape[0]
    pos = lax.broadcasted_iota(jnp.int32, b.shape, 0) & (2 * m - 1)
    if 2 * m >= SUBLANES:
        b_mid = _bcast_block_row(b, 2 * m, m - 1)
    else:
        b_mid = b
        for p in range(2 * m):
            if p != m - 1:
                b_mid = jnp.where(pos == p, pltpu.roll(b, (p - (m - 1)) % r, 0), b_mid)
    return jnp.exp2(jnp.where(pos >= m, b - b_mid, b_mid - b))


def _gla_kernel(*refs, t_sub, nseq, n_ch, levels, n_heads, has_state, steps_per_seq):
    if has_state:
        proj_ref, lb_ref, gn_ref, tril_ref, mask_ref, s0_ref, o_ref, sout_ref, st_ref = refs
    else:
        proj_ref, lb_ref, gn_ref, tril_ref, mask_ref, o_ref, sout_ref, st_ref = refs
    r = t_sub * nseq
    n_lev = len(levels)
    dk = HG_DK
    hk = n_heads * dk
    c = lax.rem(pl.program_id(0), steps_per_seq)

    @pl.when(c == 0)
    def _():
        if has_state:
            for j in range(nseq):
                for h in range(n_heads):
                    st_ref[j, h] = s0_ref[j, h].T
        else:
            st_ref[...] = jnp.zeros_like(st_ref)

    gn = gn_ref[...]
    tril = tril_ref[...]

    def finish_head(rows, h, a, qh, kh, e_cum_h, e_end_h, d_last):
        hs = slice(h * dk, (h + 1) * dk)
        vb = proj_ref[rows, 2 * hk + h * dk:2 * hk + (h + 1) * dk].astype(BF16)
        o_intra = _dot(a.astype(BF16), vb)
        qd = qh * e_cum_h
        kd = kh * e_end_h
        o_parts = []
        for j in range(nseq):
            rs = slice(j * t_sub, (j + 1) * t_sub)
            st = st_ref[j, h]
            o_parts.append(_dot_nt(qd[rs], st.astype(BF16)))
            st_ref[j, h] = st * d_last[j] + _dot_tn(vb[rs], kd[rs])
        o_inter = o_parts[0] if nseq == 1 else jnp.concatenate(o_parts, axis=0)
        o = o_intra + o_inter
        o = o * lax.rsqrt(jnp.mean(o * o, axis=-1, keepdims=True) + EPS) * gn
        zg = proj_ref[rows, 3 * hk + h * dk:3 * hk + (h + 1) * dk]
        o_ref[rows, hs] = (o * _silu(zg)).astype(BF16)

    pending = None
    head_group = n_heads if nseq == 1 else 1
    for ci in range(n_ch):
        rows = slice(ci * r, (ci + 1) * r)
        for g0 in range(0, n_heads, head_group):
            gw = head_group * dk
            gs = slice(g0 * dk, g0 * dk + gw)
            lb = lb_ref[:, gs]
            zq = proj_ref[rows, g0 * dk:g0 * dk + gw]
            zf = proj_ref[rows, hk + g0 * dk:hk + g0 * dk + gw]
            f = lb + (1.0 - lb) * jax.nn.sigmoid(zf)
            lf = jnp.log2(f)
            q = _silu(zq).astype(BF16)
            k = (1.0 - f).astype(BF16)

            p0 = lf.astype(BF16)
            r1 = lf - p0.astype(F32)
            p1 = r1.astype(BF16)
            p2 = (r1 - p1.astype(F32)).astype(BF16)
            b = _dot(tril, p0) + _dot(tril, p1) + _dot(tril, p2)
            e_cum = jnp.exp2(b)
            e_cum_b = e_cum.astype(BF16)
            e_end_b = jnp.exp2(_bcast_block_row(b, t_sub, t_sub - 1) - b).astype(BF16)
            zs = [_level_decay(b, m).astype(BF16) for m in levels]

            for hh in range(head_group):
                h = g0 + hh
                ls = slice(hh * dk, (hh + 1) * dk)
                qh = q[:, ls]
                kh = k[:, ls]
                a = _dot_nt(qh, kh) * mask_ref[n_lev]
                for li in range(n_lev):
                    z = zs[li][:, ls]
                    a = a + _dot_nt(qh * z, kh * z) * mask_ref[li]
                item = (rows, h, a, qh, kh, e_cum_b[:, ls], e_end_b[:, ls],
                        [e_cum[(j + 1) * t_sub - 1:(j + 1) * t_sub, ls] for j in range(nseq)])
                if nseq > 1:
                    finish_head(*item)
                else:
                    if pending is not None:
                        finish_head(*pending)
                    pending = item
    if pending is not None:
        finish_head(*pending)

    @pl.when(c == steps_per_seq - 1)
    def _():
        for j in range(nseq):
            for h in range(n_heads):
                sout_ref[j, h] = st_ref[j, h].T


def _gla_call(proj, lb, gn, s0, b, t_len):
    n = proj.shape[0]
    hk = lb.shape[-1]
    n_heads = hk // HG_DK
    has_state = s0 is not None
    if t_len % CHUNK == 0:
        t_sub, nseq = CHUNK, 1
    else:
        t_sub, nseq = t_len, CHUNK // t_len
        assert t_sub * nseq == CHUNK and b % nseq == 0
    r = t_sub * nseq
    tril, mask, levels = _gla_tables(t_sub, nseq)
    n_ch = math.gcd(GLA_CHUNKS_PER_STEP, t_len // t_sub)
    rows = n_ch * r
    steps_per_seq = t_len // (t_sub * n_ch)
    n_steps = n // rows
    const2 = lambda s: pl.BlockSpec(s, lambda i: (0, 0))
    in_specs = [pl.BlockSpec((rows, proj.shape[1]), lambda i: (i, 0)),
                const2((1, hk)), const2((1, HG_DK)), const2(tril.shape),
                pl.BlockSpec(mask.shape, lambda i: (0, 0, 0))]
    args = [proj, lb.reshape(1, hk), gn.reshape(1, HG_DK), jnp.asarray(tril, BF16), jnp.asarray(mask, F32)]
    s_spec = pl.BlockSpec((nseq, n_heads, HG_DK, HG_DK), lambda i: (i // steps_per_seq, 0, 0, 0))
    if has_state:
        in_specs.append(s_spec)
        args.append(s0)
    o, s_out = pl.pallas_call(
        functools.partial(_gla_kernel, t_sub=t_sub, nseq=nseq, n_ch=n_ch, levels=tuple(levels),
                          n_heads=n_heads, has_state=has_state, steps_per_seq=steps_per_seq),
        grid=(n_steps,),
        in_specs=in_specs,
        out_specs=[
            pl.BlockSpec((rows, hk), lambda i: (i, 0)),
            s_spec,
        ],
        out_shape=[
            jax.ShapeDtypeStruct((n, hk), BF16),
            jax.ShapeDtypeStruct((b, n_heads, HG_DK, HG_DK), F32),
        ],
        scratch_shapes=[pltpu.VMEM((nseq, n_heads, HG_DK, HG_DK), F32)],
        compiler_params=_cparams("arbitrary"),
        name="gla_scan",
    )(*args)
    return o, s_out


def _route_kernel(a_ref, wmix_ref, gmix_ref, x_ref, g_ref, sc_ref, sh_ref, wr_ref, br_ref, tril_ref,
                  xnew_ref, hx_ref, meta_ref, cnt_ref, *, n_groups, epg):
    rows, d = x_ref.shape
    neg = -jnp.inf
    far = float(LANES)

    @pl.when(pl.program_id(0) == 0)
    def _():
        cnt_ref[...] = jnp.zeros_like(cnt_ref)

    x = x_ref[...] + _mod_rows(gmix_ref, rows) * _dot(a_ref[...], wmix_ref[...])
    xnew_ref[...] = x
    h = _rms_mod(x, g_ref[...], _mod_rows(sc_ref, rows), _mod_rows(sh_ref, rows))
    logit = _dot(h.astype(BF16), wr_ref[...]) + br_ref[...]
    lane = lax.broadcasted_iota(jnp.int32, logit.shape, 1).astype(F32)
    gm = lane < n_groups
    gmax = jnp.max(jnp.where(gm, logit, neg), axis=-1, keepdims=True)
    gstar = jnp.min(jnp.where(gm, jnp.where(logit == gmax, lane, far), far), axis=-1, keepdims=True)
    psum = jnp.sum(jnp.where(gm, jnp.exp(logit - gmax), 0.0), axis=-1, keepdims=True)
    pstar = 1.0 / psum
    lo = n_groups + gstar * epg
    em = jnp.where(lane >= lo, jnp.where(lane < lo + epg, 1.0, 0.0), 0.0)
    l1 = jnp.where(em > 0.0, logit, neg)
    v1 = jnp.max(l1, axis=-1, keepdims=True)
    i1 = jnp.min(jnp.where(l1 == v1, lane, far), axis=-1, keepdims=True)
    l2 = jnp.where(lane == i1, neg, l1)
    v2 = jnp.max(l2, axis=-1, keepdims=True)
    i2 = jnp.min(jnp.where(l2 == v2, lane, far), axis=-1, keepdims=True)
    e2 = jnp.exp(v2 - v1)
    wt1 = pstar / (1.0 + e2)
    wt2 = pstar * e2 / (1.0 + e2)
    onehot = jnp.where(lane == gstar, 1.0, 0.0)
    within = _dot(tril_ref[...], onehot.astype(BF16))
    carry = cnt_ref[...]
    rank = jnp.sum(onehot * (within + carry), axis=-1, keepdims=True)
    cnt_ref[...] = carry + jnp.sum(onehot, axis=0, keepdims=True)
    meta = (jnp.where(lane == i1 - lo, wt1, 0.0) + jnp.where(lane == i2 - lo, wt2, 0.0)
            + jnp.where(lane == epg, gstar, 0.0) + jnp.where(lane == epg + 1, rank, 0.0))
    hx_ref[:, :d] = h
    hx_ref[:, d:] = meta
    meta_ref[...] = meta.T[0:SUBLANES]


def _route_call(a, w_mix, gate_mix, x, g, sc, sh, wr, br, t_len, name):
    n, d = x.shape
    kdim = a.shape[1]
    tm = _row_tile(n, t_len, 1024 if t_len % 1024 == 0 else 256)
    gm_op, gm_spec = _mod_operand(gate_mix, t_len, tm)
    sc_op, sc_spec = _mod_operand(sc, t_len, tm)
    sh_op, sh_spec = _mod_operand(sh, t_len, tm)
    tril = jnp.asarray(np.tril(np.ones((tm, tm), np.float32), -1), BF16)
    const = lambda s: pl.BlockSpec(s, lambda i: (0, 0), pipeline_mode=pl.Buffered(1))
    return pl.pallas_call(
        functools.partial(_route_kernel, n_groups=MOE_GROUPS, epg=MOE_EPG),
        grid=(n // tm,),
        in_specs=[
            pl.BlockSpec((tm, kdim), lambda i: (i, 0)),
            const((kdim, d)), gm_spec,
            pl.BlockSpec((tm, d), lambda i: (i, 0)),
            const((1, d)), sc_spec, sh_spec,
            const((d, LANES)), const((1, LANES)), const((tm, tm)),
        ],
        out_specs=[
            pl.BlockSpec((tm, d), lambda i: (i, 0)),
            pl.BlockSpec((tm, d + LANES), lambda i: (i, 0)),
            pl.BlockSpec((SUBLANES, tm), lambda i: (0, i)),
            pl.BlockSpec((1, LANES), lambda i: (0, 0)),
        ],
        out_shape=[
            jax.ShapeDtypeStruct((n, d), F32),
            jax.ShapeDtypeStruct((n, d + LANES), F32),
            jax.ShapeDtypeStruct((SUBLANES, n), F32),
            jax.ShapeDtypeStruct((1, LANES), F32),
        ],
        compiler_params=_cparams("arbitrary"),
        name=name + "_route",
    )(a, w_mix, gm_op, x, g.reshape(1, d), sc_op, sh_op, wr, br, tril)


def _row_gather(idx_ref, src_hbm, buf, sem, tile, slot, start, unrolled=False):
    groups = buf.shape[1]
    base = tile * (groups * SUBLANES)

    def run(s):
        def body(i, carry):
            for u in range(SUBLANES):
                idx = idx_ref[base + i * SUBLANES + u]
                cp = pltpu.make_async_copy(src_hbm.at[pl.ds(idx, 1)],
                                           buf.at[s, i, pl.ds(u, 1)], sem.at[s])
                if start:
                    cp.start(priority=u % 2)
                else:
                    cp.wait()
            return carry

        if unrolled:
            for i in range(groups):
                body(i, 0)
        else:
            lax.fori_loop(0, groups, body, 0)

    for s in range(2):
        @pl.when(slot == s)
        def _():
            run(s)


def _expert_kernel(src_ref, tgrp_ref, tval_ref, hx_hbm, w1_ref, w3_ref, w2_ref, y_ref, hbuf, sem,
                   wb1, wb3, wb2, *, epg, d):
    j = pl.program_id(0)
    rows = y_ref.shape[0]

    @pl.when(jnp.logical_or(j == 0, tgrp_ref[j] != tgrp_ref[jnp.maximum(j - 1, 0)]))
    def _():
        for e in range(epg):
            wb1[e] = w1_ref[0, e].astype(BF16)
            wb3[e] = w3_ref[0, e].astype(BF16)
            wb2[e] = w2_ref[0, e].astype(BF16)

    @pl.when(j == 0)
    def _():
        @pl.when(tval_ref[0] == 1)
        def _():
            _row_gather(src_ref, hx_hbm, hbuf, sem, 0, 0, True)

    @pl.when(j + 1 < pl.num_programs(0))
    def _():
        @pl.when(tval_ref[j + 1] == 1)
        def _():
            _row_gather(src_ref, hx_hbm, hbuf, sem, j + 1, (j + 1) % 2, True, unrolled=True)

    @pl.when(tval_ref[j] == 1)
    def _():
        slot = j % 2
        _row_gather(src_ref, hx_hbm, hbuf, sem, j, slot, False)
        tile = hbuf[slot].reshape(rows, hbuf.shape[-1])
        hb = tile[:, :d].astype(BF16)
        hids = []
        for e in range(epg):
            a = _dot(hb, wb1[e])
            b = _dot(hb, wb3[e])
            hids.append((_silu(a) * b * tile[:, d + e:d + e + 1]).astype(BF16))
        hid = jnp.concatenate(hids, axis=1)
        y_ref[...] = _dot(hid, wb2[...].reshape(hid.shape[1], d))

    @pl.when(tval_ref[j] == 0)
    def _():
        y_ref[...] = jnp.zeros_like(y_ref)


def _combine_kernel(*refs, final):
    if final:
        dest_ref, ys_hbm, x_ref, gate_ref, fg_ref, o_ref, gbuf, sem = refs
    else:
        dest_ref, ys_hbm, x_ref, gate_ref, o_ref, gbuf, sem = refs
    i = pl.program_id(0)
    rows = x_ref.shape[0]

    @pl.when(i == 0)
    def _():
        _row_gather(dest_ref, ys_hbm, gbuf, sem, 0, 0, True)

    @pl.when(i + 1 < pl.num_programs(0))
    def _():
        _row_gather(dest_ref, ys_hbm, gbuf, sem, i + 1, (i + 1) % 2, True, unrolled=True)

    slot = i % 2
    _row_gather(dest_ref, ys_hbm, gbuf, sem, i, slot, False)
    y = x_ref[...] + _mod_rows(gate_ref, rows) * gbuf[slot].reshape(rows, gbuf.shape[-1])
    if final:
        y = y * lax.rsqrt(jnp.mean(y * y, axis=-1, keepdims=True) + EPS) * fg_ref[...]
    o_ref[...] = y


def _invperm_kernel(pad_ref, dest_ref, src_ref):
    def clear(i, carry):
        src_ref[i] = 0
        return carry

    def place(i, carry):
        src_ref[dest_ref[i]] = i
        return carry

    for rng in range(pad_ref.shape[0] // 2):
        lax.fori_loop(pad_ref[2 * rng], pad_ref[2 * rng + 1], clear, 0)
    lax.fori_loop(0, dest_ref.shape[0], place, 0, unroll=8)


def _moe_call(mix, x, g, sc, sh, gate, wr, br, w1, w3, w2, layer, final_g, t_len, name):
    n, d = x.shape
    epg, dff = w1.shape[1], w1.shape[3]
    n_grp = MOE_GROUPS
    x, hx, meta, cnt = _route_call(*mix, x, g, sc, sh, wr, br, t_len, name)

    tm2 = 512 if n >= 8192 else 128
    grp = meta[epg].astype(jnp.int32)
    rank = meta[epg + 1].astype(jnp.int32)
    counts = cnt[0, :n_grp].astype(jnp.int32)
    n_tiles_g = (counts + tm2 - 1) // tm2
    tile_end = jnp.cumsum(n_tiles_g)
    tile_start = tile_end - n_tiles_g
    dest = rank
    for gi in range(n_grp):
        dest = dest + jnp.where(grp == gi, tile_start[gi] * tm2, 0)
    n_tiles = n // tm2 + n_grp
    pad_lo = jnp.concatenate([tile_start * tm2 + counts, tile_end[-1:] * tm2])
    pad_hi = jnp.concatenate([tile_end * tm2, jnp.full((1,), n_tiles * tm2, jnp.int32)])
    src = pl.pallas_call(
        _invperm_kernel,
        in_specs=[pl.BlockSpec(memory_space=pltpu.SMEM), pl.BlockSpec(memory_space=pltpu.SMEM)],
        out_specs=pl.BlockSpec(memory_space=pltpu.SMEM),
        out_shape=jax.ShapeDtypeStruct((n_tiles * tm2,), jnp.int32),
        name=name + "_invperm",
    )(jnp.stack([pad_lo, pad_hi], axis=1).reshape(-1), dest)
    jt = jnp.arange(n_tiles, dtype=jnp.int32)
    tval = (jt < tile_end[-1]).astype(jnp.int32)
    tgrp = jnp.sum((jt[:, None] >= tile_end[None, :]).astype(jnp.int32), axis=1)
    last_grp = jnp.sum((tile_end[-1] - 1 >= tile_end).astype(jnp.int32))
    tgrp = layer * n_grp + jnp.minimum(jnp.where(tval == 1, tgrp, last_grp), n_grp - 1)

    wspec = lambda shape: pl.BlockSpec(shape, lambda j, src, tgrp, tval: (tgrp[j], 0, 0, 0))
    ys = pl.pallas_call(
        functools.partial(_expert_kernel, epg=epg, d=d),
        grid_spec=pltpu.PrefetchScalarGridSpec(
            num_scalar_prefetch=3,
            grid=(n_tiles,),
            in_specs=[
                pl.BlockSpec(memory_space=pl.ANY),
                wspec((1, epg, d, dff)), wspec((1, epg, d, dff)), wspec((1, epg, dff, d)),
            ],
            out_specs=pl.BlockSpec((tm2, d), lambda j, src, tgrp, tval: (j, 0)),
            scratch_shapes=[pltpu.VMEM((2, tm2 // SUBLANES, SUBLANES, d + LANES), F32),
                            pltpu.SemaphoreType.DMA((2,)),
                            pltpu.VMEM((epg, d, dff), BF16), pltpu.VMEM((epg, d, dff), BF16),
                            pltpu.VMEM((epg, dff, d), BF16)],
        ),
        out_shape=jax.ShapeDtypeStruct((n_tiles * tm2, d), F32),
        compiler_params=_cparams("arbitrary"),
        name=name + "_experts",
    )(src, tgrp, tval, hx, w1, w3, w2)

    tm = _row_tile(n, t_len, 512 if t_len % 512 == 0 else 256)
    gate_op, gate_spec = _mod_operand(gate, t_len, tm)
    final = final_g is not None
    in_specs = [
        pl.BlockSpec(memory_space=pl.ANY),
        pl.BlockSpec((tm, d), lambda i, dest: (i, 0)),
        gate_spec,
    ]
    args = [dest, ys, x, gate_op]
    if final:
        in_specs.append(pl.BlockSpec((1, d), lambda i, dest: (0, 0)))
        args.append(final_g.reshape(1, d))
    return pl.pallas_call(
        functools.partial(_combine_kernel, final=final),
        grid_spec=pltpu.PrefetchScalarGridSpec(
            num_scalar_prefetch=1,
            grid=(n // tm,),
            in_specs=in_specs,
            out_specs=pl.BlockSpec((tm, d), lambda i, dest: (i, 0)),
            scratch_shapes=[pltpu.VMEM((2, tm // SUBLANES, SUBLANES, d), F32),
                            pltpu.SemaphoreType.DMA((2,))],
        ),
        out_shape=jax.ShapeDtypeStruct((n, d), F32),
        compiler_params=_cparams("arbitrary"),
        name=name + "_combine",
    )(*args)


def _conv_taps(xm, halo, conv_w, conv_b, row_in_seq, axis):
    acc = conv_b + xm * conv_w[ML_CONV - 1]
    for s in range(1, ML_CONV):
        shifted = pltpu.roll(xm, s, axis)
        fill = pltpu.roll(halo, (s + SUBLANES - (ML_CONV - 1)) % SUBLANES, axis)
        if axis == 0:
            top = jnp.where(row_in_seq < s, fill, shifted[0:SUBLANES])
            shifted = jnp.concatenate([top, shifted[SUBLANES:]], axis=0)
        else:
            shifted = jnp.where(row_in_seq < s, fill, shifted)
        acc = acc + shifted * conv_w[ML_CONV - 1 - s]
    return acc


def _up_conv_qkv_kernel(x_ref, g_ref, sc_ref, sh_ref, wup_ref, c0_ref, cw_ref, cb_ref,
                        wqk_ref, wv_ref, wg_ref, bg_ref,
                        q_ref, k_ref, v_ref, xc_ref, z_ref, gates_ref, tail_ref, carry_scr,
                        *, short_seq, steps_per_seq, k_scale):
    rows = x_ref.shape[0]
    inner = z_ref.shape[1]
    hb = _rms_mod(x_ref[...], g_ref[...], _mod_rows(sc_ref, rows), _mod_rows(sh_ref, rows)).astype(BF16)
    if not short_seq:
        @pl.when(pl.program_id(0) % steps_per_seq == 0)
        def _():
            carry_scr[...] = c0_ref[0]

    width = 2 * MXU_DIM

    def conv_qkv(c0, xm):
        cs = slice(c0, c0 + width)
        conv_w = [cw_ref[i:i + 1, cs] for i in range(ML_CONV)]
        conv_b = cb_ref[:, cs]
        if short_seq:
            xm3 = xm.reshape(rows // SUBLANES, SUBLANES, width)
            t_idx = lax.broadcasted_iota(jnp.int32, xm3.shape, 1)
            conv = _conv_taps(xm3, c0_ref[:, :, cs], conv_w, conv_b, t_idx, 1).reshape(rows, width)
            tail_ref[:, :, cs] = pltpu.roll(xm3, ML_CONV - 1, 1)
        else:
            row8 = lax.broadcasted_iota(jnp.int32, (SUBLANES, width), 0)
            conv = _conv_taps(xm, carry_scr[:, cs], conv_w, conv_b, row8, 0)
            tail = pltpu.roll(xm[rows - SUBLANES:], ML_CONV - 1, 0)
            carry_scr[:, cs] = tail
            tail_ref[0, :, cs] = tail
        xc = _silu(conv)
        xc_ref[:, cs] = xc
        xcb = xc.astype(BF16)
        xmb = xm.astype(BF16)
        qkv = []
        for i in range(width // MXU_DIM):
            ls = slice(i * MXU_DIM, (i + 1) * MXU_DIM)
            ti = c0 // MXU_DIM + i
            os_ = slice(ti * MXU_DIM, (ti + 1) * MXU_DIM)
            qk = _dot(xcb[:, ls], wqk_ref[ti])
            qi = qk[:, :MXU_DIM]
            ki = qk[:, MXU_DIM:]
            vi = _dot(xmb[:, ls], wv_ref[ti])
            q_ref[:, os_] = qi.astype(q_ref.dtype)
            k_ref[:, os_] = (ki * k_scale).astype(k_ref.dtype)
            v_ref[:, os_] = vi.astype(v_ref.dtype)
            qkv.append((os_, qi.astype(BF16), ki.astype(BF16), vi.astype(BF16)))
        return qkv

    def gate_logits(acc, qkv):
        for os_, qi, ki, vi in qkv:
            acc = acc + _dot(qi, wg_ref[0, os_, :])
            acc = acc + _dot(ki, wg_ref[1, os_, :])
            acc = acc + _dot(vi, wg_ref[2, os_, :])
        return acc

    gates = bg_ref[...]
    xm_prev = None
    qkv_prev = None
    for c0 in range(0, inner + 2 * width, width):
        xm = None
        if c0 < inner:
            z_ref[:, c0:c0 + width] = _dot(hb, wup_ref[:, inner + c0:inner + c0 + width])
            xm = _dot(hb, wup_ref[:, c0:c0 + width])
        qkv = None
        if xm_prev is not None:
            qkv = conv_qkv(c0 - width, xm_prev)
        if qkv_prev is not None:
            gates = gate_logits(gates, qkv_prev)
        xm_prev, qkv_prev = xm, qkv
    lane = lax.broadcasted_iota(jnp.int32, gates.shape, 1)
    log_sig = jnp.minimum(gates, 0.0) - jnp.log1p(jnp.exp(-jnp.abs(gates)))
    gates_ref[...] = jnp.where(lane < ML_HEADS, gates, log_sig)


def _up_conv_qkv_call(x, g, sc, sh, wup, conv0, conv_w, conv_b, wqk, wv, wg, bg, b, t_len):
    n, d = x.shape
    inner = wup.shape[1] // 2
    short_seq = t_len == SUBLANES
    dh = inner // ML_HEADS
    k_scale = dh ** -0.5
    qkv_dtype = F32 if short_seq else BF16
    halo = jnp.zeros((b, SUBLANES, inner), F32)
    if conv0 is not None:
        halo = halo.at[:, :ML_CONV - 1].set(conv0)
    tm = min(256, n) if short_seq else math.gcd(512, t_len)
    seq_per_step = tm // t_len if short_seq else 1
    steps_per_seq = 1 if short_seq else t_len // tm
    sc_op, sc_spec = _mod_operand(sc, t_len, tm)
    sh_op, sh_spec = _mod_operand(sh, t_len, tm)
    seq_spec = pl.BlockSpec((seq_per_step, SUBLANES, inner), lambda i: (i // steps_per_seq, 0, 0))
    row_spec = lambda w: pl.BlockSpec((tm, w), lambda i: (i, 0))
    const2 = lambda s: pl.BlockSpec(s, lambda i: (0, 0), pipeline_mode=pl.Buffered(1))
    const3 = lambda s: pl.BlockSpec(s, lambda i: (0, 0, 0), pipeline_mode=pl.Buffered(1))
    q, k, v, xc, z, gates, tail = pl.pallas_call(
        functools.partial(_up_conv_qkv_kernel, short_seq=short_seq, steps_per_seq=steps_per_seq,
                          k_scale=k_scale),
        grid=(n // tm,),
        in_specs=[
            row_spec(d), const2((1, d)), sc_spec, sh_spec, const2(wup.shape),
            seq_spec, const2((ML_CONV, inner)), const2((1, inner)),
            const3(wqk.shape), const3(wv.shape), const3(wg.shape), const2((1, LANES)),
        ],
        out_specs=[row_spec(inner), row_spec(inner), row_spec(inner), row_spec(inner), row_spec(inner),
                   row_spec(LANES), seq_spec],
        out_shape=[
            jax.ShapeDtypeStruct((n, inner), qkv_dtype),
            jax.ShapeDtypeStruct((n, inner), qkv_dtype),
            jax.ShapeDtypeStruct((n, inner), qkv_dtype),
            jax.ShapeDtypeStruct((n, inner), F32),
            jax.ShapeDtypeStruct((n, inner), F32),
            jax.ShapeDtypeStruct((n, LANES), F32),
            jax.ShapeDtypeStruct((b, SUBLANES, inner), F32),
        ],
        scratch_shapes=[pltpu.VMEM((SUBLANES, inner), F32)],
        compiler_params=_cparams("arbitrary"),
        name="mlstm_up_conv_qkv",
    )(x, g.reshape(1, d), sc_op, sh_op, wup, halo, conv_w, conv_b.reshape(1, inner), wqk, wv, wg, bg)
    return q, k, v, xc, z, gates, tail[:, :ML_CONV - 1]


def _mlstm_kernel(*refs, has_state):
    if has_state:
        (q_ref, k_ref, v_ref, gates_ref, xc_ref, z_ref, nw_ref, sk_ref, c0_ref, n0_ref, m0_ref,
         o_ref, c_ref, n_ref, m_ref) = refs
    else:
        (q_ref, k_ref, v_ref, gates_ref, xc_ref, z_ref, nw_ref, sk_ref,
         o_ref, c_ref, n_ref, m_ref) = refs
    ch = pl.program_id(1)
    length = q_ref.shape[0]
    dh = q_ref.shape[1] // ML_HEADS

    @pl.when(ch == 0)
    def _():
        if has_state:
            c_ref[...] = c0_ref[...]
            n_ref[...] = n0_ref[...]
            m_ref[...] = m0_ref[...]
        else:
            c_ref[...] = jnp.zeros_like(c_ref)
            n_ref[...] = jnp.zeros_like(n_ref)
            m_ref[...] = jnp.zeros_like(m_ref)

    gates = gates_ref[...]
    gates_t = gates.T
    t_idx = lax.broadcasted_iota(jnp.int32, (length, length), 0)
    s_idx = lax.broadcasted_iota(jnp.int32, (length, length), 1)
    causal = s_idx <= t_idx
    m_all = m_ref[0]
    m_new = m_all
    lane = lax.broadcasted_iota(jnp.int32, m_all.shape, 1)

    def finish_head(h, s_raw, q_c, q_n, w, w_c, m_t):
        hs = slice(h * dh, (h + 1) * dh)
        s = s_raw * w
        num = w_c * q_c + _dot(s.astype(BF16), v_ref[:, hs].astype(BF16))
        den = w_c * q_n + jnp.sum(s, axis=-1, keepdims=True)
        hc = num / jnp.maximum(jnp.abs(den), jnp.exp(-m_t))
        mu = jnp.mean(hc, axis=-1, keepdims=True)
        dev = hc - mu
        var = jnp.mean(dev * dev, axis=-1, keepdims=True)
        hn = dev * lax.rsqrt(var + EPS) * nw_ref[:, hs]
        ho = (hn + sk_ref[:, hs] * xc_ref[:, hs]) * _silu(z_ref[:, hs])
        o_ref[:, hs] = ho.astype(o_ref.dtype)

    pending = None
    for h in range(ML_HEADS):
        hs = slice(h * dh, (h + 1) * dh)
        ig_col = gates[:, h:h + 1]
        lf_col = gates[:, ML_HEADS + h:ML_HEADS + h + 1]
        ig_row = gates_t[h:h + 1, :]
        lf_row = gates_t[ML_HEADS + h:ML_HEADS + h + 1, :]
        b_col = jnp.sum(jnp.where(causal, lf_row, 0.0), axis=1, keepdims=True)
        b_row = jnp.sum(jnp.where(t_idx <= s_idx, lf_col, 0.0), axis=0, keepdims=True)
        m_prev = m_all[:, h:h + 1]
        dm = jnp.where(causal, b_col - b_row + ig_row, -jnp.inf)
        a = b_col + m_prev
        m_t = jnp.maximum(a, jnp.max(dm, axis=1, keepdims=True))
        w_c = jnp.exp(a - m_t)
        w = jnp.exp(dm - m_t)
        qh = q_ref[:, hs].astype(BF16)
        kh = k_ref[:, hs]
        vh = v_ref[:, hs].astype(BF16)
        c_h = c_ref[0, h]
        n_h = n_ref[0, :, hs]
        s_raw = _dot_nt(qh, kh.astype(BF16))
        q_c = _dot(qh, c_h.astype(BF16))
        q_n = jnp.sum(qh.astype(F32) * n_h, axis=-1, keepdims=True)
        m_last = m_t[length - 1:length]
        b_last = b_col[length - 1:length]
        wl_c = jnp.exp(a[length - 1:length] - m_last)
        wl_col = jnp.exp(b_last - b_col + ig_col - m_last)
        kw = kh.astype(F32) * wl_col
        c_ref[0, h] = wl_c * c_h + _dot_tn(kw.astype(BF16), vh)
        n_ref[0, :, hs] = wl_c * n_h + jnp.sum(kw, axis=0, keepdims=True)
        m_new = jnp.where(lane == h, m_last, m_new)
        if pending is not None:
            finish_head(*pending)
        pending = (h, s_raw, q_c, q_n, w, w_c, m_t)
    finish_head(*pending)
    m_ref[0] = m_new


def _mlstm_call(q, k, v, gates, xc, z, norm_w, skip, c0, n0, m0, b, t_len):
    n, inner = q.shape
    dh = inner // ML_HEADS
    has_state = c0 is not None
    length = math.gcd(t_len, ML_CHUNK)
    nc = t_len // length
    row = lambda w: pl.BlockSpec((length, w), lambda i, c: (i * nc + c, 0))
    const = lambda w: pl.BlockSpec((1, w), lambda i, c: (0, 0))
    c_spec = pl.BlockSpec((1, ML_HEADS, dh, dh), lambda i, c: (i, 0, 0, 0))
    n_spec = pl.BlockSpec((1, 1, inner), lambda i, c: (i, 0, 0))
    m_spec = pl.BlockSpec((1, 1, LANES), lambda i, c: (i, 0, 0))
    in_specs = [row(inner), row(inner), row(inner), row(LANES), row(inner), row(inner),
                const(inner), const(inner)]
    args = [q, k, v, gates, xc, z, norm_w.reshape(1, inner), skip.reshape(1, inner)]
    if has_state:
        in_specs += [c_spec, n_spec, m_spec]
        m0_pad = jnp.zeros((b, 1, LANES), F32).at[:, 0, :ML_HEADS].set(m0)
        args += [c0, n0.reshape(b, 1, inner), m0_pad]
    ho, c_t, n_t, m_t = pl.pallas_call(
        functools.partial(_mlstm_kernel, has_state=has_state),
        grid=(b, nc),
        in_specs=in_specs,
        out_specs=[row(inner), c_spec, n_spec, m_spec],
        out_shape=[
            jax.ShapeDtypeStruct((n, inner), BF16 if length % 16 == 0 else F32),
            jax.ShapeDtypeStruct((b, ML_HEADS, dh, dh), F32),
            jax.ShapeDtypeStruct((b, 1, inner), F32),
            jax.ShapeDtypeStruct((b, 1, LANES), F32),
        ],
        compiler_params=_cparams("parallel", "arbitrary"),
        name="mlstm_scan",
    )(*args)
    return ho, c_t, n_t.reshape(b, ML_HEADS, dh), m_t[:, 0, :ML_HEADS]


def _block_diag_tiles(w):
    n_blk, blk, _ = w.shape
    per = MXU_DIM // blk
    rows = w.reshape(n_blk // per, per, blk, blk).transpose(0, 1, 3, 2).reshape(n_blk // per, MXU_DIM, blk)
    r_blk = np.arange(MXU_DIM)[:, None] // blk
    c_blk = np.arange(MXU_DIM)[None, :] // blk
    same_block = jnp.asarray((r_blk == c_blk).astype(np.float32))
    return jnp.tile(rows, (1, 1, per)) * same_block


def _prep_weights(p):
    w = {}
    w['hg_win'] = p['hg_win'].astype(BF16)
    w['hg_wo'] = p['hg_wo'].astype(BF16)
    w['ml_wup'] = p['ml_wup'].astype(BF16)
    w['ml_wdown'] = p['ml_wdown'].astype(BF16)
    dep, n_exp, d_model, dff = p['moe_w1'].shape
    n_stack = dep * n_exp // MOE_EPG
    w['moe_w1'] = p['moe_w1'].reshape(n_stack, MOE_EPG, d_model, dff)
    w['moe_w3'] = p['moe_w3'].reshape(n_stack, MOE_EPG, d_model, dff)
    w['moe_w2'] = p['moe_w2'].reshape(n_stack, MOE_EPG, dff, d_model)
    depth, d, g = p['moe_wrg'].shape
    n_exp = g * p['moe_wre'].shape[-1]
    wr = jnp.zeros((depth, d, LANES), F32)
    wr = wr.at[:, :, :g].set(p['moe_wrg']).at[:, :, g:g + n_exp].set(p['moe_wre'].reshape(depth, d, n_exp))
    w['moe_wr'] = wr.astype(BF16)
    br = jnp.zeros((depth, 1, LANES), F32)
    br = br.at[:, 0, :g].set(p['moe_brg']).at[:, 0, g:g + n_exp].set(p['moe_bre'].reshape(depth, n_exp))
    w['moe_br'] = br
    n_b = p['ml_wq'].shape[0]
    wq = jnp.stack([_block_diag_tiles(p['ml_wq'][j]) for j in range(n_b)])
    wk = jnp.stack([_block_diag_tiles(p['ml_wk'][j]) for j in range(n_b)])
    wv = jnp.stack([_block_diag_tiles(p['ml_wv'][j]) for j in range(n_b)])
    w['ml_wqk'] = jnp.concatenate([wq, wk], axis=-1).astype(BF16)
    w['ml_wv'] = wv.astype(BF16)
    inner = p['ml_conv_b'].shape[-1]
    wg = jnp.zeros((n_b, 3 * inner, LANES), F32)
    wg = wg.at[:, :, :ML_HEADS].set(p['ml_wig']).at[:, :, ML_HEADS:2 * ML_HEADS].set(p['ml_wfg'])
    w['ml_wg'] = wg.reshape(n_b, 3, inner, LANES).astype(BF16)
    bg = jnp.zeros((n_b, 1, LANES), F32)
    bg = bg.at[:, 0, :ML_HEADS].set(p['ml_big']).at[:, 0, ML_HEADS:2 * ML_HEADS].set(p['ml_bfg'])
    w['ml_bg'] = bg
    return w


def _lb_kernel(lb_ref, o_ref):
    x = lb_ref[...]
    mx = jnp.max(x, axis=0, keepdims=True)
    ex = jnp.exp(x - mx)
    sm = ex / jnp.sum(ex, axis=0, keepdims=True)
    rows = []
    run = jnp.zeros_like(sm[0:1])
    for i in range(x.shape[0]):
        run = run + sm[i:i + 1]
        rows.append(run)
    o_ref[...] = jnp.concatenate(rows, axis=0)


def _lb_call(hg_lb):
    return pl.pallas_call(
        _lb_kernel,
        out_shape=jax.ShapeDtypeStruct(hg_lb.shape, F32),
        name="hgrn_lower_bound",
    )(hg_lb)


def _trunk(x3, mods, s_hg, s_c, s_n, s_m, s_conv, p, w, lb_all):
    b, t_len, d = x3.shape
    n = b * t_len
    x = x3.reshape(n, d)
    depth = p['norm_g'].shape[0]
    new_hg, new_c, new_n, new_m, new_conv = [], [], [], [], []
    for l in range(depth):
        sh1, sc1, g1, sh2, sc2, g2 = [mods[l][:, i * d:(i + 1) * d] for i in range(6)]
        if l % 2 == 0:
            a = l // 2
            proj = _norm_mm_call(x, p['norm_g'][l, 0], sc1, sh1, w['hg_win'][a], t_len, "hgrn_in_proj")
            o, s_t = _gla_call(proj, lb_all[l], p['hg_norm'][a], None if s_hg is None else s_hg[a], b, t_len)
            new_hg.append(s_t)
            mix = (o, w['hg_wo'][a], g1)
        else:
            j = l // 2
            q, k, v, xc, z, gates, conv_tail = _up_conv_qkv_call(
                x, p['norm_g'][l, 0], sc1, sh1, w['ml_wup'][j],
                None if s_conv is None else s_conv[j], p['ml_conv_w'][j], p['ml_conv_b'][j],
                w['ml_wqk'][j], w['ml_wv'][j], w['ml_wg'][j], w['ml_bg'][j], b, t_len)
            ho, c_t, n_t, m_t = _mlstm_call(
                q, k, v, gates, xc, z, p['ml_norm'][j], p['ml_skip'][j],
                None if s_c is None else s_c[j], None if s_n is None else s_n[j],
                None if s_m is None else s_m[j], b, t_len)
            new_c.append(c_t)
            new_n.append(n_t)
            new_m.append(m_t)
            new_conv.append(conv_tail)
            mix = (ho.astype(BF16), w['ml_wdown'][j], g1)
        x = _moe_call(mix, x, p['norm_g'][l, 1], sc2, sh2, g2, w['moe_wr'][l], w['moe_br'][l],
                      w['moe_w1'], w['moe_w3'], w['moe_w2'], l,
                      p['final_g'] if l == depth - 1 else None, t_len, "moe_layer%d" % l)
    return (x.reshape(b, t_len, d), jnp.stack(new_hg), jnp.stack(new_c), jnp.stack(new_n),
            jnp.stack(new_m), jnp.stack(new_conv))


def kernel(x_prompt, x_sample, c_prompt, c_sample, state_hgrn, state_mlstm_c, state_mlstm_n, state_mlstm_m, state_conv, w_ada, b_ada, norm_g, final_g, hg_win, hg_wo, hg_norm, hg_lb, ml_wup, ml_conv_w, ml_conv_b, ml_wq, ml_wk, ml_wv, ml_wig, ml_big, ml_wfg, ml_bfg, ml_norm, ml_skip, ml_wdown, moe_wrg, moe_brg, moe_wre, moe_bre, moe_w1, moe_w3, moe_w2):
    p = dict(w_ada=w_ada, b_ada=b_ada, norm_g=norm_g, final_g=final_g,
             hg_win=hg_win, hg_wo=hg_wo, hg_norm=hg_norm, hg_lb=hg_lb,
             ml_wup=ml_wup, ml_conv_w=ml_conv_w, ml_conv_b=ml_conv_b, ml_wq=ml_wq, ml_wk=ml_wk, ml_wv=ml_wv,
             ml_wig=ml_wig, ml_big=ml_big, ml_wfg=ml_wfg, ml_bfg=ml_bfg, ml_norm=ml_norm, ml_skip=ml_skip,
             ml_wdown=ml_wdown, moe_wrg=moe_wrg, moe_brg=moe_brg, moe_wre=moe_wre, moe_bre=moe_bre,
             moe_w1=moe_w1, moe_w3=moe_w3, moe_w2=moe_w2)
    w = _prep_weights(p)
    lb_all = _lb_call(hg_lb)
    bp = x_prompt.shape[0]
    c_all = jnp.concatenate([c_prompt, c_sample], axis=0)
    mod_all = _ada_call(c_all, w_ada, b_ada)
    mods_p = [mod_all[l, :bp] for l in range(mod_all.shape[0])]
    mods_s = [mod_all[l, bp:] for l in range(mod_all.shape[0])]
    y_p, hg_p, mc_p, mn_p, mm_p, conv_p = _trunk(x_prompt, mods_p, None, None, None, None, None, p, w, lb_all)
    y_s, hg_s, mc_s, mn_s, mm_s, conv_s = _trunk(x_sample, mods_s, state_hgrn, state_mlstm_c, state_mlstm_n,
                                                 state_mlstm_m, state_conv, p, w, lb_all)
    return (y_p, y_s, hg_p, mc_p, mn_p, mm_p, conv_p, hg_s, mc_s, mn_s, mm_s, conv_s)
```

```python
import functools
import math

import numpy as np
import jax
import jax.numpy as jnp
from jax import lax
from jax.experimental import pallas as pl
from jax.experimental.pallas import tpu as pltpu

F32 = jnp.float32
BF16 = jnp.bfloat16
EPS = 1e-6

HG_DK = 128
ML_HEADS = 4
ML_CONV = 4
ML_QKV_BLOCK = 4
MOE_GROUPS = 4
MOE_EPG = 4
CHUNK = 64
GLA_CHUNKS_PER_STEP = 8
ML_CHUNK = 256

LANES = 128
SUBLANES = 8
MXU_DIM = 256
VMEM_LIMIT_BYTES = 56 * 1024 * 1024


def _cparams(*sem):
    return pltpu.CompilerParams(dimension_semantics=sem, vmem_limit_bytes=VMEM_LIMIT_BYTES)


def _silu(x):
    return x * jax.nn.sigmoid(x)


def _dot(a, b):
    return jnp.dot(a, b, preferred_element_type=F32)


def _dot_nt(a, b):
    return lax.dot_general(a, b, (((1,), (1,)), ((), ())), preferred_element_type=F32)


def _dot_tn(a, b):
    return lax.dot_general(a, b, (((0,), (0,)), ((), ())), preferred_element_type=F32)


def _rms_mod(x, g, sc, sh):
    ms = jnp.mean(x * x, axis=-1, keepdims=True)
    h = x * lax.rsqrt(ms + EPS) * g
    return h * (1.0 + sc) + sh


def _ada_kernel(c_ref, w_ref, b_ref, o_ref):
    cm = _silu(c_ref[...]).astype(BF16)
    o_ref[0] = _dot(cm, w_ref[0].astype(BF16)) + b_ref[0]


def _ada_call(c_all, w_ada, b_ada):
    depth, d, n_out = w_ada.shape
    m = c_all.shape[0]
    tn = 512
    return pl.pallas_call(
        _ada_kernel,
        grid=(depth, n_out // tn),
        in_specs=[
            pl.BlockSpec((m, d), lambda l, j: (0, 0)),
            pl.BlockSpec((1, d, tn), lambda l, j: (l, 0, j)),
            pl.BlockSpec((1, 1, tn), lambda l, j: (l, 0, j)),
        ],
        out_specs=pl.BlockSpec((1, m, tn), lambda l, j: (l, 0, j)),
        out_shape=jax.ShapeDtypeStruct((depth, m, n_out), F32),
        compiler_params=_cparams("parallel", "parallel"),
        name="ada_mod",
    )(c_all, w_ada, b_ada.reshape(depth, 1, n_out))


def _mod_operand(m, t_len, tm):
    b, d = m.shape
    if t_len % tm == 0:
        per_b = t_len // tm
        return m.reshape(b, 1, d), pl.BlockSpec((1, 1, d), lambda i, *_: (i // per_b, 0, 0))
    assert tm % t_len == 0
    nb = tm // t_len
    return m.reshape(b // nb, nb, 1, d), pl.BlockSpec((1, nb, 1, d), lambda i, *_: (i, 0, 0, 0))


def _mod_rows(ref, rows):
    v = ref[0]
    if v.ndim == 2:
        return v
    nb, _, d = v.shape
    return jnp.broadcast_to(v, (nb, rows // nb, d)).reshape(rows, d)


def _row_tile(n, t_len, target):
    tm = min(target, n)
    while n % tm or (t_len % tm and tm % t_len):
        tm //= 2
    return tm


def _norm_mm_kernel(x_ref, g_ref, sc_ref, sh_ref, w_ref, o_ref, *, col_chunk):
    rows = x_ref.shape[0]
    hb = _rms_mod(x_ref[...], g_ref[...], _mod_rows(sc_ref, rows), _mod_rows(sh_ref, rows)).astype(BF16)
    for c0 in range(0, o_ref.shape[1], col_chunk):
        o_ref[:, c0:c0 + col_chunk] = _dot(hb, w_ref[:, c0:c0 + col_chunk])


def _norm_mm_call(x, g, sc, sh, w, t_len, name):
    n, d = x.shape
    n_out = w.shape[1]
    tm = _row_tile(n, t_len, 1024 if t_len % 1024 == 0 else 256)
    sc_op, sc_spec = _mod_operand(sc, t_len, tm)
    sh_op, sh_spec = _mod_operand(sh, t_len, tm)
    return pl.pallas_call(
        functools.partial(_norm_mm_kernel, col_chunk=512),
        grid=(n // tm,),
        in_specs=[
            pl.BlockSpec((tm, d), lambda i: (i, 0)),
            pl.BlockSpec((1, d), lambda i: (0, 0)),
            sc_spec, sh_spec,
            pl.BlockSpec((d, n_out), lambda i: (0, 0), pipeline_mode=pl.Buffered(1)),
        ],
        out_specs=pl.BlockSpec((tm, n_out), lambda i: (i, 0)),
        out_shape=jax.ShapeDtypeStruct((n, n_out), F32),
        compiler_params=_cparams("parallel"),
        name=name,
    )(x, g.reshape(1, d), sc_op, sh_op, w)


def _gla_tables(t_sub, nseq):
    r = t_sub * nseq
    levels = []
    m = t_sub // 2
    while m >= 1:
        levels.append(m)
        m //= 2
    n_lev = len(levels)
    tril = np.zeros((r, r), np.float32)
    mask = np.zeros((n_lev + 1, r, r), np.float32)
    for li, m in enumerate(levels):
        for row in range(r):
            blk = (row // (2 * m)) * 2 * m
            if row - blk >= m:
                mask[li, row, blk:blk + m] = 1.0
    for row in range(r):
        s0 = (row // t_sub) * t_sub
        tril[row, s0:row + 1] = 1.0
        mask[n_lev, row, row] = 1.0
    return tril, mask, levels


def _bcast_block_row(b, block, row_in_block):
    parts = [jnp.broadcast_to(b[s + row_in_block:s + row_in_block + 1, :], (block, b.shape[1]))
             for s in range(0, b.shape[0], block)]
    return parts[0] if len(parts) == 1 else jnp.concatenate(parts, axis=0)


def _level_decay(b, m):
    r = b.shape[0]
    pos = lax.broadcasted_iota(jnp.int32, b.shape, 0) & (2 * m - 1)
    if 2 * m >= SUBLANES:
        b_mid = _bcast_block_row(b, 2 * m, m - 1)
    else:
        b_mid = b
        for p in range(2 * m):
            if p != m - 1:
                b_mid = jnp.where(pos == p, pltpu.roll(b, (p - (m - 1)) % r, 0), b_mid)
    return jnp.exp2(jnp.where(pos >= m, b - b_mid, b_mid - b))


def _gla_kernel(*refs, t_sub, nseq, n_ch, levels, n_heads, has_state, steps_per_seq):
    if has_state:
        proj_ref, lb_ref, gn_ref, tril_ref, mask_ref, s0_ref, o_ref, sout_ref, st_ref = refs
    else:
        proj_ref, lb_ref, gn_ref, tril_ref, mask_ref, o_ref, sout_ref, st_ref = refs
    r = t_sub * nseq
    n_lev = len(levels)
    dk = HG_DK
    hk = n_heads * dk
    c = lax.rem(pl.program_id(0), steps_per_seq)

    @pl.when(c == 0)
    def _():
        if has_state:
            for j in range(nseq):
                for h in range(n_heads):
                    st_ref[j, h] = s0_ref[j, h].T
        else:
            st_ref[...] = jnp.zeros_like(st_ref)

    gn = gn_ref[...]
    tril = tril_ref[...]

    def finish_head(rows, h, a, qh, kh, e_cum_h, e_end_h, d_last):
        hs = slice(h * dk, (h + 1) * dk)
        vb = proj_ref[rows, 2 * hk + h * dk:2 * hk + (h + 1) * dk].astype(BF16)
        o_intra = _dot(a.astype(BF16), vb)
        qd = qh * e_cum_h
        kd = kh * e_end_h
        o_parts = []
        for j in range(nseq):
            rs = slice(j * t_sub, (j + 1) * t_sub)
            st = st_ref[j, h]
            o_parts.append(_dot_nt(qd[rs], st.astype(BF16)))
            st_ref[j, h] = st * d_last[j] + _dot_tn(vb[rs], kd[rs])
        o_inter = o_parts[0] if nseq == 1 else jnp.concatenate(o_parts, axis=0)
        o = o_intra + o_inter
        o = o * lax.rsqrt(jnp.mean(o * o, axis=-1, keepdims=True) + EPS) * gn
        zg = proj_ref[rows, 3 * hk + h * dk:3 * hk + (h + 1) * dk]
        o_ref[rows, hs] = (o * _silu(zg)).astype(BF16)

    pending = None
    head_group = n_heads if nseq == 1 else 1
    for ci in range(n_ch):
        rows = slice(ci * r, (ci + 1) * r)
        for g0 in range(0, n_heads, head_group):
            gw = head_group * dk
            gs = slice(g0 * dk, g0 * dk + gw)
            lb = lb_ref[:, gs]
            zq = proj_ref[rows, g0 * dk:g0 * dk + gw]
            zf = proj_ref[rows, hk + g0 * dk:hk + g0 * dk + gw]
            f = lb + (1.0 - lb) * jax.nn.sigmoid(zf)
            lf = jnp.log2(f)
            q = _silu(zq).astype(BF16)
            k = (1.0 - f).astype(BF16)

            p0 = lf.astype(BF16)
            r1 = lf - p0.astype(F32)
            p1 = r1.astype(BF16)
            p2 = (r1 - p1.astype(F32)).astype(BF16)
            b = _dot(tril, p0) + _dot(tril, p1) + _dot(tril, p2)
            e_cum = jnp.exp2(b)
            e_cum_b = e_cum.astype(BF16)
            e_end_b = jnp.exp2(_bcast_block_row(b, t_sub, t_sub - 1) - b).astype(BF16)
            zs = [_level_decay(b, m).astype(BF16) for m in levels]

            for hh in range(head_group):
                h = g0 + hh
                ls = slice(hh * dk, (hh + 1) * dk)
                qh = q[:, ls]
                kh = k[:, ls]
                a = _dot_nt(qh, kh) * mask_ref[n_lev]
                for li in range(n_lev):
                    z = zs[li][:, ls]
                    a = a + _dot_nt(qh * z, kh * z) * mask_ref[li]
                item = (rows, h, a, qh, kh, e_cum_b[:, ls], e_end_b[:, ls],
                        [e_cum[(j + 1) * t_sub - 1:(j + 1) * t_sub, ls] for j in range(nseq)])
                if nseq > 1:
                    finish_head(*item)
                else:
                    if pending is not None:
                        finish_head(*pending)
                    pending = item
    if pending is not None:
        finish_head(*pending)

    @pl.when(c == steps_per_seq - 1)
    def _():
        for j in range(nseq):
            for h in range(n_heads):
                sout_ref[j, h] = st_ref[j, h].T


def _gla_call(proj, lb, gn, s0, b, t_len):
    n = proj.shape[0]
    hk = lb.shape[-1]
    n_heads = hk // HG_DK
    has_state = s0 is not None
    if t_len % CHUNK == 0:
        t_sub, nseq = CHUNK, 1
    else:
        t_sub, nseq = t_len, CHUNK // t_len
        assert t_sub * nseq == CHUNK and b % nseq == 0
    r = t_sub * nseq
    tril, mask, levels = _gla_tables(t_sub, nseq)
    n_ch = math.gcd(GLA_CHUNKS_PER_STEP, t_len // t_sub)
    rows = n_ch * r
    steps_per_seq = t_len // (t_sub * n_ch)
    n_steps = n // rows
    const2 = lambda s: pl.BlockSpec(s, lambda i: (0, 0))
    in_specs = [pl.BlockSpec((rows, proj.shape[1]), lambda i: (i, 0)),
                const2((1, hk)), const2((1, HG_DK)), const2(tril.shape),
                pl.BlockSpec(mask.shape, lambda i: (0, 0, 0))]
    args = [proj, lb.reshape(1, hk), gn.reshape(1, HG_DK), jnp.asarray(tril, BF16), jnp.asarray(mask, F32)]
    s_spec = pl.BlockSpec((nseq, n_heads, HG_DK, HG_DK), lambda i: (i // steps_per_seq, 0, 0, 0))
    if has_state:
        in_specs.append(s_spec)
        args.append(s0)
    o, s_out = pl.pallas_call(
        functools.partial(_gla_kernel, t_sub=t_sub, nseq=nseq, n_ch=n_ch, levels=tuple(levels),
                          n_heads=n_heads, has_state=has_state, steps_per_seq=steps_per_seq),
        grid=(n_steps,),
        in_specs=in_specs,
        out_specs=[
            pl.BlockSpec((rows, hk), lambda i: (i, 0)),
            s_spec,
        ],
        out_shape=[
            jax.ShapeDtypeStruct((n, hk), BF16),
            jax.ShapeDtypeStruct((b, n_heads, HG_DK, HG_DK), F32),
        ],
        scratch_shapes=[pltpu.VMEM((nseq, n_heads, HG_DK, HG_DK), F32)],
        compiler_params=_cparams("arbitrary"),
        name="gla_scan",
    )(*args)
    return o, s_out


def _route_kernel(a_ref, wmix_ref, gmix_ref, x_ref, g_ref, sc_ref, sh_ref, wr_ref, br_ref, tril_ref,
                  xnew_ref, hx_ref, meta_ref, cnt_ref, *, n_groups, epg):
    rows, d = x_ref.shape
    neg = -jnp.inf
    far = float(LANES)

    @pl.when(pl.program_id(0) == 0)
    def _():
        cnt_ref[...] = jnp.zeros_like(cnt_ref)

    x = x_ref[...] + _mod_rows(gmix_ref, rows) * _dot(a_ref[...], wmix_ref[...])
    xnew_ref[...] = x
    h = _rms_mod(x, g_ref[...], _mod_rows(sc_ref, rows), _mod_rows(sh_ref, rows))
    wr = wr_ref[...]
    h_hi = h.astype(BF16)
    h_lo = (h - h_hi.astype(F32)).astype(BF16)
    w_hi = wr.astype(BF16)
    w_lo = (wr - w_hi.astype(F32)).astype(BF16)
    logit = _dot(h_hi, w_hi) + _dot(h_hi, w_lo) + _dot(h_lo, w_hi) + br_ref[...]
    lane = lax.broadcasted_iota(jnp.int32, logit.shape, 1).astype(F32)
    gm = lane < n_groups
    gmax = jnp.max(jnp.where(gm, logit, neg), axis=-1, keepdims=True)
    gstar = jnp.min(jnp.where(gm, jnp.where(logit == gmax, lane, far), far), axis=-1, keepdims=True)
    psum = jnp.sum(jnp.where(gm, jnp.exp(logit - gmax), 0.0), axis=-1, keepdims=True)
    pstar = 1.0 / psum
    lo = n_groups + gstar * epg
    em = jnp.where(lane >= lo, jnp.where(lane < lo + epg, 1.0, 0.0), 0.0)
    l1 = jnp.where(em > 0.0, logit, neg)
    v1 = jnp.max(l1, axis=-1, keepdims=True)
    i1 = jnp.min(jnp.where(l1 == v1, lane, far), axis=-1, keepdims=True)
    l2 = jnp.where(lane == i1, neg, l1)
    v2 = jnp.max(l2, axis=-1, keepdims=True)
    i2 = jnp.min(jnp.where(l2 == v2, lane, far), axis=-1, keepdims=True)
    e2 = jnp.exp(v2 - v1)
    wt1 = pstar / (1.0 + e2)
    wt2 = pstar * e2 / (1.0 + e2)
    onehot = jnp.where(lane == gstar, 1.0, 0.0)
    within = _dot(tril_ref[...], onehot.astype(BF16))
    carry = cnt_ref[...]
    rank = jnp.sum(onehot * (within + carry), axis=-1, keepdims=True)
    cnt_ref[...] = carry + jnp.sum(onehot, axis=0, keepdims=True)
    meta = (jnp.where(lane == i1 - lo, wt1, 0.0) + jnp.where(lane == i2 - lo, wt2, 0.0)
            + jnp.where(lane == epg, gstar, 0.0) + jnp.where(lane == epg + 1, rank, 0.0))
    hx_ref[:, :d] = h
    hx_ref[:, d:] = meta
    meta_ref[...] = meta.T[0:SUBLANES]


def _route_call(a, w_mix, gate_mix, x, g, sc, sh, wr, br, t_len, name):
    n, d = x.shape
    kdim = a.shape[1]
    tm = _row_tile(n, t_len, 1024 if t_len % 1024 == 0 else 256)
    gm_op, gm_spec = _mod_operand(gate_mix, t_len, tm)
    sc_op, sc_spec = _mod_operand(sc, t_len, tm)
    sh_op, sh_spec = _mod_operand(sh, t_len, tm)
    tril = jnp.asarray(np.tril(np.ones((tm, tm), np.float32), -1), BF16)
    const = lambda s: pl.BlockSpec(s, lambda i: (0, 0), pipeline_mode=pl.Buffered(1))
    return pl.pallas_call(
        functools.partial(_route_kernel, n_groups=MOE_GROUPS, epg=MOE_EPG),
        grid=(n // tm,),
        in_specs=[
            pl.BlockSpec((tm, kdim), lambda i: (i, 0)),
            const((kdim, d)), gm_spec,
            pl.BlockSpec((tm, d), lambda i: (i, 0)),
            const((1, d)), sc_spec, sh_spec,
            const((d, LANES)), const((1, LANES)), const((tm, tm)),
        ],
        out_specs=[
            pl.BlockSpec((tm, d), lambda i: (i, 0)),
            pl.BlockSpec((tm, d + LANES), lambda i: (i, 0)),
            pl.BlockSpec((SUBLANES, tm), lambda i: (0, i)),
            pl.BlockSpec((1, LANES), lambda i: (0, 0)),
        ],
        out_shape=[
            jax.ShapeDtypeStruct((n, d), F32),
            jax.ShapeDtypeStruct((n, d + LANES), F32),
            jax.ShapeDtypeStruct((SUBLANES, n), F32),
            jax.ShapeDtypeStruct((1, LANES), F32),
        ],
        compiler_params=_cparams("arbitrary"),
        name=name + "_route",
    )(a, w_mix, gm_op, x, g.reshape(1, d), sc_op, sh_op, wr, br, tril)


def _row_gather(idx_ref, src_hbm, buf, sem, tile, slot, start, unrolled=False):
    groups = buf.shape[1]
    base = tile * (groups * SUBLANES)

    def run(s):
        def body(i, carry):
            for u in range(SUBLANES):
                idx = idx_ref[base + i * SUBLANES + u]
                cp = pltpu.make_async_copy(src_hbm.at[pl.ds(idx, 1)],
                                           buf.at[s, i, pl.ds(u, 1)], sem.at[s])
                if start:
                    cp.start(priority=u % 2)
                else:
                    cp.wait()
            return carry

        if unrolled:
            for i in range(groups):
                body(i, 0)
        else:
            lax.fori_loop(0, groups, body, 0)

    for s in range(2):
        @pl.when(slot == s)
        def _():
            run(s)


def _expert_kernel(pad_ref, dest_ref, tgrp_ref, tval_ref, hx_hbm, w1_ref, w3_ref, w2_ref, y_ref,
                   hbuf, sem, wb1, wb3, wb2, src_ref, *, epg, d):
    j = pl.program_id(0)
    rows = y_ref.shape[0]

    @pl.when(j == 0)
    def _():
        _invperm(pad_ref, dest_ref, src_ref)

    @pl.when(jnp.logical_or(j == 0, tgrp_ref[j] != tgrp_ref[jnp.maximum(j - 1, 0)]))
    def _():
        for e in range(epg):
            wb1[e] = w1_ref[0, e].astype(BF16)
            wb3[e] = w3_ref[0, e].astype(BF16)
            wb2[e] = w2_ref[0, e].astype(BF16)

    @pl.when(j == 0)
    def _():
        @pl.when(tval_ref[0] == 1)
        def _():
            _row_gather(src_ref, hx_hbm, hbuf, sem, 0, 0, True)

    @pl.when(j + 1 < pl.num_programs(0))
    def _():
        @pl.when(tval_ref[j + 1] == 1)
        def _():
            _row_gather(src_ref, hx_hbm, hbuf, sem, j + 1, (j + 1) % 2, True, unrolled=True)

    @pl.when(tval_ref[j] == 1)
    def _():
        slot = j % 2
        _row_gather(src_ref, hx_hbm, hbuf, sem, j, slot, False)
        tile = hbuf[slot].reshape(rows, hbuf.shape[-1])
        hb = tile[:, :d].astype(BF16)
        hids = []
        for e in range(epg):
            a = _dot(hb, wb1[e])
            b = _dot(hb, wb3[e])
            hids.append((_silu(a) * b * tile[:, d + e:d + e + 1]).astype(BF16))
        hid = jnp.concatenate(hids, axis=1)
        y_ref[...] = _dot(hid, wb2[...].reshape(hid.shape[1], d))

    @pl.when(tval_ref[j] == 0)
    def _():
        y_ref[...] = jnp.zeros_like(y_ref)


def _combine_kernel(*refs, final):
    if final:
        dest_ref, ys_hbm, x_ref, gate_ref, fg_ref, o_ref, gbuf, sem = refs
    else:
        dest_ref, ys_hbm, x_ref, gate_ref, o_ref, gbuf, sem = refs
    i = pl.program_id(0)
    rows = x_ref.shape[0]

    @pl.when(i == 0)
    def _():
        _row_gather(dest_ref, ys_hbm, gbuf, sem, 0, 0, True)

    @pl.when(i + 1 < pl.num_programs(0))
    def _():
        _row_gather(dest_ref, ys_hbm, gbuf, sem, i + 1, (i + 1) % 2, True, unrolled=True)

    slot = i % 2
    _row_gather(dest_ref, ys_hbm, gbuf, sem, i, slot, False)
    y = x_ref[...] + _mod_rows(gate_ref, rows) * gbuf[slot].reshape(rows, gbuf.shape[-1])
    if final:
        y = y * lax.rsqrt(jnp.mean(y * y, axis=-1, keepdims=True) + EPS) * fg_ref[...]
    o_ref[...] = y


def _invperm(pad_ref, dest_ref, src_ref):
    def clear(i, carry):
        src_ref[i] = 0
        return carry

    def place(i, carry):
        src_ref[dest_ref[i]] = i
        return carry

    for rng in range(pad_ref.shape[0] // 2):
        lax.fori_loop(pad_ref[2 * rng], pad_ref[2 * rng + 1], clear, 0)
    lax.fori_loop(0, dest_ref.shape[0], place, 0, unroll=8)


def _moe_call(mix, x, g, sc, sh, gate, wr, br, w1, w3, w2, layer, final_g, t_len, name):
    n, d = x.shape
    epg, dff = w1.shape[1], w1.shape[3]
    n_grp = MOE_GROUPS
    x, hx, meta, cnt = _route_call(*mix, x, g, sc, sh, wr, br, t_len, name)

    tm2 = 512 if n >= 8192 else 128
    grp = meta[epg].astype(jnp.int32)
    rank = meta[epg + 1].astype(jnp.int32)
    counts = cnt[0, :n_grp].astype(jnp.int32)
    n_tiles_g = (counts + tm2 - 1) // tm2
    tile_end = jnp.cumsum(n_tiles_g)
    tile_start = tile_end - n_tiles_g
    dest = rank
    for gi in range(n_grp):
        dest = dest + jnp.where(grp == gi, tile_start[gi] * tm2, 0)
    n_tiles = n // tm2 + n_grp
    pad_lo = jnp.concatenate([tile_start * tm2 + counts, tile_end[-1:] * tm2])
    pad_hi = jnp.concatenate([tile_end * tm2, jnp.full((1,), n_tiles * tm2, jnp.int32)])
    pad = jnp.stack([pad_lo, pad_hi], axis=1).reshape(-1)
    jt = jnp.arange(n_tiles, dtype=jnp.int32)
    tval = (jt < tile_end[-1]).astype(jnp.int32)
    tgrp = jnp.sum((jt[:, None] >= tile_end[None, :]).astype(jnp.int32), axis=1)
    last_grp = jnp.sum((tile_end[-1] - 1 >= tile_end).astype(jnp.int32))
    tgrp = layer * n_grp + jnp.minimum(jnp.where(tval == 1, tgrp, last_grp), n_grp - 1)

    wspec = lambda shape: pl.BlockSpec(shape, lambda j, pad, dest, tgrp, tval: (tgrp[j], 0, 0, 0))
    ys = pl.pallas_call(
        functools.partial(_expert_kernel, epg=epg, d=d),
        grid_spec=pltpu.PrefetchScalarGridSpec(
            num_scalar_prefetch=4,
            grid=(n_tiles,),
            in_specs=[
                pl.BlockSpec(memory_space=pl.ANY),
                wspec((1, epg, d, dff)), wspec((1, epg, d, dff)), wspec((1, epg, dff, d)),
            ],
            out_specs=pl.BlockSpec((tm2, d), lambda j, pad, dest, tgrp, tval: (j, 0)),
            scratch_shapes=[pltpu.VMEM((2, tm2 // SUBLANES, SUBLANES, d + LANES), F32),
                            pltpu.SemaphoreType.DMA((2,)),
                            pltpu.VMEM((epg, d, dff), BF16), pltpu.VMEM((epg, d, dff), BF16),
                            pltpu.VMEM((epg, dff, d), BF16),
                            pltpu.SMEM((n_tiles * tm2,), jnp.int32)],
        ),
        out_shape=jax.ShapeDtypeStruct((n_tiles * tm2, d), F32),
        compiler_params=_cparams("arbitrary"),
        name=name + "_experts",
    )(pad, dest, tgrp, tval, hx, w1, w3, w2)

    tm = _row_tile(n, t_len, 512 if t_len % 512 == 0 else 256)
    gate_op, gate_spec = _mod_operand(gate, t_len, tm)
    final = final_g is not None
    in_specs = [
        pl.BlockSpec(memory_space=pl.ANY),
        pl.BlockSpec((tm, d), lambda i, dest: (i, 0)),
        gate_spec,
    ]
    args = [dest, ys, x, gate_op]
    if final:
        in_specs.append(pl.BlockSpec((1, d), lambda i, dest: (0, 0)))
        args.append(final_g.reshape(1, d))
    return pl.pallas_call(
        functools.partial(_combine_kernel, final=final),
        grid_spec=pltpu.PrefetchScalarGridSpec(
            num_scalar_prefetch=1,
            grid=(n // tm,),
            in_specs=in_specs,
            out_specs=pl.BlockSpec((tm, d), lambda i, dest: (i, 0)),
            scratch_shapes=[pltpu.VMEM((2, tm // SUBLANES, SUBLANES, d), F32),
                            pltpu.SemaphoreType.DMA((2,))],
        ),
        out_shape=jax.ShapeDtypeStruct((n, d), F32),
        compiler_params=_cparams("arbitrary"),
        name=name + "_combine",
    )(*args)


def _conv_taps(xm, halo, conv_w, conv_b, row_in_seq, axis):
    acc = conv_b + xm * conv_w[ML_CONV - 1]
    for s in range(1, ML_CONV):
        shifted = pltpu.roll(xm, s, axis)
        fill = pltpu.roll(halo, (s + SUBLANES - (ML_CONV - 1)) % SUBLANES, axis)
        if axis == 0:
            top = jnp.where(row_in_seq < s, fill, shifted[0:SUBLANES])
            shifted = jnp.concatenate([top, shifted[SUBLANES:]], axis=0)
        else:
            shifted = jnp.where(row_in_seq < s, fill, shifted)
        acc = acc + shifted * conv_w[ML_CONV - 1 - s]
    return acc


def _up_conv_qkv_kernel(x_ref, g_ref, sc_ref, sh_ref, wup_ref, c0_ref, cw_ref, cb_ref,
                        wqk_ref, wv_ref, wg_ref, bg_ref,
                        q_ref, k_ref, v_ref, xc_ref, z_ref, gates_ref, tail_ref, carry_scr,
                        *, short_seq, steps_per_seq, k_scale):
    rows = x_ref.shape[0]
    inner = z_ref.shape[1]
    hb = _rms_mod(x_ref[...], g_ref[...], _mod_rows(sc_ref, rows), _mod_rows(sh_ref, rows)).astype(BF16)
    if not short_seq:
        @pl.when(pl.program_id(0) % steps_per_seq == 0)
        def _():
            carry_scr[...] = c0_ref[0]

    width = 2 * MXU_DIM

    def conv_qkv(c0, xm):
        cs = slice(c0, c0 + width)
        conv_w = [cw_ref[i:i + 1, cs] for i in range(ML_CONV)]
        conv_b = cb_ref[:, cs]
        if short_seq:
            xm3 = xm.reshape(rows // SUBLANES, SUBLANES, width)
            t_idx = lax.broadcasted_iota(jnp.int32, xm3.shape, 1)
            conv = _conv_taps(xm3, c0_ref[:, :, cs], conv_w, conv_b, t_idx, 1).reshape(rows, width)
            tail_ref[:, :, cs] = pltpu.roll(xm3, ML_CONV - 1, 1)
        else:
            row8 = lax.broadcasted_iota(jnp.int32, (SUBLANES, width), 0)
            conv = _conv_taps(xm, carry_scr[:, cs], conv_w, conv_b, row8, 0)
            tail = pltpu.roll(xm[rows - SUBLANES:], ML_CONV - 1, 0)
            carry_scr[:, cs] = tail
            tail_ref[0, :, cs] = tail
        xc = _silu(conv)
        xc_ref[:, cs] = xc
        xcb = xc.astype(BF16)
        xmb = xm.astype(BF16)
        qkv = []
        for i in range(width // MXU_DIM):
            ls = slice(i * MXU_DIM, (i + 1) * MXU_DIM)
            ti = c0 // MXU_DIM + i
            os_ = slice(ti * MXU_DIM, (ti + 1) * MXU_DIM)
            qk = _dot(xcb[:, ls], wqk_ref[ti])
            qi = qk[:, :MXU_DIM]
            ki = qk[:, MXU_DIM:]
            vi = _dot(xmb[:, ls], wv_ref[ti])
            q_ref[:, os_] = qi.astype(q_ref.dtype)
            k_ref[:, os_] = (ki * k_scale).astype(k_ref.dtype)
            v_ref[:, os_] = vi.astype(v_ref.dtype)
            qkv.append((os_, qi.astype(BF16), ki.astype(BF16), vi.astype(BF16)))
        return qkv

    def gate_logits(acc, qkv):
        for os_, qi, ki, vi in qkv:
            acc = acc + _dot(qi, wg_ref[0, os_, :])
            acc = acc + _dot(ki, wg_ref[1, os_, :])
            acc = acc + _dot(vi, wg_ref[2, os_, :])
        return acc

    gates = bg_ref[...]
    xm_prev = None
    qkv_prev = None
    for c0 in range(0, inner + 2 * width, width):
        xm = None
        if c0 < inner:
            z_ref[:, c0:c0 + width] = _dot(hb, wup_ref[:, inner + c0:inner + c0 + width])
            xm = _dot(hb, wup_ref[:, c0:c0 + width])
        qkv = None
        if xm_prev is not None:
            qkv = conv_qkv(c0 - width, xm_prev)
        if qkv_prev is not None:
            gates = gate_logits(gates, qkv_prev)
        xm_prev, qkv_prev = xm, qkv
    lane = lax.broadcasted_iota(jnp.int32, gates.shape, 1)
    log_sig = jnp.minimum(gates, 0.0) - jnp.log1p(jnp.exp(-jnp.abs(gates)))
    gates_ref[...] = jnp.where(lane < ML_HEADS, gates, log_sig)


def _up_conv_qkv_call(x, g, sc, sh, wup, conv0, conv_w, conv_b, wqk, wv, wg, bg, b, t_len):
    n, d = x.shape
    inner = wup.shape[1] // 2
    short_seq = t_len == SUBLANES
    dh = inner // ML_HEADS
    k_scale = dh ** -0.5
    qkv_dtype = F32 if short_seq else BF16
    halo = jnp.zeros((b, SUBLANES, inner), F32)
    if conv0 is not None:
        halo = halo.at[:, :ML_CONV - 1].set(conv0)
    tm = min(256, n) if short_seq else math.gcd(512, t_len)
    seq_per_step = tm // t_len if short_seq else 1
    steps_per_seq = 1 if short_seq else t_len // tm
    sc_op, sc_spec = _mod_operand(sc, t_len, tm)
    sh_op, sh_spec = _mod_operand(sh, t_len, tm)
    seq_spec = pl.BlockSpec((seq_per_step, SUBLANES, inner), lambda i: (i // steps_per_seq, 0, 0))
    row_spec = lambda w: pl.BlockSpec((tm, w), lambda i: (i, 0))
    const2 = lambda s: pl.BlockSpec(s, lambda i: (0, 0), pipeline_mode=pl.Buffered(1))
    const3 = lambda s: pl.BlockSpec(s, lambda i: (0, 0, 0), pipeline_mode=pl.Buffered(1))
    q, k, v, xc, z, gates, tail = pl.pallas_call(
        functools.partial(_up_conv_qkv_kernel, short_seq=short_seq, steps_per_seq=steps_per_seq,
                          k_scale=k_scale),
        grid=(n // tm,),
        in_specs=[
            row_spec(d), const2((1, d)), sc_spec, sh_spec, const2(wup.shape),
            seq_spec, const2((ML_CONV, inner)), const2((1, inner)),
            const3(wqk.shape), const3(wv.shape), const3(wg.shape), const2((1, LANES)),
        ],
        out_specs=[row_spec(inner), row_spec(inner), row_spec(inner), row_spec(inner), row_spec(inner),
                   row_spec(LANES), seq_spec],
        out_shape=[
            jax.ShapeDtypeStruct((n, inner), qkv_dtype),
            jax.ShapeDtypeStruct((n, inner), qkv_dtype),
            jax.ShapeDtypeStruct((n, inner), qkv_dtype),
            jax.ShapeDtypeStruct((n, inner), F32),
            jax.ShapeDtypeStruct((n, inner), F32),
            jax.ShapeDtypeStruct((n, LANES), F32),
            jax.ShapeDtypeStruct((b, SUBLANES, inner), F32),
        ],
        scratch_shapes=[pltpu.VMEM((SUBLANES, inner), F32)],
        compiler_params=_cparams("arbitrary"),
        name="mlstm_up_conv_qkv",
    )(x, g.reshape(1, d), sc_op, sh_op, wup, halo, conv_w, conv_b.reshape(1, inner), wqk, wv, wg, bg)
    return q, k, v, xc, z, gates, tail[:, :ML_CONV - 1]


def _mlstm_kernel(*refs, has_state):
    if has_state:
        (q_ref, k_ref, v_ref, gates_ref, xc_ref, z_ref, nw_ref, sk_ref, c0_ref, n0_ref, m0_ref,
         o_ref, c_ref, n_ref, m_ref) = refs
    else:
        (q_ref, k_ref, v_ref, gates_ref, xc_ref, z_ref, nw_ref, sk_ref,
         o_ref, c_ref, n_ref, m_ref) = refs
    ch = pl.program_id(1)
    length = q_ref.shape[0]
    dh = q_ref.shape[1] // ML_HEADS

    @pl.when(ch == 0)
    def _():
        if has_state:
            c_ref[...] = c0_ref[...]
            n_ref[...] = n0_ref[...]
            m_ref[...] = m0_ref[...]
        else:
            c_ref[...] = jnp.zeros_like(c_ref)
            n_ref[...] = jnp.zeros_like(n_ref)
            m_ref[...] = jnp.zeros_like(m_ref)

    gates = gates_ref[...]
    gates_t = gates.T
    t_idx = lax.broadcasted_iota(jnp.int32, (length, length), 0)
    s_idx = lax.broadcasted_iota(jnp.int32, (length, length), 1)
    causal = s_idx <= t_idx
    m_all = m_ref[0]
    m_new = m_all
    lane = lax.broadcasted_iota(jnp.int32, m_all.shape, 1)

    def finish_head(h, s_raw, q_c, q_n, w, w_c, m_t):
        hs = slice(h * dh, (h + 1) * dh)
        s = s_raw * w
        num = w_c * q_c + _dot(s.astype(BF16), v_ref[:, hs].astype(BF16))
        den = w_c * q_n + jnp.sum(s, axis=-1, keepdims=True)
        hc = num / jnp.maximum(jnp.abs(den), jnp.exp(-m_t))
        mu = jnp.mean(hc, axis=-1, keepdims=True)
        dev = hc - mu
        var = jnp.mean(dev * dev, axis=-1, keepdims=True)
        hn = dev * lax.rsqrt(var + EPS) * nw_ref[:, hs]
        ho = (hn + sk_ref[:, hs] * xc_ref[:, hs]) * _silu(z_ref[:, hs])
        o_ref[:, hs] = ho.astype(o_ref.dtype)

    pending = None
    for h in range(ML_HEADS):
        hs = slice(h * dh, (h + 1) * dh)
        ig_col = gates[:, h:h + 1]
        lf_col = gates[:, ML_HEADS + h:ML_HEADS + h + 1]
        ig_row = gates_t[h:h + 1, :]
        lf_row = gates_t[ML_HEADS + h:ML_HEADS + h + 1, :]
        b_col = jnp.sum(jnp.where(causal, lf_row, 0.0), axis=1, keepdims=True)
        b_row = jnp.sum(jnp.where(t_idx <= s_idx, lf_col, 0.0), axis=0, keepdims=True)
        m_prev = m_all[:, h:h + 1]
        dm = jnp.where(causal, b_col - b_row + ig_row, -jnp.inf)
        a = b_col + m_prev
        m_t = jnp.maximum(a, jnp.max(dm, axis=1, keepdims=True))
        w_c = jnp.exp(a - m_t)
        w = jnp.exp(dm - m_t)
        qh = q_ref[:, hs].astype(BF16)
        kh = k_ref[:, hs]
        vh = v_ref[:, hs].astype(BF16)
        c_h = c_ref[0, h]
        n_h = n_ref[0, :, hs]
        s_raw = _dot_nt(qh, kh.astype(BF16))
        q_c = _dot(qh, c_h.astype(BF16))
        q_n = jnp.sum(qh.astype(F32) * n_h, axis=-1, keepdims=True)
        m_last = m_t[length - 1:length]
        b_last = b_col[length - 1:length]
        wl_c = jnp.exp(a[length - 1:length] - m_last)
        wl_col = jnp.exp(b_last - b_col + ig_col - m_last)
        kw = kh.astype(F32) * wl_col
        c_ref[0, h] = wl_c * c_h + _dot_tn(kw.astype(BF16), vh)
        n_ref[0, :, hs] = wl_c * n_h + jnp.sum(kw, axis=0, keepdims=True)
        m_new = jnp.where(lane == h, m_last, m_new)
        if pending is not None:
            finish_head(*pending)
        pending = (h, s_raw, q_c, q_n, w, w_c, m_t)
    finish_head(*pending)
    m_ref[0] = m_new


def _mlstm_call(q, k, v, gates, xc, z, norm_w, skip, c0, n0, m0, b, t_len):
    n, inner = q.shape
    dh = inner // ML_HEADS
    has_state = c0 is not None
    length = math.gcd(t_len, ML_CHUNK)
    nc = t_len // length
    row = lambda w: pl.BlockSpec((length, w), lambda i, c: (i * nc + c, 0))
    const = lambda w: pl.BlockSpec((1, w), lambda i, c: (0, 0))
    c_spec = pl.BlockSpec((1, ML_HEADS, dh, dh), lambda i, c: (i, 0, 0, 0))
    n_spec = pl.BlockSpec((1, 1, inner), lambda i, c: (i, 0, 0))
    m_spec = pl.BlockSpec((1, 1, LANES), lambda i, c: (i, 0, 0))
    in_specs = [row(inner), row(inner), row(inner), row(LANES), row(inner), row(inner),
                const(inner), const(inner)]
    args = [q, k, v, gates, xc, z, norm_w.reshape(1, inner), skip.reshape(1, inner)]
    if has_state:
        in_specs += [c_spec, n_spec, m_spec]
        m0_pad = jnp.zeros((b, 1, LANES), F32).at[:, 0, :ML_HEADS].set(m0)
        args += [c0, n0.reshape(b, 1, inner), m0_pad]
    ho, c_t, n_t, m_t = pl.pallas_call(
        functools.partial(_mlstm_kernel, has_state=has_state),
        grid=(b, nc),
        in_specs=in_specs,
        out_specs=[row(inner), c_spec, n_spec, m_spec],
        out_shape=[
            jax.ShapeDtypeStruct((n, inner), BF16 if length % 16 == 0 else F32),
            jax.ShapeDtypeStruct((b, ML_HEADS, dh, dh), F32),
            jax.ShapeDtypeStruct((b, 1, inner), F32),
            jax.ShapeDtypeStruct((b, 1, LANES), F32),
        ],
        compiler_params=_cparams("parallel", "arbitrary"),
        name="mlstm_scan",
    )(*args)
    return ho, c_t, n_t.reshape(b, ML_HEADS, dh), m_t[:, 0, :ML_HEADS]


def _block_diag_tiles(w):
    n_blk, blk, _ = w.shape
    per = MXU_DIM // blk
    rows = w.reshape(n_blk // per, per, blk, blk).transpose(0, 1, 3, 2).reshape(n_blk // per, MXU_DIM, blk)
    r_blk = np.arange(MXU_DIM)[:, None] // blk
    c_blk = np.arange(MXU_DIM)[None, :] // blk
    same_block = jnp.asarray((r_blk == c_blk).astype(np.float32))
    return jnp.tile(rows, (1, 1, per)) * same_block


def _prep_weights(p):
    w = {}
    w['hg_win'] = p['hg_win'].astype(BF16)
    w['hg_wo'] = p['hg_wo'].astype(BF16)
    w['ml_wup'] = p['ml_wup'].astype(BF16)
    w['ml_wdown'] = p['ml_wdown'].astype(BF16)
    dep, n_exp, d_model, dff = p['moe_w1'].shape
    n_stack = dep * n_exp // MOE_EPG
    w['moe_w1'] = p['moe_w1'].reshape(n_stack, MOE_EPG, d_model, dff)
    w['moe_w3'] = p['moe_w3'].reshape(n_stack, MOE_EPG, d_model, dff)
    w['moe_w2'] = p['moe_w2'].reshape(n_stack, MOE_EPG, dff, d_model)
    depth, d, g = p['moe_wrg'].shape
    n_exp = g * p['moe_wre'].shape[-1]
    wr = jnp.zeros((depth, d, LANES), F32)
    wr = wr.at[:, :, :g].set(p['moe_wrg']).at[:, :, g:g + n_exp].set(p['moe_wre'].reshape(depth, d, n_exp))
    w['moe_wr'] = wr
    br = jnp.zeros((depth, 1, LANES), F32)
    br = br.at[:, 0, :g].set(p['moe_brg']).at[:, 0, g:g + n_exp].set(p['moe_bre'].reshape(depth, n_exp))
    w['moe_br'] = br
    n_b = p['ml_wq'].shape[0]
    wq = jnp.stack([_block_diag_tiles(p['ml_wq'][j]) for j in range(n_b)])
    wk = jnp.stack([_block_diag_tiles(p['ml_wk'][j]) for j in range(n_b)])
    wv = jnp.stack([_block_diag_tiles(p['ml_wv'][j]) for j in range(n_b)])
    w['ml_wqk'] = jnp.concatenate([wq, wk], axis=-1).astype(BF16)
    w['ml_wv'] = wv.astype(BF16)
    inner = p['ml_conv_b'].shape[-1]
    wg = jnp.zeros((n_b, 3 * inner, LANES), F32)
    wg = wg.at[:, :, :ML_HEADS].set(p['ml_wig']).at[:, :, ML_HEADS:2 * ML_HEADS].set(p['ml_wfg'])
    w['ml_wg'] = wg.reshape(n_b, 3, inner, LANES).astype(BF16)
    bg = jnp.zeros((n_b, 1, LANES), F32)
    bg = bg.at[:, 0, :ML_HEADS].set(p['ml_big']).at[:, 0, ML_HEADS:2 * ML_HEADS].set(p['ml_bfg'])
    w['ml_bg'] = bg
    return w


def _lb_kernel(lb_ref, o_ref):
    x = lb_ref[...]
    mx = jnp.max(x, axis=0, keepdims=True)
    ex = jnp.exp(x - mx)
    sm = ex / jnp.sum(ex, axis=0, keepdims=True)
    rows = []
    run = jnp.zeros_like(sm[0:1])
    for i in range(x.shape[0]):
        run = run + sm[i:i + 1]
        rows.append(run)
    o_ref[...] = jnp.concatenate(rows, axis=0)


def _lb_call(hg_lb):
    return pl.pallas_call(
        _lb_kernel,
        out_shape=jax.ShapeDtypeStruct(hg_lb.shape, F32),
        name="hgrn_lower_bound",
    )(hg_lb)


def _trunk(x3, mods, s_hg, s_c, s_n, s_m, s_conv, p, w, lb_all):
    b, t_len, d = x3.shape
    n = b * t_len
    x = x3.reshape(n, d)
    depth = p['norm_g'].shape[0]
    new_hg, new_c, new_n, new_m, new_conv = [], [], [], [], []
    for l in range(depth):
        sh1, sc1, g1, sh2, sc2, g2 = [mods[l][:, i * d:(i + 1) * d] for i in range(6)]
        if l % 2 == 0:
            a = l // 2
            proj = _norm_mm_call(x, p['norm_g'][l, 0], sc1, sh1, w['hg_win'][a], t_len, "hgrn_in_proj")
            o, s_t = _gla_call(proj, lb_all[l], p['hg_norm'][a], None if s_hg is None else s_hg[a], b, t_len)
            new_hg.append(s_t)
            mix = (o, w['hg_wo'][a], g1)
        else:
            j = l // 2
            q, k, v, xc, z, gates, conv_tail = _up_conv_qkv_call(
                x, p['norm_g'][l, 0], sc1, sh1, w['ml_wup'][j],
                None if s_conv is None else s_conv[j], p['ml_conv_w'][j], p['ml_conv_b'][j],
                w['ml_wqk'][j], w['ml_wv'][j], w['ml_wg'][j], w['ml_bg'][j], b, t_len)
            ho, c_t, n_t, m_t = _mlstm_call(
                q, k, v, gates, xc, z, p['ml_norm'][j], p['ml_skip'][j],
                None if s_c is None else s_c[j], None if s_n is None else s_n[j],
                None if s_m is None else s_m[j], b, t_len)
            new_c.append(c_t)
            new_n.append(n_t)
            new_m.append(m_t)
            new_conv.append(conv_tail)
            mix = (ho.astype(BF16), w['ml_wdown'][j], g1)
        x = _moe_call(mix, x, p['norm_g'][l, 1], sc2, sh2, g2, w['moe_wr'][l], w['moe_br'][l],
                      w['moe_w1'], w['moe_w3'], w['moe_w2'], l,
                      p['final_g'] if l == depth - 1 else None, t_len, "moe_layer%d" % l)
    return (x.reshape(b, t_len, d), jnp.stack(new_hg), jnp.stack(new_c), jnp.stack(new_n),
            jnp.stack(new_m), jnp.stack(new_conv))


def kernel(x_prompt, x_sample, c_prompt, c_sample, state_hgrn, state_mlstm_c, state_mlstm_n, state_mlstm_m, state_conv, w_ada, b_ada, norm_g, final_g, hg_win, hg_wo, hg_norm, hg_lb, ml_wup, ml_conv_w, ml_conv_b, ml_wq, ml_wk, ml_wv, ml_wig, ml_big, ml_wfg, ml_bfg, ml_norm, ml_skip, ml_wdown, moe_wrg, moe_brg, moe_wre, moe_bre, moe_w1, moe_w3, moe_w2):
    p = dict(w_ada=w_ada, b_ada=b_ada, norm_g=norm_g, final_g=final_g,
             hg_win=hg_win, hg_wo=hg_wo, hg_norm=hg_norm, hg_lb=hg_lb,
             ml_wup=ml_wup, ml_conv_w=ml_conv_w, ml_conv_b=ml_conv_b, ml_wq=ml_wq, ml_wk=ml_wk, ml_wv=ml_wv,
             ml_wig=ml_wig, ml_big=ml_big, ml_wfg=ml_wfg, ml_bfg=ml_bfg, ml_norm=ml_norm, ml_skip=ml_skip,
             ml_wdown=ml_wdown, moe_wrg=moe_wrg, moe_brg=moe_brg, moe_wre=moe_wre, moe_bre=moe_bre,
             moe_w1=moe_w1, moe_w3=moe_w3, moe_w2=moe_w2)
    w = _prep_weights(p)
    lb_all = _lb_call(hg_lb)
    bp = x_prompt.shape[0]
    c_all = jnp.concatenate([c_prompt, c_sample], axis=0)
    mod_all = _ada_call(c_all, w_ada, b_ada)
    mods_p = [mod_all[l, :bp] for l in range(mod_all.shape[0])]
    mods_s = [mod_all[l, bp:] for l in range(mod_all.shape[0])]
    y_p, hg_p, mc_p, mn_p, mm_p, conv_p = _trunk(x_prompt, mods_p, None, None, None, None, None, p, w, lb_all)
    y_s, hg_s, mc_s, mn_s, mm_s, conv_s = _trunk(x_sample, mods_s, state_hgrn, state_mlstm_c, state_mlstm_n,
                                                 state_mlstm_m, state_conv, p, w, lb_all)
    return (y_p, y_s, hg_p, mc_p, mn_p, mm_p, conv_p, hg_s, mc_s, mn_s, mm_s, conv_s)
```

```python
import functools
import math

import numpy as np
import jax
import jax.numpy as jnp
from jax import lax
from jax.experimental import pallas as pl
from jax.experimental.pallas import tpu as pltpu

F32 = jnp.float32
BF16 = jnp.bfloat16
EPS = 1e-6

HG_DK = 128
ML_HEADS = 4
ML_CONV = 4
ML_QKV_BLOCK = 4
MOE_GROUPS = 4
MOE_EPG = 4
CHUNK = 64
GLA_CHUNKS_PER_STEP = 8
ML_CHUNK = 256

LANES = 128
SUBLANES = 8
MXU_DIM = 256
VMEM_LIMIT_BYTES = 56 * 1024 * 1024


def _cparams(*sem):
    return pltpu.CompilerParams(dimension_semantics=sem, vmem_limit_bytes=VMEM_LIMIT_BYTES)


def _silu(x):
    return x * jax.nn.sigmoid(x)


def _dot(a, b):
    return jnp.dot(a, b, preferred_element_type=F32)


def _dot_nt(a, b):
    return lax.dot_general(a, b, (((1,), (1,)), ((), ())), preferred_element_type=F32)


def _dot_tn(a, b):
    return lax.dot_general(a, b, (((0,), (0,)), ((), ())), preferred_element_type=F32)


def _rms_mod(x, g, sc, sh):
    ms = jnp.mean(x * x, axis=-1, keepdims=True)
    h = x * lax.rsqrt(ms + EPS) * g
    return h * (1.0 + sc) + sh


def _ada_kernel(c_ref, w_ref, b_ref, o_ref):
    cm = _silu(c_ref[...]).astype(BF16)
    o_ref[0] = _dot(cm, w_ref[0].astype(BF16)) + b_ref[0]


def _ada_call(c_all, w_ada, b_ada):
    depth, d, n_out = w_ada.shape
    m = c_all.shape[0]
    tn = 512
    return pl.pallas_call(
        _ada_kernel,
        grid=(depth, n_out // tn),
        in_specs=[
            pl.BlockSpec((m, d), lambda l, j: (0, 0)),
            pl.BlockSpec((1, d, tn), lambda l, j: (l, 0, j)),
            pl.BlockSpec((1, 1, tn), lambda l, j: (l, 0, j)),
        ],
        out_specs=pl.BlockSpec((1, m, tn), lambda l, j: (l, 0, j)),
        out_shape=jax.ShapeDtypeStruct((depth, m, n_out), F32),
        compiler_params=_cparams("parallel", "parallel"),
        name="ada_mod",
    )(c_all, w_ada, b_ada.reshape(depth, 1, n_out))


def _mod_operand(m, t_len, tm):
    b, d = m.shape
    if t_len % tm == 0:
        per_b = t_len // tm
        return m.reshape(b, 1, d), pl.BlockSpec((1, 1, d), lambda i, *_: (i // per_b, 0, 0))
    assert tm % t_len == 0
    nb = tm // t_len
    return m.reshape(b // nb, nb, 1, d), pl.BlockSpec((1, nb, 1, d), lambda i, *_: (i, 0, 0, 0))


def _mod_rows(ref, rows):
    v = ref[0]
    if v.ndim == 2:
        return v
    nb, _, d = v.shape
    return jnp.broadcast_to(v, (nb, rows // nb, d)).reshape(rows, d)


def _row_tile(n, t_len, target):
    tm = min(target, n)
    while n % tm or (t_len % tm and tm % t_len):
        tm //= 2
    return tm


def _norm_mm_kernel(x_ref, g_ref, sc_ref, sh_ref, w_ref, o_ref, *, col_chunk):
    rows = x_ref.shape[0]
    hb = _rms_mod(x_ref[...], g_ref[...], _mod_rows(sc_ref, rows), _mod_rows(sh_ref, rows)).astype(BF16)
    for c0 in range(0, o_ref.shape[1], col_chunk):
        o_ref[:, c0:c0 + col_chunk] = _dot(hb, w_ref[:, c0:c0 + col_chunk])


def _norm_mm_call(x, g, sc, sh, w, t_len, name):
    n, d = x.shape
    n_out = w.shape[1]
    tm = _row_tile(n, t_len, 1024 if t_len % 1024 == 0 else 256)
    sc_op, sc_spec = _mod_operand(sc, t_len, tm)
    sh_op, sh_spec = _mod_operand(sh, t_len, tm)
    return pl.pallas_call(
        functools.partial(_norm_mm_kernel, col_chunk=512),
        grid=(n // tm,),
        in_specs=[
            pl.BlockSpec((tm, d), lambda i: (i, 0)),
            pl.BlockSpec((1, d), lambda i: (0, 0)),
            sc_spec, sh_spec,
            pl.BlockSpec((d, n_out), lambda i: (0, 0), pipeline_mode=pl.Buffered(1)),
        ],
        out_specs=pl.BlockSpec((tm, n_out), lambda i: (i, 0)),
        out_shape=jax.ShapeDtypeStruct((n, n_out), F32),
        compiler_params=_cparams("parallel"),
        name=name,
    )(x, g.reshape(1, d), sc_op, sh_op, w)


def _gla_tables(t_sub, nseq):
    r = t_sub * nseq
    levels = []
    m = t_sub // 2
    while m >= 1:
        levels.append(m)
        m //= 2
    n_lev = len(levels)
    tril = np.zeros((r, r), np.float32)
    mask = np.zeros((n_lev + 1, r, r), np.float32)
    for li, m in enumerate(levels):
        for row in range(r):
            blk = (row // (2 * m)) * 2 * m
            if row - blk >= m:
                mask[li, row, blk:blk + m] = 1.0
    for row in range(r):
        s0 = (row // t_sub) * t_sub
        tril[row, s0:row + 1] = 1.0
        mask[n_lev, row, row] = 1.0
    return tril, mask, levels


def _bcast_block_row(b, block, row_in_block):
    parts = [jnp.broadcast_to(b[s + row_in_block:s + row_in_block + 1, :], (block, b.shape[1]))
             for s in range(0, b.shape[0], block)]
    return parts[0] if len(parts) == 1 else jnp.concatenate(parts, axis=0)


def _level_decay(b, m):
    r = b.shape[0]
    pos = lax.broadcasted_iota(jnp.int32, b.shape, 0) & (2 * m - 1)
    if 2 * m >= SUBLANES:
        b_mid = _bcast_block_row(b, 2 * m, m - 1)
    else:
        b_mid = b
        for p in range(2 * m):
            if p != m - 1:
                b_mid = jnp.where(pos == p, pltpu.roll(b, (p - (m - 1)) % r, 0), b_mid)
    return jnp.exp2(jnp.where(pos >= m, b - b_mid, b_mid - b))


def _gla_kernel(*refs, t_sub, nseq, n_ch, levels, n_heads, has_state, steps_per_seq):
    if has_state:
        proj_ref, lb_ref, gn_ref, tril_ref, mask_ref, s0_ref, o_ref, sout_ref, st_ref = refs
    else:
        proj_ref, lb_ref, gn_ref, tril_ref, mask_ref, o_ref, sout_ref, st_ref = refs
    r = t_sub * nseq
    n_lev = len(levels)
    dk = HG_DK
    hk = n_heads * dk
    c = lax.rem(pl.program_id(0), steps_per_seq)

    @pl.when(c == 0)
    def _():
        if has_state:
            for j in range(nseq):
                for h in range(n_heads):
                    st_ref[j, h] = s0_ref[j, h].T
        else:
            st_ref[...] = jnp.zeros_like(st_ref)

    gn = gn_ref[...]
    tril = tril_ref[...]

    def finish_head(rows, h, a, qh, kh, e_cum_h, e_end_h, d_last):
        hs = slice(h * dk, (h + 1) * dk)
        vb = proj_ref[rows, 2 * hk + h * dk:2 * hk + (h + 1) * dk].astype(BF16)
        o_intra = _dot(a.astype(BF16), vb)
        qd = qh * e_cum_h
        kd = kh * e_end_h
        o_parts = []
        for j in range(nseq):
            rs = slice(j * t_sub, (j + 1) * t_sub)
            st = st_ref[j, h]
            o_parts.append(_dot_nt(qd[rs], st.astype(BF16)))
            st_ref[j, h] = st * d_last[j] + _dot_tn(vb[rs], kd[rs])
        o_inter = o_parts[0] if nseq == 1 else jnp.concatenate(o_parts, axis=0)
        o = o_intra + o_inter
        o = o * lax.rsqrt(jnp.mean(o * o, axis=-1, keepdims=True) + EPS) * gn
        zg = proj_ref[rows, 3 * hk + h * dk:3 * hk + (h + 1) * dk]
        o_ref[rows, hs] = (o * _silu(zg)).astype(BF16)

    pending = None
    head_group = n_heads if nseq == 1 else 1
    for ci in range(n_ch):
        rows = slice(ci * r, (ci + 1) * r)
        for g0 in range(0, n_heads, head_group):
            gw = head_group * dk
            gs = slice(g0 * dk, g0 * dk + gw)
            lb = lb_ref[:, gs]
            zq = proj_ref[rows, g0 * dk:g0 * dk + gw]
            zf = proj_ref[rows, hk + g0 * dk:hk + g0 * dk + gw]
            f = lb + (1.0 - lb) * jax.nn.sigmoid(zf)
            lf = jnp.log2(f)
            q = _silu(zq).astype(BF16)
            k = (1.0 - f).astype(BF16)

            p0 = lf.astype(BF16)
            r1 = lf - p0.astype(F32)
            p1 = r1.astype(BF16)
            p2 = (r1 - p1.astype(F32)).astype(BF16)
            b = _dot(tril, p0) + _dot(tril, p1) + _dot(tril, p2)
            e_cum = jnp.exp2(b)
            e_cum_b = e_cum.astype(BF16)
            e_end_b = jnp.exp2(_bcast_block_row(b, t_sub, t_sub - 1) - b).astype(BF16)
            zs = [_level_decay(b, m).astype(BF16) for m in levels]

            for hh in range(head_group):
                h = g0 + hh
                ls = slice(hh * dk, (hh + 1) * dk)
                qh = q[:, ls]
                kh = k[:, ls]
                a = _dot_nt(qh, kh) * mask_ref[n_lev]
                for li in range(n_lev):
                    z = zs[li][:, ls]
                    a = a + _dot_nt(qh * z, kh * z) * mask_ref[li]
                item = (rows, h, a, qh, kh, e_cum_b[:, ls], e_end_b[:, ls],
                        [e_cum[(j + 1) * t_sub - 1:(j + 1) * t_sub, ls] for j in range(nseq)])
                if nseq > 1:
                    finish_head(*item)
                else:
                    if pending is not None:
                        finish_head(*pending)
                    pending = item
    if pending is not None:
        finish_head(*pending)

    @pl.when(c == steps_per_seq - 1)
    def _():
        for j in range(nseq):
            for h in range(n_heads):
                sout_ref[j, h] = st_ref[j, h].T


def _gla_call(proj, lb, gn, s0, b, t_len):
    n = proj.shape[0]
    hk = lb.shape[-1]
    n_heads = hk // HG_DK
    has_state = s0 is not None
    if t_len % CHUNK == 0:
        t_sub, nseq = CHUNK, 1
    else:
        t_sub, nseq = t_len, CHUNK // t_len
        assert t_sub * nseq == CHUNK and b % nseq == 0
    r = t_sub * nseq
    tril, mask, levels = _gla_tables(t_sub, nseq)
    n_ch = math.gcd(GLA_CHUNKS_PER_STEP, t_len // t_sub)
    rows = n_ch * r
    steps_per_seq = t_len // (t_sub * n_ch)
    n_steps = n // rows
    const2 = lambda s: pl.BlockSpec(s, lambda i: (0, 0))
    in_specs = [pl.BlockSpec((rows, proj.shape[1]), lambda i: (i, 0)),
                const2((1, hk)), const2((1, HG_DK)), const2(tril.shape),
                pl.BlockSpec(mask.shape, lambda i: (0, 0, 0))]
    args = [proj, lb.reshape(1, hk), gn.reshape(1, HG_DK), jnp.asarray(tril, BF16), jnp.asarray(mask, F32)]
    s_spec = pl.BlockSpec((nseq, n_heads, HG_DK, HG_DK), lambda i: (i // steps_per_seq, 0, 0, 0))
    if has_state:
        in_specs.append(s_spec)
        args.append(s0)
    o, s_out = pl.pallas_call(
        functools.partial(_gla_kernel, t_sub=t_sub, nseq=nseq, n_ch=n_ch, levels=tuple(levels),
                          n_heads=n_heads, has_state=has_state, steps_per_seq=steps_per_seq),
        grid=(n_steps,),
        in_specs=in_specs,
        out_specs=[
            pl.BlockSpec((rows, hk), lambda i: (i, 0)),
            s_spec,
        ],
        out_shape=[
            jax.ShapeDtypeStruct((n, hk), BF16),
            jax.ShapeDtypeStruct((b, n_heads, HG_DK, HG_DK), F32),
        ],
        scratch_shapes=[pltpu.VMEM((nseq, n_heads, HG_DK, HG_DK), F32)],
        compiler_params=_cparams("arbitrary"),
        name="gla_scan",
    )(*args)
    return o, s_out


def _route_kernel(a_ref, wmix_ref, gmix_ref, x_ref, g_ref, sc_ref, sh_ref, wr_ref, br_ref, tril_ref,
                  xnew_ref, hx_ref, meta_ref, cnt_ref, *, n_groups, epg):
    rows, d = x_ref.shape

    @pl.when(pl.program_id(0) == 0)
    def _():
        cnt_ref[...] = jnp.zeros_like(cnt_ref)

    halves = [slice(k * tril_ref.shape[0], (k + 1) * tril_ref.shape[0])
              for k in range(rows // tril_ref.shape[0])]
    part = lambda v, rs: v if v.shape[0] == 1 else v[rs]
    gmix, scale, shift = (_mod_rows(r, rows) for r in (gmix_ref, sc_ref, sh_ref))
    projs = [_dot(a_ref[rs, :], wmix_ref[...]) for rs in halves]

    wr = wr_ref[...]
    w_hi = wr.astype(BF16)
    w_lo = (wr - w_hi.astype(F32)).astype(BF16)
    logits = []
    for rs, proj in zip(halves, projs):
        x = x_ref[rs, :] + part(gmix, rs) * proj
        xnew_ref[rs, :] = x
        h = _rms_mod(x, g_ref[...], part(scale, rs), part(shift, rs))
        hx_ref[rs, :d] = h
        h_hi = h.astype(BF16)
        h_lo = (h - h_hi.astype(F32)).astype(BF16)
        logits.append(_dot(h_hi, w_hi) + _dot(h_hi, w_lo) + _dot(h_lo, w_hi) + br_ref[...])
    for rs, logit in zip(halves, logits):
        _route_rows(rs, logit, tril_ref, hx_ref, meta_ref, cnt_ref, d, n_groups, epg)


def _route_rows(rs, logit, tril_ref, hx_ref, meta_ref, cnt_ref, d, n_groups, epg):
    neg = -jnp.inf
    far = float(LANES)
    lane = lax.broadcasted_iota(jnp.int32, logit.shape, 1).astype(F32)
    gm = lane < n_groups
    gmax = jnp.max(jnp.where(gm, logit, neg), axis=-1, keepdims=True)
    gstar = jnp.min(jnp.where(gm, jnp.where(logit == gmax, lane, far), far), axis=-1, keepdims=True)
    psum = jnp.sum(jnp.where(gm, jnp.exp(logit - gmax), 0.0), axis=-1, keepdims=True)
    pstar = 1.0 / psum
    lo = n_groups + gstar * epg
    em = jnp.where(lane >= lo, jnp.where(lane < lo + epg, 1.0, 0.0), 0.0)
    l1 = jnp.where(em > 0.0, logit, neg)
    v1 = jnp.max(l1, axis=-1, keepdims=True)
    i1 = jnp.min(jnp.where(l1 == v1, lane, far), axis=-1, keepdims=True)
    l2 = jnp.where(lane == i1, neg, l1)
    v2 = jnp.max(l2, axis=-1, keepdims=True)
    i2 = jnp.min(jnp.where(l2 == v2, lane, far), axis=-1, keepdims=True)
    e2 = jnp.exp(v2 - v1)
    wt1 = pstar / (1.0 + e2)
    wt2 = pstar * e2 / (1.0 + e2)
    onehot = jnp.where(lane == gstar, 1.0, 0.0)
    within = _dot(tril_ref[...], onehot.astype(BF16))
    carry = cnt_ref[...]
    rank = jnp.sum(onehot * (within + carry), axis=-1, keepdims=True)
    cnt_ref[...] = carry + jnp.sum(onehot, axis=0, keepdims=True)
    meta = (jnp.where(lane == i1 - lo, wt1, 0.0) + jnp.where(lane == i2 - lo, wt2, 0.0)
            + jnp.where(lane == epg, gstar, 0.0) + jnp.where(lane == epg + 1, rank, 0.0))
    hx_ref[rs, d:] = meta
    meta_ref[:, rs] = meta.T[0:SUBLANES]


def _route_call(a, w_mix, gate_mix, x, g, sc, sh, wr, br, t_len, name):
    n, d = x.shape
    kdim = a.shape[1]
    tm = _row_tile(n, t_len, 1024 if t_len % 1024 == 0 else 256)
    gm_op, gm_spec = _mod_operand(gate_mix, t_len, tm)
    sc_op, sc_spec = _mod_operand(sc, t_len, tm)
    sh_op, sh_spec = _mod_operand(sh, t_len, tm)
    half = tm // 2 if tm % (2 * LANES) == 0 else tm
    tril = jnp.asarray(np.tril(np.ones((half, half), np.float32), -1), BF16)
    const = lambda s: pl.BlockSpec(s, lambda i: (0, 0), pipeline_mode=pl.Buffered(1))
    return pl.pallas_call(
        functools.partial(_route_kernel, n_groups=MOE_GROUPS, epg=MOE_EPG),
        grid=(n // tm,),
        in_specs=[
            pl.BlockSpec((tm, kdim), lambda i: (i, 0)),
            const((kdim, d)), gm_spec,
            pl.BlockSpec((tm, d), lambda i: (i, 0)),
            const((1, d)), sc_spec, sh_spec,
            const((d, LANES)), const((1, LANES)), const((half, half)),
        ],
        out_specs=[
            pl.BlockSpec((tm, d), lambda i: (i, 0)),
            pl.BlockSpec((tm, d + LANES), lambda i: (i, 0)),
            pl.BlockSpec((SUBLANES, tm), lambda i: (0, i)),
            pl.BlockSpec((1, LANES), lambda i: (0, 0)),
        ],
        out_shape=[
            jax.ShapeDtypeStruct((n, d), F32),
            jax.ShapeDtypeStruct((n, d + LANES), F32),
            jax.ShapeDtypeStruct((SUBLANES, n), F32),
            jax.ShapeDtypeStruct((1, LANES), F32),
        ],
        compiler_params=_cparams("arbitrary"),
        name=name + "_route",
    )(a, w_mix, gm_op, x, g.reshape(1, d), sc_op, sh_op, wr, br, tril)


def _row_gather(idx_ref, src_hbm, buf, sem, tile, slot, start, unrolled=False):
    groups = buf.shape[1]
    base = tile * (groups * SUBLANES)

    def run(s):
        def body(i, carry):
            for u in range(SUBLANES):
                idx = idx_ref[base + i * SUBLANES + u]
                cp = pltpu.make_async_copy(src_hbm.at[pl.ds(idx, 1)],
                                           buf.at[s, i, pl.ds(u, 1)], sem.at[s])
                if start:
                    cp.start(priority=u % 2)
                else:
                    cp.wait()
            return carry

        if unrolled:
            for i in range(groups):
                body(i, 0)
        else:
            lax.fori_loop(0, groups, body, 0)

    for s in range(2):
        @pl.when(slot == s)
        def _():
            run(s)


def _expert_kernel(pad_ref, dest_ref, tgrp_ref, tval_ref, hx_hbm, w1_ref, w3_ref, w2_ref, y_ref,
                   hbuf, sem, wb1, wb3, wb2, src_ref, *, epg, d):
    j = pl.program_id(0)
    rows = y_ref.shape[0]

    @pl.when(j == 0)
    def _():
        _invperm(pad_ref, dest_ref, src_ref)

    @pl.when(jnp.logical_or(j == 0, tgrp_ref[j] != tgrp_ref[jnp.maximum(j - 1, 0)]))
    def _():
        for e in range(epg):
            wb1[e] = w1_ref[0, e].astype(BF16)
            wb3[e] = w3_ref[0, e].astype(BF16)
            wb2[e] = w2_ref[0, e].astype(BF16)

    @pl.when(j == 0)
    def _():
        @pl.when(tval_ref[0] == 1)
        def _():
            _row_gather(src_ref, hx_hbm, hbuf, sem, 0, 0, True)

    @pl.when(j + 1 < pl.num_programs(0))
    def _():
        @pl.when(tval_ref[j + 1] == 1)
        def _():
            _row_gather(src_ref, hx_hbm, hbuf, sem, j + 1, (j + 1) % 2, True, unrolled=True)

    @pl.when(tval_ref[j] == 1)
    def _():
        slot = j % 2
        _row_gather(src_ref, hx_hbm, hbuf, sem, j, slot, False)
        tile = hbuf[slot].reshape(rows, hbuf.shape[-1])
        hb = tile[:, :d].astype(BF16)
        hids = []
        for e in range(epg):
            a = _dot(hb, wb1[e])
            b = _dot(hb, wb3[e])
            hids.append((_silu(a) * b * tile[:, d + e:d + e + 1]).astype(BF16))
        hid = jnp.concatenate(hids, axis=1)
        y_ref[...] = _dot(hid, wb2[...].reshape(hid.shape[1], d))

    @pl.when(tval_ref[j] == 0)
    def _():
        y_ref[...] = jnp.zeros_like(y_ref)


def _combine_kernel(*refs, final):
    if final:
        dest_ref, ys_hbm, x_ref, gate_ref, fg_ref, o_ref, gbuf, sem = refs
    else:
        dest_ref, ys_hbm, x_ref, gate_ref, o_ref, gbuf, sem = refs
    i = pl.program_id(0)
    rows = x_ref.shape[0]

    @pl.when(i == 0)
    def _():
        _row_gather(dest_ref, ys_hbm, gbuf, sem, 0, 0, True)

    @pl.when(i + 1 < pl.num_programs(0))
    def _():
        _row_gather(dest_ref, ys_hbm, gbuf, sem, i + 1, (i + 1) % 2, True, unrolled=True)

    slot = i % 2
    _row_gather(dest_ref, ys_hbm, gbuf, sem, i, slot, False)
    y = x_ref[...] + _mod_rows(gate_ref, rows) * gbuf[slot].reshape(rows, gbuf.shape[-1])
    if final:
        y = y * lax.rsqrt(jnp.mean(y * y, axis=-1, keepdims=True) + EPS) * fg_ref[...]
    o_ref[...] = y


def _invperm(pad_ref, dest_ref, src_ref):
    def clear(i, carry):
        src_ref[i] = 0
        return carry

    def place(i, carry):
        src_ref[dest_ref[i]] = i
        return carry

    for rng in range(pad_ref.shape[0] // 2):
        lax.fori_loop(pad_ref[2 * rng], pad_ref[2 * rng + 1], clear, 0)
    lax.fori_loop(0, dest_ref.shape[0], place, 0, unroll=8)


def _moe_call(mix, x, g, sc, sh, gate, wr, br, w1, w3, w2, layer, final_g, t_len, name):
    n, d = x.shape
    epg, dff = w1.shape[1], w1.shape[3]
    n_grp = MOE_GROUPS
    x, hx, meta, cnt = _route_call(*mix, x, g, sc, sh, wr, br, t_len, name)

    tm2 = 512 if n >= 8192 else 128
    grp = meta[epg].astype(jnp.int32)
    rank = meta[epg + 1].astype(jnp.int32)
    counts = cnt[0, :n_grp].astype(jnp.int32)
    n_tiles_g = (counts + tm2 - 1) // tm2
    tile_end = jnp.cumsum(n_tiles_g)
    tile_start = tile_end - n_tiles_g
    dest = rank
    for gi in range(n_grp):
        dest = dest + jnp.where(grp == gi, tile_start[gi] * tm2, 0)
    n_tiles = n // tm2 + n_grp
    pad_lo = jnp.concatenate([tile_start * tm2 + counts, tile_end[-1:] * tm2])
    pad_hi = jnp.concatenate([tile_end * tm2, jnp.full((1,), n_tiles * tm2, jnp.int32)])
    pad = jnp.stack([pad_lo, pad_hi], axis=1).reshape(-1)
    jt = jnp.arange(n_tiles, dtype=jnp.int32)
    tval = (jt < tile_end[-1]).astype(jnp.int32)
    tgrp = jnp.sum((jt[:, None] >= tile_end[None, :]).astype(jnp.int32), axis=1)
    last_grp = jnp.sum((tile_end[-1] - 1 >= tile_end).astype(jnp.int32))
    tgrp = layer * n_grp + jnp.minimum(jnp.where(tval == 1, tgrp, last_grp), n_grp - 1)

    wspec = lambda shape: pl.BlockSpec(shape, lambda j, pad, dest, tgrp, tval: (tgrp[j], 0, 0, 0))
    ys = pl.pallas_call(
        functools.partial(_expert_kernel, epg=epg, d=d),
        grid_spec=pltpu.PrefetchScalarGridSpec(
            num_scalar_prefetch=4,
            grid=(n_tiles,),
            in_specs=[
                pl.BlockSpec(memory_space=pl.ANY),
                wspec((1, epg, d, dff)), wspec((1, epg, d, dff)), wspec((1, epg, dff, d)),
            ],
            out_specs=pl.BlockSpec((tm2, d), lambda j, pad, dest, tgrp, tval: (j, 0)),
            scratch_shapes=[pltpu.VMEM((2, tm2 // SUBLANES, SUBLANES, d + LANES), F32),
                            pltpu.SemaphoreType.DMA((2,)),
                            pltpu.VMEM((epg, d, dff), BF16), pltpu.VMEM((epg, d, dff), BF16),
                            pltpu.VMEM((epg, dff, d), BF16),
                            pltpu.SMEM((n_tiles * tm2,), jnp.int32)],
        ),
        out_shape=jax.ShapeDtypeStruct((n_tiles * tm2, d), F32),
        compiler_params=_cparams("arbitrary"),
        name=name + "_experts",
    )(pad, dest, tgrp, tval, hx, w1, w3, w2)

    tm = _row_tile(n, t_len, 512 if t_len % 512 == 0 else 256)
    gate_op, gate_spec = _mod_operand(gate, t_len, tm)
    final = final_g is not None
    in_specs = [
        pl.BlockSpec(memory_space=pl.ANY),
        pl.BlockSpec((tm, d), lambda i, dest: (i, 0)),
        gate_spec,
    ]
    args = [dest, ys, x, gate_op]
    if final:
        in_specs.append(pl.BlockSpec((1, d), lambda i, dest: (0, 0)))
        args.append(final_g.reshape(1, d))
    return pl.pallas_call(
        functools.partial(_combine_kernel, final=final),
        grid_spec=pltpu.PrefetchScalarGridSpec(
            num_scalar_prefetch=1,
            grid=(n // tm,),
            in_specs=in_specs,
            out_specs=pl.BlockSpec((tm, d), lambda i, dest: (i, 0)),
            scratch_shapes=[pltpu.VMEM((2, tm // SUBLANES, SUBLANES, d), F32),
                            pltpu.SemaphoreType.DMA((2,))],
        ),
        out_shape=jax.ShapeDtypeStruct((n, d), F32),
        compiler_params=_cparams("arbitrary"),
        name=name + "_combine",
    )(*args)


def _conv_taps(xm, halo, conv_w, conv_b, row_in_seq, axis):
    acc = conv_b + xm * conv_w[ML_CONV - 1]
    for s in range(1, ML_CONV):
        shifted = pltpu.roll(xm, s, axis)
        fill = pltpu.roll(halo, (s + SUBLANES - (ML_CONV - 1)) % SUBLANES, axis)
        if axis == 0:
            top = jnp.where(row_in_seq < s, fill, shifted[0:SUBLANES])
            shifted = jnp.concatenate([top, shifted[SUBLANES:]], axis=0)
        else:
            shifted = jnp.where(row_in_seq < s, fill, shifted)
        acc = acc + shifted * conv_w[ML_CONV - 1 - s]
    return acc


def _up_conv_qkv_kernel(x_ref, g_ref, sc_ref, sh_ref, wup_ref, c0_ref, cw_ref, cb_ref,
                        wqk_ref, wv_ref, wg_ref, bg_ref,
                        q_ref, k_ref, v_ref, xc_ref, z_ref, gates_ref, tail_ref, carry_scr,
                        *, short_seq, steps_per_seq, k_scale):
    rows = x_ref.shape[0]
    inner = z_ref.shape[1]
    hb = _rms_mod(x_ref[...], g_ref[...], _mod_rows(sc_ref, rows), _mod_rows(sh_ref, rows)).astype(BF16)
    if not short_seq:
        @pl.when(pl.program_id(0) % steps_per_seq == 0)
        def _():
            carry_scr[...] = c0_ref[0]

    width = 2 * MXU_DIM

    def conv_qkv(c0, xm):
        cs = slice(c0, c0 + width)
        conv_w = [cw_ref[i:i + 1, cs] for i in range(ML_CONV)]
        conv_b = cb_ref[:, cs]
        if short_seq:
            xm3 = xm.reshape(rows // SUBLANES, SUBLANES, width)
            t_idx = lax.broadcasted_iota(jnp.int32, xm3.shape, 1)
            conv = _conv_taps(xm3, c0_ref[:, :, cs], conv_w, conv_b, t_idx, 1).reshape(rows, width)
            tail_ref[:, :, cs] = pltpu.roll(xm3, ML_CONV - 1, 1)
        else:
            row8 = lax.broadcasted_iota(jnp.int32, (SUBLANES, width), 0)
            conv = _conv_taps(xm, carry_scr[:, cs], conv_w, conv_b, row8, 0)
            tail = pltpu.roll(xm[rows - SUBLANES:], ML_CONV - 1, 0)
            carry_scr[:, cs] = tail
            tail_ref[0, :, cs] = tail
        xc = _silu(conv)
        xc_ref[:, cs] = xc
        xcb = xc.astype(BF16)
        xmb = xm.astype(BF16)
        qkv = []
        for i in range(width // MXU_DIM):
            ls = slice(i * MXU_DIM, (i + 1) * MXU_DIM)
            ti = c0 // MXU_DIM + i
            os_ = slice(ti * MXU_DIM, (ti + 1) * MXU_DIM)
            qk = _dot(xcb[:, ls], wqk_ref[ti])
            qi = qk[:, :MXU_DIM]
            ki = qk[:, MXU_DIM:]
            vi = _dot(xmb[:, ls], wv_ref[ti])
            q_ref[:, os_] = qi.astype(q_ref.dtype)
            k_ref[:, os_] = (ki * k_scale).astype(k_ref.dtype)
            v_ref[:, os_] = vi.astype(v_ref.dtype)
            qkv.append((os_, qi.astype(BF16), ki.astype(BF16), vi.astype(BF16)))
        return qkv

    def gate_logits(acc, qkv):
        for os_, qi, ki, vi in qkv:
            acc = acc + _dot(qi, wg_ref[0, os_, :])
            acc = acc + _dot(ki, wg_ref[1, os_, :])
            acc = acc + _dot(vi, wg_ref[2, os_, :])
        return acc

    gates = bg_ref[...]
    xm_prev = None
    qkv_prev = None
    for c0 in range(0, inner + 2 * width, width):
        xm = None
        if c0 < inner:
            z_ref[:, c0:c0 + width] = _dot(hb, wup_ref[:, inner + c0:inner + c0 + width])
            xm = _dot(hb, wup_ref[:, c0:c0 + width])
        qkv = None
        if xm_prev is not None:
            qkv = conv_qkv(c0 - width, xm_prev)
        if qkv_prev is not None:
            gates = gate_logits(gates, qkv_prev)
        xm_prev, qkv_prev = xm, qkv
    lane = lax.broadcasted_iota(jnp.int32, gates.shape, 1)
    log_sig = jnp.minimum(gates, 0.0) - jnp.log1p(jnp.exp(-jnp.abs(gates)))
    gates_ref[...] = jnp.where(lane < ML_HEADS, gates, log_sig)


def _up_conv_qkv_call(x, g, sc, sh, wup, conv0, conv_w, conv_b, wqk, wv, wg, bg, b, t_len):
    n, d = x.shape
    inner = wup.shape[1] // 2
    short_seq = t_len == SUBLANES
    dh = inner // ML_HEADS
    k_scale = dh ** -0.5
    qkv_dtype = F32 if short_seq else BF16
    halo = jnp.zeros((b, SUBLANES, inner), F32)
    if conv0 is not None:
        halo = halo.at[:, :ML_CONV - 1].set(conv0)
    tm = min(256, n) if short_seq else math.gcd(512, t_len)
    seq_per_step = tm // t_len if short_seq else 1
    steps_per_seq = 1 if short_seq else t_len // tm
    sc_op, sc_spec = _mod_operand(sc, t_len, tm)
    sh_op, sh_spec = _mod_operand(sh, t_len, tm)
    seq_spec = pl.BlockSpec((seq_per_step, SUBLANES, inner), lambda i: (i // steps_per_seq, 0, 0))
    row_spec = lambda w: pl.BlockSpec((tm, w), lambda i: (i, 0))
    const2 = lambda s: pl.BlockSpec(s, lambda i: (0, 0), pipeline_mode=pl.Buffered(1))
    const3 = lambda s: pl.BlockSpec(s, lambda i: (0, 0, 0), pipeline_mode=pl.Buffered(1))
    q, k, v, xc, z, gates, tail = pl.pallas_call(
        functools.partial(_up_conv_qkv_kernel, short_seq=short_seq, steps_per_seq=steps_per_seq,
                          k_scale=k_scale),
        grid=(n // tm,),
        in_specs=[
            row_spec(d), const2((1, d)), sc_spec, sh_spec, const2(wup.shape),
            seq_spec, const2((ML_CONV, inner)), const2((1, inner)),
            const3(wqk.shape), const3(wv.shape), const3(wg.shape), const2((1, LANES)),
        ],
        out_specs=[row_spec(inner), row_spec(inner), row_spec(inner), row_spec(inner), row_spec(inner),
                   row_spec(LANES), seq_spec],
        out_shape=[
            jax.ShapeDtypeStruct((n, inner), qkv_dtype),
            jax.ShapeDtypeStruct((n, inner), qkv_dtype),
            jax.ShapeDtypeStruct((n, inner), qkv_dtype),
            jax.ShapeDtypeStruct((n, inner), F32),
            jax.ShapeDtypeStruct((n, inner), F32),
            jax.ShapeDtypeStruct((n, LANES), F32),
            jax.ShapeDtypeStruct((b, SUBLANES, inner), F32),
        ],
        scratch_shapes=[pltpu.VMEM((SUBLANES, inner), F32)],
        compiler_params=_cparams("arbitrary"),
        name="mlstm_up_conv_qkv",
    )(x, g.reshape(1, d), sc_op, sh_op, wup, halo, conv_w, conv_b.reshape(1, inner), wqk, wv, wg, bg)
    return q, k, v, xc, z, gates, tail[:, :ML_CONV - 1]


def _mlstm_kernel(*refs, has_state):
    if has_state:
        (q_ref, k_ref, v_ref, gates_ref, xc_ref, z_ref, nw_ref, sk_ref, c0_ref, n0_ref, m0_ref,
         o_ref, c_ref, n_ref, m_ref) = refs
    else:
        (q_ref, k_ref, v_ref, gates_ref, xc_ref, z_ref, nw_ref, sk_ref,
         o_ref, c_ref, n_ref, m_ref) = refs
    ch = pl.program_id(1)
    length = q_ref.shape[0]
    dh = q_ref.shape[1] // ML_HEADS

    @pl.when(ch == 0)
    def _():
        if has_state:
            c_ref[...] = c0_ref[...]
            n_ref[...] = n0_ref[...]
            m_ref[...] = m0_ref[...]
        else:
            c_ref[...] = jnp.zeros_like(c_ref)
            n_ref[...] = jnp.zeros_like(n_ref)
            m_ref[...] = jnp.zeros_like(m_ref)

    gates = gates_ref[...]
    gates_t = gates.T
    t_idx = lax.broadcasted_iota(jnp.int32, (length, length), 0)
    s_idx = lax.broadcasted_iota(jnp.int32, (length, length), 1)
    causal = s_idx <= t_idx
    m_all = m_ref[0]
    m_new = m_all
    lane = lax.broadcasted_iota(jnp.int32, m_all.shape, 1)

    def finish_head(h, s_raw, q_c, q_n, w, w_c, m_t):
        hs = slice(h * dh, (h + 1) * dh)
        s = s_raw * w
        num = w_c * q_c + _dot(s.astype(BF16), v_ref[:, hs].astype(BF16))
        den = w_c * q_n + jnp.sum(s, axis=-1, keepdims=True)
        hc = num / jnp.maximum(jnp.abs(den), jnp.exp(-m_t))
        mu = jnp.mean(hc, axis=-1, keepdims=True)
        dev = hc - mu
        var = jnp.mean(dev * dev, axis=-1, keepdims=True)
        hn = dev * lax.rsqrt(var + EPS) * nw_ref[:, hs]
        ho = (hn + sk_ref[:, hs] * xc_ref[:, hs]) * _silu(z_ref[:, hs])
        o_ref[:, hs] = ho.astype(o_ref.dtype)

    pending = None
    for h in range(ML_HEADS):
        hs = slice(h * dh, (h + 1) * dh)
        ig_col = gates[:, h:h + 1]
        lf_col = gates[:, ML_HEADS + h:ML_HEADS + h + 1]
        ig_row = gates_t[h:h + 1, :]
        lf_row = gates_t[ML_HEADS + h:ML_HEADS + h + 1, :]
        b_col = jnp.sum(jnp.where(causal, lf_row, 0.0), axis=1, keepdims=True)
        b_row = jnp.sum(jnp.where(t_idx <= s_idx, lf_col, 0.0), axis=0, keepdims=True)
        m_prev = m_all[:, h:h + 1]
        dm = jnp.where(causal, b_col - b_row + ig_row, -jnp.inf)
        a = b_col + m_prev
        m_t = jnp.maximum(a, jnp.max(dm, axis=1, keepdims=True))
        w_c = jnp.exp(a - m_t)
        w = jnp.exp(dm - m_t)
        qh = q_ref[:, hs].astype(BF16)
        kh = k_ref[:, hs]
        vh = v_ref[:, hs].astype(BF16)
        c_h = c_ref[0, h]
        n_h = n_ref[0, :, hs]
        s_raw = _dot_nt(qh, kh.astype(BF16))
        q_c = _dot(qh, c_h.astype(BF16))
        q_n = jnp.sum(qh.astype(F32) * n_h, axis=-1, keepdims=True)
        m_last = m_t[length - 1:length]
        b_last = b_col[length - 1:length]
        wl_c = jnp.exp(a[length - 1:length] - m_last)
        wl_col = jnp.exp(b_last - b_col + ig_col - m_last)
        kw = kh.astype(F32) * wl_col
        c_ref[0, h] = wl_c * c_h + _dot_tn(kw.astype(BF16), vh)
        n_ref[0, :, hs] = wl_c * n_h + jnp.sum(kw, axis=0, keepdims=True)
        m_new = jnp.where(lane == h, m_last, m_new)
        if pending is not None:
            finish_head(*pending)
        pending = (h, s_raw, q_c, q_n, w, w_c, m_t)
    finish_head(*pending)
    m_ref[0] = m_new


def _mlstm_call(q, k, v, gates, xc, z, norm_w, skip, c0, n0, m0, b, t_len):
    n, inner = q.shape
    dh = inner // ML_HEADS
    has_state = c0 is not None
    length = math.gcd(t_len, ML_CHUNK)
    nc = t_len // length
    row = lambda w: pl.BlockSpec((length, w), lambda i, c: (i * nc + c, 0))
    const = lambda w: pl.BlockSpec((1, w), lambda i, c: (0, 0))
    c_spec = pl.BlockSpec((1, ML_HEADS, dh, dh), lambda i, c: (i, 0, 0, 0))
    n_spec = pl.BlockSpec((1, 1, inner), lambda i, c: (i, 0, 0))
    m_spec = pl.BlockSpec((1, 1, LANES), lambda i, c: (i, 0, 0))
    in_specs = [row(inner), row(inner), row(inner), row(LANES), row(inner), row(inner),
                const(inner), const(inner)]
    args = [q, k, v, gates, xc, z, norm_w.reshape(1, inner), skip.reshape(1, inner)]
    if has_state:
        in_specs += [c_spec, n_spec, m_spec]
        m0_pad = jnp.zeros((b, 1, LANES), F32).at[:, 0, :ML_HEADS].set(m0)
        args += [c0, n0.reshape(b, 1, inner), m0_pad]
    ho, c_t, n_t, m_t = pl.pallas_call(
        functools.partial(_mlstm_kernel, has_state=has_state),
        grid=(b, nc),
        in_specs=in_specs,
        out_specs=[row(inner), c_spec, n_spec, m_spec],
        out_shape=[
            jax.ShapeDtypeStruct((n, inner), BF16 if length % 16 == 0 else F32),
            jax.ShapeDtypeStruct((b, ML_HEADS, dh, dh), F32),
            jax.ShapeDtypeStruct((b, 1, inner), F32),
            jax.ShapeDtypeStruct((b, 1, LANES), F32),
        ],
        compiler_params=_cparams("parallel", "arbitrary"),
        name="mlstm_scan",
    )(*args)
    return ho, c_t, n_t.reshape(b, ML_HEADS, dh), m_t[:, 0, :ML_HEADS]


def _block_diag_tiles(w):
    n_blk, blk, _ = w.shape
    per = MXU_DIM // blk
    rows = w.reshape(n_blk // per, per, blk, blk).transpose(0, 1, 3, 2).reshape(n_blk // per, MXU_DIM, blk)
    r_blk = np.arange(MXU_DIM)[:, None] // blk
    c_blk = np.arange(MXU_DIM)[None, :] // blk
    same_block = jnp.asarray((r_blk == c_blk).astype(np.float32))
    return jnp.tile(rows, (1, 1, per)) * same_block


def _prep_weights(p):
    w = {}
    w['hg_win'] = p['hg_win'].astype(BF16)
    w['hg_wo'] = p['hg_wo'].astype(BF16)
    w['ml_wup'] = p['ml_wup'].astype(BF16)
    w['ml_wdown'] = p['ml_wdown'].astype(BF16)
    dep, n_exp, d_model, dff = p['moe_w1'].shape
    n_stack = dep * n_exp // MOE_EPG
    w['moe_w1'] = p['moe_w1'].reshape(n_stack, MOE_EPG, d_model, dff)
    w['moe_w3'] = p['moe_w3'].reshape(n_stack, MOE_EPG, d_model, dff)
    w['moe_w2'] = p['moe_w2'].reshape(n_stack, MOE_EPG, dff, d_model)
    depth, d, g = p['moe_wrg'].shape
    n_exp = g * p['moe_wre'].shape[-1]
    wr = jnp.zeros((depth, d, LANES), F32)
    wr = wr.at[:, :, :g].set(p['moe_wrg']).at[:, :, g:g + n_exp].set(p['moe_wre'].reshape(depth, d, n_exp))
    w['moe_wr'] = wr
    br = jnp.zeros((depth, 1, LANES), F32)
    br = br.at[:, 0, :g].set(p['moe_brg']).at[:, 0, g:g + n_exp].set(p['moe_bre'].reshape(depth, n_exp))
    w['moe_br'] = br
    n_b = p['ml_wq'].shape[0]
    wq = jnp.stack([_block_diag_tiles(p['ml_wq'][j]) for j in range(n_b)])
    wk = jnp.stack([_block_diag_tiles(p['ml_wk'][j]) for j in range(n_b)])
    wv = jnp.stack([_block_diag_tiles(p['ml_wv'][j]) for j in range(n_b)])
    w['ml_wqk'] = jnp.concatenate([wq, wk], axis=-1).astype(BF16)
    w['ml_wv'] = wv.astype(BF16)
    inner = p['ml_conv_b'].shape[-1]
    wg = jnp.zeros((n_b, 3 * inner, LANES), F32)
    wg = wg.at[:, :, :ML_HEADS].set(p['ml_wig']).at[:, :, ML_HEADS:2 * ML_HEADS].set(p['ml_wfg'])
    w['ml_wg'] = wg.reshape(n_b, 3, inner, LANES).astype(BF16)
    bg = jnp.zeros((n_b, 1, LANES), F32)
    bg = bg.at[:, 0, :ML_HEADS].set(p['ml_big']).at[:, 0, ML_HEADS:2 * ML_HEADS].set(p['ml_bfg'])
    w['ml_bg'] = bg
    return w


def _lb_kernel(lb_ref, o_ref):
    x = lb_ref[...]
    mx = jnp.max(x, axis=0, keepdims=True)
    ex = jnp.exp(x - mx)
    sm = ex / jnp.sum(ex, axis=0, keepdims=True)
    rows = []
    run = jnp.zeros_like(sm[0:1])
    for i in range(x.shape[0]):
        run = run + sm[i:i + 1]
        rows.append(run)
    o_ref[...] = jnp.concatenate(rows, axis=0)


def _lb_call(hg_lb):
    return pl.pallas_call(
        _lb_kernel,
        out_shape=jax.ShapeDtypeStruct(hg_lb.shape, F32),
        name="hgrn_lower_bound",
    )(hg_lb)


def _trunk(x3, mods, s_hg, s_c, s_n, s_m, s_conv, p, w, lb_all):
    b, t_len, d = x3.shape
    n = b * t_len
    x = x3.reshape(n, d)
    depth = p['norm_g'].shape[0]
    new_hg, new_c, new_n, new_m, new_conv = [], [], [], [], []
    for l in range(depth):
        sh1, sc1, g1, sh2, sc2, g2 = [mods[l][:, i * d:(i + 1) * d] for i in range(6)]
        if l % 2 == 0:
            a = l // 2
            proj = _norm_mm_call(x, p['norm_g'][l, 0], sc1, sh1, w['hg_win'][a], t_len, "hgrn_in_proj")
            o, s_t = _gla_call(proj, lb_all[l], p['hg_norm'][a], None if s_hg is None else s_hg[a], b, t_len)
            new_hg.append(s_t)
            mix = (o, w['hg_wo'][a], g1)
        else:
            j = l // 2
            q, k, v, xc, z, gates, conv_tail = _up_conv_qkv_call(
                x, p['norm_g'][l, 0], sc1, sh1, w['ml_wup'][j],
                None if s_conv is None else s_conv[j], p['ml_conv_w'][j], p['ml_conv_b'][j],
                w['ml_wqk'][j], w['ml_wv'][j], w['ml_wg'][j], w['ml_bg'][j], b, t_len)
            ho, c_t, n_t, m_t = _mlstm_call(
                q, k, v, gates, xc, z, p['ml_norm'][j], p['ml_skip'][j],
                None if s_c is None else s_c[j], None if s_n is None else s_n[j],
                None if s_m is None else s_m[j], b, t_len)
            new_c.append(c_t)
            new_n.append(n_t)
            new_m.append(m_t)
            new_conv.append(conv_tail)
            mix = (ho.astype(BF16), w['ml_wdown'][j], g1)
        x = _moe_call(mix, x, p['norm_g'][l, 1], sc2, sh2, g2, w['moe_wr'][l], w['moe_br'][l],
                      w['moe_w1'], w['moe_w3'], w['moe_w2'], l,
                      p['final_g'] if l == depth - 1 else None, t_len, "moe_layer%d" % l)
    return (x.reshape(b, t_len, d), jnp.stack(new_hg), jnp.stack(new_c), jnp.stack(new_n),
            jnp.stack(new_m), jnp.stack(new_conv))


def kernel(x_prompt, x_sample, c_prompt, c_sample, state_hgrn, state_mlstm_c, state_mlstm_n, state_mlstm_m, state_conv, w_ada, b_ada, norm_g, final_g, hg_win, hg_wo, hg_norm, hg_lb, ml_wup, ml_conv_w, ml_conv_b, ml_wq, ml_wk, ml_wv, ml_wig, ml_big, ml_wfg, ml_bfg, ml_norm, ml_skip, ml_wdown, moe_wrg, moe_brg, moe_wre, moe_bre, moe_w1, moe_w3, moe_w2):
    p = dict(w_ada=w_ada, b_ada=b_ada, norm_g=norm_g, final_g=final_g,
             hg_win=hg_win, hg_wo=hg_wo, hg_norm=hg_norm, hg_lb=hg_lb,
             ml_wup=ml_wup, ml_conv_w=ml_conv_w, ml_conv_b=ml_conv_b, ml_wq=ml_wq, ml_wk=ml_wk, ml_wv=ml_wv,
             ml_wig=ml_wig, ml_big=ml_big, ml_wfg=ml_wfg, ml_bfg=ml_bfg, ml_norm=ml_norm, ml_skip=ml_skip,
             ml_wdown=ml_wdown, moe_wrg=moe_wrg, moe_brg=moe_brg, moe_wre=moe_wre, moe_bre=moe_bre,
             moe_w1=moe_w1, moe_w3=moe_w3, moe_w2=moe_w2)
    w = _prep_weights(p)
    lb_all = _lb_call(hg_lb)
    bp = x_prompt.shape[0]
    c_all = jnp.concatenate([c_prompt, c_sample], axis=0)
    mod_all = _ada_call(c_all, w_ada, b_ada)
    mods_p = [mod_all[l, :bp] for l in range(mod_all.shape[0])]
    mods_s = [mod_all[l, bp:] for l in range(mod_all.shape[0])]
    y_p, hg_p, mc_p, mn_p, mm_p, conv_p = _trunk(x_prompt, mods_p, None, None, None, None, None, p, w, lb_all)
    y_s, hg_s, mc_s, mn_s, mm_s, conv_s = _trunk(x_sample, mods_s, state_hgrn, state_mlstm_c, state_mlstm_n,
                                                 state_mlstm_m, state_conv, p, w, lb_all)
    return (y_p, y_s, hg_p, mc_p, mn_p, mm_p, conv_p, hg_s, mc_s, mn_s, mm_s, conv_s)
```

```python
import functools
import math

import numpy as np
import jax
import jax.numpy as jnp
from jax import lax
from jax.experimental import pallas as pl
from jax.experimental.pallas import tpu as pltpu

F32 = jnp.float32
BF16 = jnp.bfloat16
EPS = 1e-6

HG_DK = 128
ML_HEADS = 4
ML_CONV = 4
ML_QKV_BLOCK = 4
MOE_GROUPS = 4
MOE_EPG = 4
CHUNK = 64
GLA_CHUNKS_PER_STEP = 8
ML_CHUNK = 512

LANES = 128
SUBLANES = 8
MXU_DIM = 256
VMEM_LIMIT_BYTES = 56 * 1024 * 1024


def _cparams(*sem):
    return pltpu.CompilerParams(dimension_semantics=sem, vmem_limit_bytes=VMEM_LIMIT_BYTES)


def _silu(x):
    return x * jax.nn.sigmoid(x)


def _dot(a, b):
    return jnp.dot(a, b, preferred_element_type=F32)


def _dot_nt(a, b):
    return lax.dot_general(a, b, (((1,), (1,)), ((), ())), preferred_element_type=F32)


def _dot_tn(a, b):
    return lax.dot_general(a, b, (((0,), (0,)), ((), ())), preferred_element_type=F32)


def _rms_mod(x, g, sc, sh):
    ms = jnp.mean(x * x, axis=-1, keepdims=True)
    h = x * lax.rsqrt(ms + EPS) * g
    return h * (1.0 + sc) + sh


def _ada_kernel(c_ref, w_ref, b_ref, o_ref):
    cm = _silu(c_ref[...]).astype(BF16)
    o_ref[0] = _dot(cm, w_ref[0].astype(BF16)) + b_ref[0]


def _ada_call(c_all, w_ada, b_ada):
    depth, d, n_out = w_ada.shape
    m = c_all.shape[0]
    tn = 512
    return pl.pallas_call(
        _ada_kernel,
        grid=(depth, n_out // tn),
        in_specs=[
            pl.BlockSpec((m, d), lambda l, j: (0, 0)),
            pl.BlockSpec((1, d, tn), lambda l, j: (l, 0, j)),
            pl.BlockSpec((1, 1, tn), lambda l, j: (l, 0, j)),
        ],
        out_specs=pl.BlockSpec((1, m, tn), lambda l, j: (l, 0, j)),
        out_shape=jax.ShapeDtypeStruct((depth, m, n_out), F32),
        compiler_params=_cparams("parallel", "parallel"),
        name="ada_mod",
    )(c_all, w_ada, b_ada.reshape(depth, 1, n_out))


def _mod_operand(m, t_len, tm):
    b, d = m.shape
    if t_len % tm == 0:
        per_b = t_len // tm
        return m.reshape(b, 1, d), pl.BlockSpec((1, 1, d), lambda i, *_: (i // per_b, 0, 0))
    assert tm % t_len == 0
    nb = tm // t_len
    return m.reshape(b // nb, nb, 1, d), pl.BlockSpec((1, nb, 1, d), lambda i, *_: (i, 0, 0, 0))


def _mod_rows(ref, rows):
    v = ref[0]
    if v.ndim == 2:
        return v
    nb, _, d = v.shape
    return jnp.broadcast_to(v, (nb, rows // nb, d)).reshape(rows, d)


def _row_tile(n, t_len, target):
    tm = min(target, n)
    while n % tm or (t_len % tm and tm % t_len):
        tm //= 2
    return tm


def _norm_mm_kernel(x_ref, g_ref, sc_ref, sh_ref, w_ref, o_ref, *, col_chunk):
    rows = x_ref.shape[0]
    hb = _rms_mod(x_ref[...], g_ref[...], _mod_rows(sc_ref, rows), _mod_rows(sh_ref, rows)).astype(BF16)
    for c0 in range(0, o_ref.shape[1], col_chunk):
        o_ref[:, c0:c0 + col_chunk] = _dot(hb, w_ref[:, c0:c0 + col_chunk])


def _norm_mm_call(x, g, sc, sh, w, t_len, name):
    n, d = x.shape
    n_out = w.shape[1]
    tm = _row_tile(n, t_len, 1024 if t_len % 1024 == 0 else 256)
    sc_op, sc_spec = _mod_operand(sc, t_len, tm)
    sh_op, sh_spec = _mod_operand(sh, t_len, tm)
    return pl.pallas_call(
        functools.partial(_norm_mm_kernel, col_chunk=512),
        grid=(n // tm,),
        in_specs=[
            pl.BlockSpec((tm, d), lambda i: (i, 0)),
            pl.BlockSpec((1, d), lambda i: (0, 0)),
            sc_spec, sh_spec,
            pl.BlockSpec((d, n_out), lambda i: (0, 0), pipeline_mode=pl.Buffered(1)),
        ],
        out_specs=pl.BlockSpec((tm, n_out), lambda i: (i, 0)),
        out_shape=jax.ShapeDtypeStruct((n, n_out), F32),
        compiler_params=_cparams("parallel"),
        name=name,
    )(x, g.reshape(1, d), sc_op, sh_op, w)


def _gla_tables(t_sub, nseq):
    r = t_sub * nseq
    levels = []
    m = t_sub // 2
    while m >= 1:
        levels.append(m)
        m //= 2
    n_lev = len(levels)
    tril = np.zeros((r, r), np.float32)
    mask = np.zeros((n_lev + 1, r, r), np.float32)
    for li, m in enumerate(levels):
        for row in range(r):
            blk = (row // (2 * m)) * 2 * m
            if row - blk >= m:
                mask[li, row, blk:blk + m] = 1.0
    for row in range(r):
        s0 = (row // t_sub) * t_sub
        tril[row, s0:row + 1] = 1.0
        mask[n_lev, row, row] = 1.0
    return tril, mask, levels


def _bcast_block_row(b, block, row_in_block):
    parts = [jnp.broadcast_to(b[s + row_in_block:s + row_in_block + 1, :], (block, b.shape[1]))
             for s in range(0, b.shape[0], block)]
    return parts[0] if len(parts) == 1 else jnp.concatenate(parts, axis=0)


def _level_decay(b, m):
    r = b.shape[0]
    pos = lax.broadcasted_iota(jnp.int32, b.shape, 0) & (2 * m - 1)
    if 2 * m >= SUBLANES:
        b_mid = _bcast_block_row(b, 2 * m, m - 1)
    else:
        b_mid = b
        for p in range(2 * m):
            if p != m - 1:
                b_mid = jnp.where(pos == p, pltpu.roll(b, (p - (m - 1)) % r, 0), b_mid)
    return jnp.exp2(jnp.where(pos >= m, b - b_mid, b_mid - b))


def _gla_kernel(*refs, t_sub, nseq, n_ch, levels, n_heads, has_state, steps_per_seq):
    if has_state:
        proj_ref, lb_ref, gn_ref, tril_ref, mask_ref, s0_ref, o_ref, sout_ref, st_ref = refs
    else:
        proj_ref, lb_ref, gn_ref, tril_ref, mask_ref, o_ref, sout_ref, st_ref = refs
    r = t_sub * nseq
    n_lev = len(levels)
    dk = HG_DK
    hk = n_heads * dk
    c = lax.rem(pl.program_id(0), steps_per_seq)

    @pl.when(c == 0)
    def _():
        if has_state:
            for j in range(nseq):
                for h in range(n_heads):
                    st_ref[j, h] = s0_ref[j, h].T
        else:
            st_ref[...] = jnp.zeros_like(st_ref)

    gn = gn_ref[...]
    tril = tril_ref[...]

    def finish_head(rows, h, a, qh, kh, e_cum_h, e_end_h, d_last):
        hs = slice(h * dk, (h + 1) * dk)
        vb = proj_ref[rows, 2 * hk + h * dk:2 * hk + (h + 1) * dk].astype(BF16)
        o_intra = _dot(a.astype(BF16), vb)
        qd = qh * e_cum_h
        kd = kh * e_end_h
        o_parts = []
        for j in range(nseq):
            rs = slice(j * t_sub, (j + 1) * t_sub)
            st = st_ref[j, h]
            o_parts.append(_dot_nt(qd[rs], st.astype(BF16)))
            st_ref[j, h] = st * d_last[j] + _dot_tn(vb[rs], kd[rs])
        o_inter = o_parts[0] if nseq == 1 else jnp.concatenate(o_parts, axis=0)
        o = o_intra + o_inter
        o = o * lax.rsqrt(jnp.mean(o * o, axis=-1, keepdims=True) + EPS) * gn
        zg = proj_ref[rows, 3 * hk + h * dk:3 * hk + (h + 1) * dk]
        o_ref[rows, hs] = (o * _silu(zg)).astype(BF16)

    pending = None
    head_group = n_heads if nseq == 1 else 1
    for ci in range(n_ch):
        rows = slice(ci * r, (ci + 1) * r)
        for g0 in range(0, n_heads, head_group):
            gw = head_group * dk
            gs = slice(g0 * dk, g0 * dk + gw)
            lb = lb_ref[:, gs]
            zq = proj_ref[rows, g0 * dk:g0 * dk + gw]
            zf = proj_ref[rows, hk + g0 * dk:hk + g0 * dk + gw]
            f = lb + (1.0 - lb) * jax.nn.sigmoid(zf)
            lf = jnp.log2(f)
            q = _silu(zq).astype(BF16)
            k = (1.0 - f).astype(BF16)

            p0 = lf.astype(BF16)
            r1 = lf - p0.astype(F32)
            p1 = r1.astype(BF16)
            p2 = (r1 - p1.astype(F32)).astype(BF16)
            b = _dot(tril, p0) + _dot(tril, p1) + _dot(tril, p2)
            e_cum = jnp.exp2(b)
            e_cum_b = e_cum.astype(BF16)
            e_end_b = jnp.exp2(_bcast_block_row(b, t_sub, t_sub - 1) - b).astype(BF16)
            zs = [_level_decay(b, m).astype(BF16) for m in levels]

            for hh in range(head_group):
                h = g0 + hh
                ls = slice(hh * dk, (hh + 1) * dk)
                qh = q[:, ls]
                kh = k[:, ls]
                a = _dot_nt(qh, kh) * mask_ref[n_lev]
                for li in range(n_lev):
                    z = zs[li][:, ls]
                    a = a + _dot_nt(qh * z, kh * z) * mask_ref[li]
                item = (rows, h, a, qh, kh, e_cum_b[:, ls], e_end_b[:, ls],
                        [e_cum[(j + 1) * t_sub - 1:(j + 1) * t_sub, ls] for j in range(nseq)])
                if nseq > 1:
                    finish_head(*item)
                else:
                    if pending is not None:
                        finish_head(*pending)
                    pending = item
    if pending is not None:
        finish_head(*pending)

    @pl.when(c == steps_per_seq - 1)
    def _():
        for j in range(nseq):
            for h in range(n_heads):
                sout_ref[j, h] = st_ref[j, h].T


def _gla_call(proj, lb, gn, s0, b, t_len):
    n = proj.shape[0]
    hk = lb.shape[-1]
    n_heads = hk // HG_DK
    has_state = s0 is not None
    if t_len % CHUNK == 0:
        t_sub, nseq = CHUNK, 1
    else:
        t_sub, nseq = t_len, CHUNK // t_len
        assert t_sub * nseq == CHUNK and b % nseq == 0
    r = t_sub * nseq
    tril, mask, levels = _gla_tables(t_sub, nseq)
    n_ch = math.gcd(GLA_CHUNKS_PER_STEP, t_len // t_sub)
    rows = n_ch * r
    steps_per_seq = t_len // (t_sub * n_ch)
    n_steps = n // rows
    const2 = lambda s: pl.BlockSpec(s, lambda i: (0, 0))
    in_specs = [pl.BlockSpec((rows, proj.shape[1]), lambda i: (i, 0)),
                const2((1, hk)), const2((1, HG_DK)), const2(tril.shape),
                pl.BlockSpec(mask.shape, lambda i: (0, 0, 0))]
    args = [proj, lb.reshape(1, hk), gn.reshape(1, HG_DK), jnp.asarray(tril, BF16), jnp.asarray(mask, F32)]
    s_spec = pl.BlockSpec((nseq, n_heads, HG_DK, HG_DK), lambda i: (i // steps_per_seq, 0, 0, 0))
    if has_state:
        in_specs.append(s_spec)
        args.append(s0)
    o, s_out = pl.pallas_call(
        functools.partial(_gla_kernel, t_sub=t_sub, nseq=nseq, n_ch=n_ch, levels=tuple(levels),
                          n_heads=n_heads, has_state=has_state, steps_per_seq=steps_per_seq),
        grid=(n_steps,),
        in_specs=in_specs,
        out_specs=[
            pl.BlockSpec((rows, hk), lambda i: (i, 0)),
            s_spec,
        ],
        out_shape=[
            jax.ShapeDtypeStruct((n, hk), BF16),
            jax.ShapeDtypeStruct((b, n_heads, HG_DK, HG_DK), F32),
        ],
        scratch_shapes=[pltpu.VMEM((nseq, n_heads, HG_DK, HG_DK), F32)],
        compiler_params=_cparams("arbitrary"),
        name="gla_scan",
    )(*args)
    return o, s_out


def _route_kernel(a_ref, wmix_ref, gmix_ref, x_ref, g_ref, sc_ref, sh_ref, wr_ref, br_ref, tril_ref,
                  xnew_ref, hx_ref, meta_ref, cnt_ref, *, n_groups, epg):
    rows, d = x_ref.shape

    @pl.when(pl.program_id(0) == 0)
    def _():
        cnt_ref[...] = jnp.zeros_like(cnt_ref)

    halves = [slice(k * tril_ref.shape[0], (k + 1) * tril_ref.shape[0])
              for k in range(rows // tril_ref.shape[0])]
    part = lambda v, rs: v if v.shape[0] == 1 else v[rs]
    gmix, scale, shift = (_mod_rows(r, rows) for r in (gmix_ref, sc_ref, sh_ref))
    projs = [_dot(a_ref[rs, :], wmix_ref[...]) for rs in halves]

    wr = wr_ref[...]
    w_hi = wr.astype(BF16)
    w_lo = (wr - w_hi.astype(F32)).astype(BF16)
    logits = []
    for rs, proj in zip(halves, projs):
        x = x_ref[rs, :] + part(gmix, rs) * proj
        xnew_ref[rs, :] = x
        h = _rms_mod(x, g_ref[...], part(scale, rs), part(shift, rs))
        hx_ref[rs, :d] = h
        h_hi = h.astype(BF16)
        h_lo = (h - h_hi.astype(F32)).astype(BF16)
        logits.append(_dot(h_hi, w_hi) + _dot(h_hi, w_lo) + _dot(h_lo, w_hi) + br_ref[...])
    for rs, logit in zip(halves, logits):
        _route_rows(rs, logit, tril_ref, hx_ref, meta_ref, cnt_ref, d, n_groups, epg)


def _route_rows(rs, logit, tril_ref, hx_ref, meta_ref, cnt_ref, d, n_groups, epg):
    neg = -jnp.inf
    far = float(LANES)
    lane = lax.broadcasted_iota(jnp.int32, logit.shape, 1).astype(F32)
    gm = lane < n_groups
    gmax = jnp.max(jnp.where(gm, logit, neg), axis=-1, keepdims=True)
    gstar = jnp.min(jnp.where(gm, jnp.where(logit == gmax, lane, far), far), axis=-1, keepdims=True)
    psum = jnp.sum(jnp.where(gm, jnp.exp(logit - gmax), 0.0), axis=-1, keepdims=True)
    pstar = 1.0 / psum
    lo = n_groups + gstar * epg
    em = jnp.where(lane >= lo, jnp.where(lane < lo + epg, 1.0, 0.0), 0.0)
    l1 = jnp.where(em > 0.0, logit, neg)
    v1 = jnp.max(l1, axis=-1, keepdims=True)
    i1 = jnp.min(jnp.where(l1 == v1, lane, far), axis=-1, keepdims=True)
    l2 = jnp.where(lane == i1, neg, l1)
    v2 = jnp.max(l2, axis=-1, keepdims=True)
    i2 = jnp.min(jnp.where(l2 == v2, lane, far), axis=-1, keepdims=True)
    e2 = jnp.exp(v2 - v1)
    wt1 = pstar / (1.0 + e2)
    wt2 = pstar * e2 / (1.0 + e2)
    onehot = jnp.where(lane == gstar, 1.0, 0.0)
    within = _dot(tril_ref[...], onehot.astype(BF16))
    carry = cnt_ref[...]
    rank = jnp.sum(onehot * (within + carry), axis=-1, keepdims=True)
    cnt_ref[...] = carry + jnp.sum(onehot, axis=0, keepdims=True)
    meta = (jnp.where(lane == i1 - lo, wt1, 0.0) + jnp.where(lane == i2 - lo, wt2, 0.0)
            + jnp.where(lane == epg, gstar, 0.0) + jnp.where(lane == epg + 1, rank, 0.0))
    hx_ref[rs, d:] = meta
    meta_ref[:, rs] = meta.T[0:SUBLANES]


def _route_call(a, w_mix, gate_mix, x, g, sc, sh, wr, br, t_len, name):
    n, d = x.shape
    kdim = a.shape[1]
    tm = _row_tile(n, t_len, 1024 if t_len % 1024 == 0 else 256)
    gm_op, gm_spec = _mod_operand(gate_mix, t_len, tm)
    sc_op, sc_spec = _mod_operand(sc, t_len, tm)
    sh_op, sh_spec = _mod_operand(sh, t_len, tm)
    half = tm // 2 if tm % (2 * LANES) == 0 else tm
    tril = jnp.asarray(np.tril(np.ones((half, half), np.float32), -1), BF16)
    const = lambda s: pl.BlockSpec(s, lambda i: (0, 0), pipeline_mode=pl.Buffered(1))
    return pl.pallas_call(
        functools.partial(_route_kernel, n_groups=MOE_GROUPS, epg=MOE_EPG),
        grid=(n // tm,),
        in_specs=[
            pl.BlockSpec((tm, kdim), lambda i: (i, 0)),
            const((kdim, d)), gm_spec,
            pl.BlockSpec((tm, d), lambda i: (i, 0)),
            const((1, d)), sc_spec, sh_spec,
            const((d, LANES)), const((1, LANES)), const((half, half)),
        ],
        out_specs=[
            pl.BlockSpec((tm, d), lambda i: (i, 0)),
            pl.BlockSpec((tm, d + LANES), lambda i: (i, 0)),
            pl.BlockSpec((SUBLANES, tm), lambda i: (0, i)),
            pl.BlockSpec((1, LANES), lambda i: (0, 0)),
        ],
        out_shape=[
            jax.ShapeDtypeStruct((n, d), F32),
            jax.ShapeDtypeStruct((n, d + LANES), F32),
            jax.ShapeDtypeStruct((SUBLANES, n), F32),
            jax.ShapeDtypeStruct((1, LANES), F32),
        ],
        compiler_params=_cparams("arbitrary"),
        name=name + "_route",
    )(a, w_mix, gm_op, x, g.reshape(1, d), sc_op, sh_op, wr, br, tril)


def _row_gather(idx_ref, src_hbm, buf, sem, tile, slot, start, unrolled=False):
    groups = buf.shape[1]
    base = tile * (groups * SUBLANES)

    def run(s):
        def body(i, carry):
            for u in range(SUBLANES):
                idx = idx_ref[base + i * SUBLANES + u]
                cp = pltpu.make_async_copy(src_hbm.at[pl.ds(idx, 1)],
                                           buf.at[s, i, pl.ds(u, 1)], sem.at[s])
                if start:
                    cp.start(priority=u % 2)
                else:
                    cp.wait()
            return carry

        if unrolled:
            for i in range(groups):
                body(i, 0)
        else:
            lax.fori_loop(0, groups, body, 0)

    for s in range(2):
        @pl.when(slot == s)
        def _():
            run(s)


def _expert_kernel(pad_ref, dest_ref, tgrp_ref, tval_ref, hx_hbm, w1_ref, w3_ref, w2_ref, y_ref,
                   hbuf, sem, wb1, wb3, wb2, src_ref, *, epg, d):
    j = pl.program_id(0)
    rows = y_ref.shape[0]

    @pl.when(j == 0)
    def _():
        _invperm(pad_ref, dest_ref, src_ref)

    @pl.when(jnp.logical_or(j == 0, tgrp_ref[j] != tgrp_ref[jnp.maximum(j - 1, 0)]))
    def _():
        for e in range(epg):
            wb1[e] = w1_ref[0, e].astype(BF16)
            wb3[e] = w3_ref[0, e].astype(BF16)
            wb2[e] = w2_ref[0, e].astype(BF16)

    @pl.when(j == 0)
    def _():
        @pl.when(tval_ref[0] == 1)
        def _():
            _row_gather(src_ref, hx_hbm, hbuf, sem, 0, 0, True)

    @pl.when(j + 1 < pl.num_programs(0))
    def _():
        @pl.when(tval_ref[j + 1] == 1)
        def _():
            _row_gather(src_ref, hx_hbm, hbuf, sem, j + 1, (j + 1) % 2, True, unrolled=True)

    @pl.when(tval_ref[j] == 1)
    def _():
        slot = j % 2
        _row_gather(src_ref, hx_hbm, hbuf, sem, j, slot, False)
        tile = hbuf[slot].reshape(rows, hbuf.shape[-1])
        hb = tile[:, :d].astype(BF16)
        hids = []
        for e in range(epg):
            a = _dot(hb, wb1[e])
            b = _dot(hb, wb3[e])
            hids.append((_silu(a) * b * tile[:, d + e:d + e + 1]).astype(BF16))
        hid = jnp.concatenate(hids, axis=1)
        y_ref[...] = _dot(hid, wb2[...].reshape(hid.shape[1], d))

    @pl.when(tval_ref[j] == 0)
    def _():
        y_ref[...] = jnp.zeros_like(y_ref)


def _combine_kernel(*refs, final):
    if final:
        dest_ref, ys_hbm, x_ref, gate_ref, fg_ref, o_ref, gbuf, sem = refs
    else:
        dest_ref, ys_hbm, x_ref, gate_ref, o_ref, gbuf, sem = refs
    i = pl.program_id(0)
    rows = x_ref.shape[0]

    @pl.when(i == 0)
    def _():
        _row_gather(dest_ref, ys_hbm, gbuf, sem, 0, 0, True)

    @pl.when(i + 1 < pl.num_programs(0))
    def _():
        _row_gather(dest_ref, ys_hbm, gbuf, sem, i + 1, (i + 1) % 2, True, unrolled=True)

    slot = i % 2
    _row_gather(dest_ref, ys_hbm, gbuf, sem, i, slot, False)
    y = x_ref[...] + _mod_rows(gate_ref, rows) * gbuf[slot].reshape(rows, gbuf.shape[-1])
    if final:
        y = y * lax.rsqrt(jnp.mean(y * y, axis=-1, keepdims=True) + EPS) * fg_ref[...]
    o_ref[...] = y


def _invperm(pad_ref, dest_ref, src_ref):
    def clear(i, carry):
        src_ref[i] = 0
        return carry

    def place(i, carry):
        src_ref[dest_ref[i]] = i
        return carry

    for rng in range(pad_ref.shape[0] // 2):
        lax.fori_loop(pad_ref[2 * rng], pad_ref[2 * rng + 1], clear, 0)
    lax.fori_loop(0, dest_ref.shape[0], place, 0, unroll=8)


def _moe_call(mix, x, g, sc, sh, gate, wr, br, w1, w3, w2, layer, final_g, t_len, name):
    n, d = x.shape
    epg, dff = w1.shape[1], w1.shape[3]
    n_grp = MOE_GROUPS
    x, hx, meta, cnt = _route_call(*mix, x, g, sc, sh, wr, br, t_len, name)

    tm2 = 512 if n >= 8192 else 128
    grp = meta[epg].astype(jnp.int32)
    rank = meta[epg + 1].astype(jnp.int32)
    counts = cnt[0, :n_grp].astype(jnp.int32)
    n_tiles_g = (counts + tm2 - 1) // tm2
    tile_end = jnp.cumsum(n_tiles_g)
    tile_start = tile_end - n_tiles_g
    dest = rank
    for gi in range(n_grp):
        dest = dest + jnp.where(grp == gi, tile_start[gi] * tm2, 0)
    n_tiles = n // tm2 + n_grp
    pad_lo = jnp.concatenate([tile_start * tm2 + counts, tile_end[-1:] * tm2])
    pad_hi = jnp.concatenate([tile_end * tm2, jnp.full((1,), n_tiles * tm2, jnp.int32)])
    pad = jnp.stack([pad_lo, pad_hi], axis=1).reshape(-1)
    jt = jnp.arange(n_tiles, dtype=jnp.int32)
    tval = (jt < tile_end[-1]).astype(jnp.int32)
    tgrp = jnp.sum((jt[:, None] >= tile_end[None, :]).astype(jnp.int32), axis=1)
    last_grp = jnp.sum((tile_end[-1] - 1 >= tile_end).astype(jnp.int32))
    tgrp = layer * n_grp + jnp.minimum(jnp.where(tval == 1, tgrp, last_grp), n_grp - 1)

    wspec = lambda shape: pl.BlockSpec(shape, lambda j, pad, dest, tgrp, tval: (tgrp[j], 0, 0, 0))
    ys = pl.pallas_call(
        functools.partial(_expert_kernel, epg=epg, d=d),
        grid_spec=pltpu.PrefetchScalarGridSpec(
            num_scalar_prefetch=4,
            grid=(n_tiles,),
            in_specs=[
                pl.BlockSpec(memory_space=pl.ANY),
                wspec((1, epg, d, dff)), wspec((1, epg, d, dff)), wspec((1, epg, dff, d)),
            ],
            out_specs=pl.BlockSpec((tm2, d), lambda j, pad, dest, tgrp, tval: (j, 0)),
            scratch_shapes=[pltpu.VMEM((2, tm2 // SUBLANES, SUBLANES, d + LANES), F32),
                            pltpu.SemaphoreType.DMA((2,)),
                            pltpu.VMEM((epg, d, dff), BF16), pltpu.VMEM((epg, d, dff), BF16),
                            pltpu.VMEM((epg, dff, d), BF16),
                            pltpu.SMEM((n_tiles * tm2,), jnp.int32)],
        ),
        out_shape=jax.ShapeDtypeStruct((n_tiles * tm2, d), F32),
        compiler_params=_cparams("arbitrary"),
        name=name + "_experts",
    )(pad, dest, tgrp, tval, hx, w1, w3, w2)

    tm = _row_tile(n, t_len, 1024 if t_len % 1024 == 0 else 256)
    gate_op, gate_spec = _mod_operand(gate, t_len, tm)
    final = final_g is not None
    in_specs = [
        pl.BlockSpec(memory_space=pl.ANY),
        pl.BlockSpec((tm, d), lambda i, dest: (i, 0)),
        gate_spec,
    ]
    args = [dest, ys, x, gate_op]
    if final:
        in_specs.append(pl.BlockSpec((1, d), lambda i, dest: (0, 0)))
        args.append(final_g.reshape(1, d))
    return pl.pallas_call(
        functools.partial(_combine_kernel, final=final),
        grid_spec=pltpu.PrefetchScalarGridSpec(
            num_scalar_prefetch=1,
            grid=(n // tm,),
            in_specs=in_specs,
            out_specs=pl.BlockSpec((tm, d), lambda i, dest: (i, 0)),
            scratch_shapes=[pltpu.VMEM((2, tm // SUBLANES, SUBLANES, d), F32),
                            pltpu.SemaphoreType.DMA((2,))],
        ),
        out_shape=jax.ShapeDtypeStruct((n, d), F32),
        compiler_params=_cparams("arbitrary"),
        name=name + "_combine",
    )(*args)


def _conv_taps(xm, halo, conv_w, conv_b, row_in_seq, axis):
    acc = conv_b + xm * conv_w[ML_CONV - 1]
    for s in range(1, ML_CONV):
        shifted = pltpu.roll(xm, s, axis)
        fill = pltpu.roll(halo, (s + SUBLANES - (ML_CONV - 1)) % SUBLANES, axis)
        if axis == 0:
            top = jnp.where(row_in_seq < s, fill, shifted[0:SUBLANES])
            shifted = jnp.concatenate([top, shifted[SUBLANES:]], axis=0)
        else:
            shifted = jnp.where(row_in_seq < s, fill, shifted)
        acc = acc + shifted * conv_w[ML_CONV - 1 - s]
    return acc


def _up_conv_qkv_kernel(x_ref, g_ref, sc_ref, sh_ref, wup_ref, c0_ref, cw_ref, cb_ref,
                        wqk_ref, wv_ref, wg_ref, bg_ref,
                        q_ref, k_ref, v_ref, xc_ref, z_ref, gates_ref, tail_ref, carry_scr,
                        *, short_seq, steps_per_seq, k_scale):
    rows = x_ref.shape[0]
    inner = z_ref.shape[1]
    hb = _rms_mod(x_ref[...], g_ref[...], _mod_rows(sc_ref, rows), _mod_rows(sh_ref, rows)).astype(BF16)
    if not short_seq:
        @pl.when(pl.program_id(0) % steps_per_seq == 0)
        def _():
            carry_scr[...] = c0_ref[0]

    width = 2 * MXU_DIM

    def conv_qkv(c0, xm):
        cs = slice(c0, c0 + width)
        conv_w = [cw_ref[i:i + 1, cs] for i in range(ML_CONV)]
        conv_b = cb_ref[:, cs]
        if short_seq:
            xm3 = xm.reshape(rows // SUBLANES, SUBLANES, width)
            t_idx = lax.broadcasted_iota(jnp.int32, xm3.shape, 1)
            conv = _conv_taps(xm3, c0_ref[:, :, cs], conv_w, conv_b, t_idx, 1).reshape(rows, width)
            tail_ref[:, :, cs] = pltpu.roll(xm3, ML_CONV - 1, 1)
        else:
            row8 = lax.broadcasted_iota(jnp.int32, (SUBLANES, width), 0)
            conv = _conv_taps(xm, carry_scr[:, cs], conv_w, conv_b, row8, 0)
            tail = pltpu.roll(xm[rows - SUBLANES:], ML_CONV - 1, 0)
            carry_scr[:, cs] = tail
            tail_ref[0, :, cs] = tail
        xc = _silu(conv)
        xc_ref[:, cs] = xc
        xcb = xc.astype(BF16)
        xmb = xm.astype(BF16)
        qkv = []
        for i in range(width // MXU_DIM):
            ls = slice(i * MXU_DIM, (i + 1) * MXU_DIM)
            ti = c0 // MXU_DIM + i
            os_ = slice(ti * MXU_DIM, (ti + 1) * MXU_DIM)
            qk = _dot(xcb[:, ls], wqk_ref[ti])
            qi = qk[:, :MXU_DIM]
            ki = qk[:, MXU_DIM:]
            vi = _dot(xmb[:, ls], wv_ref[ti])
            q_ref[:, os_] = qi.astype(q_ref.dtype)
            k_ref[:, os_] = (ki * k_scale).astype(k_ref.dtype)
            v_ref[:, os_] = vi.astype(v_ref.dtype)
            qkv.append((os_, qi.astype(BF16), ki.astype(BF16), vi.astype(BF16)))
        return qkv

    def gate_logits(acc, qkv):
        for os_, qi, ki, vi in qkv:
            acc = acc + _dot(qi, wg_ref[0, os_, :])
            acc = acc + _dot(ki, wg_ref[1, os_, :])
            acc = acc + _dot(vi, wg_ref[2, os_, :])
        return acc

    gates = bg_ref[...]
    xm_prev = None
    qkv_prev = None
    for c0 in range(0, inner + 2 * width, width):
        xm = None
        if c0 < inner:
            z_ref[:, c0:c0 + width] = _dot(hb, wup_ref[:, inner + c0:inner + c0 + width])
            xm = _dot(hb, wup_ref[:, c0:c0 + width])
        qkv = None
        if xm_prev is not None:
            qkv = conv_qkv(c0 - width, xm_prev)
        if qkv_prev is not None:
            gates = gate_logits(gates, qkv_prev)
        xm_prev, qkv_prev = xm, qkv
    lane = lax.broadcasted_iota(jnp.int32, gates.shape, 1)
    log_sig = jnp.minimum(gates, 0.0) - jnp.log1p(jnp.exp(-jnp.abs(gates)))
    gates_ref[...] = jnp.where(lane < ML_HEADS, gates, log_sig)


def _up_conv_qkv_call(x, g, sc, sh, wup, conv0, conv_w, conv_b, wqk, wv, wg, bg, b, t_len):
    n, d = x.shape
    inner = wup.shape[1] // 2
    short_seq = t_len == SUBLANES
    dh = inner // ML_HEADS
    k_scale = dh ** -0.5
    qkv_dtype = F32 if short_seq else BF16
    halo = jnp.zeros((b, SUBLANES, inner), F32)
    if conv0 is not None:
        halo = halo.at[:, :ML_CONV - 1].set(conv0)
    tm = min(256, n) if short_seq else math.gcd(512, t_len)
    seq_per_step = tm // t_len if short_seq else 1
    steps_per_seq = 1 if short_seq else t_len // tm
    sc_op, sc_spec = _mod_operand(sc, t_len, tm)
    sh_op, sh_spec = _mod_operand(sh, t_len, tm)
    seq_spec = pl.BlockSpec((seq_per_step, SUBLANES, inner), lambda i: (i // steps_per_seq, 0, 0))
    row_spec = lambda w: pl.BlockSpec((tm, w), lambda i: (i, 0))
    const2 = lambda s: pl.BlockSpec(s, lambda i: (0, 0), pipeline_mode=pl.Buffered(1))
    const3 = lambda s: pl.BlockSpec(s, lambda i: (0, 0, 0), pipeline_mode=pl.Buffered(1))
    q, k, v, xc, z, gates, tail = pl.pallas_call(
        functools.partial(_up_conv_qkv_kernel, short_seq=short_seq, steps_per_seq=steps_per_seq,
                          k_scale=k_scale),
        grid=(n // tm,),
        in_specs=[
            row_spec(d), const2((1, d)), sc_spec, sh_spec, const2(wup.shape),
            seq_spec, const2((ML_CONV, inner)), const2((1, inner)),
            const3(wqk.shape), const3(wv.shape), const3(wg.shape), const2((1, LANES)),
        ],
        out_specs=[row_spec(inner), row_spec(inner), row_spec(inner), row_spec(inner), row_spec(inner),
                   row_spec(LANES), seq_spec],
        out_shape=[
            jax.ShapeDtypeStruct((n, inner), qkv_dtype),
            jax.ShapeDtypeStruct((n, inner), qkv_dtype),
            jax.ShapeDtypeStruct((n, inner), qkv_dtype),
            jax.ShapeDtypeStruct((n, inner), F32),
            jax.ShapeDtypeStruct((n, inner), F32),
            jax.ShapeDtypeStruct((n, LANES), F32),
            jax.ShapeDtypeStruct((b, SUBLANES, inner), F32),
        ],
        scratch_shapes=[pltpu.VMEM((SUBLANES, inner), F32)],
        compiler_params=_cparams("arbitrary"),
        name="mlstm_up_conv_qkv",
    )(x, g.reshape(1, d), sc_op, sh_op, wup, halo, conv_w, conv_b.reshape(1, inner), wqk, wv, wg, bg)
    return q, k, v, xc, z, gates, tail[:, :ML_CONV - 1]


def _mlstm_kernel(*refs, has_state):
    if has_state:
        (q_ref, k_ref, v_ref, gates_ref, xc_ref, z_ref, nw_ref, sk_ref, c0_ref, n0_ref, m0_ref,
         o_ref, c_ref, n_ref, m_ref) = refs
    else:
        (q_ref, k_ref, v_ref, gates_ref, xc_ref, z_ref, nw_ref, sk_ref,
         o_ref, c_ref, n_ref, m_ref) = refs
    ch = pl.program_id(1)
    length = q_ref.shape[0]
    dh = q_ref.shape[1] // ML_HEADS

    @pl.when(ch == 0)
    def _():
        if has_state:
            c_ref[...] = c0_ref[...]
            n_ref[...] = n0_ref[...]
            m_ref[...] = m0_ref[...]
        else:
            c_ref[...] = jnp.zeros_like(c_ref)
            n_ref[...] = jnp.zeros_like(n_ref)
            m_ref[...] = jnp.zeros_like(m_ref)

    gates = gates_ref[...]
    gates_t = gates.T
    t_idx = lax.broadcasted_iota(jnp.int32, (length, length), 0)
    s_idx = lax.broadcasted_iota(jnp.int32, (length, length), 1)
    causal = s_idx <= t_idx
    m_all = m_ref[0]
    m_new = m_all
    lane = lax.broadcasted_iota(jnp.int32, m_all.shape, 1)

    def finish_head(h, s_raw, q_c, q_n, w, w_c, m_t):
        hs = slice(h * dh, (h + 1) * dh)
        s = s_raw * w
        num = w_c * q_c + _dot(s.astype(BF16), v_ref[:, hs].astype(BF16))
        den = w_c * q_n + jnp.sum(s, axis=-1, keepdims=True)
        hc = num / jnp.maximum(jnp.abs(den), jnp.exp(-m_t))
        mu = jnp.mean(hc, axis=-1, keepdims=True)
        dev = hc - mu
        var = jnp.mean(dev * dev, axis=-1, keepdims=True)
        hn = dev * lax.rsqrt(var + EPS) * nw_ref[:, hs]
        ho = (hn + sk_ref[:, hs] * xc_ref[:, hs]) * _silu(z_ref[:, hs])
        o_ref[:, hs] = ho.astype(o_ref.dtype)

    pending = None
    for h in range(ML_HEADS):
        hs = slice(h * dh, (h + 1) * dh)
        ig_col = gates[:, h:h + 1]
        lf_col = gates[:, ML_HEADS + h:ML_HEADS + h + 1]
        ig_row = gates_t[h:h + 1, :]
        lf_row = gates_t[ML_HEADS + h:ML_HEADS + h + 1, :]
        b_col = jnp.sum(jnp.where(causal, lf_row, 0.0), axis=1, keepdims=True)
        b_row = jnp.sum(jnp.where(t_idx <= s_idx, lf_col, 0.0), axis=0, keepdims=True)
        m_prev = m_all[:, h:h + 1]
        dm = jnp.where(causal, b_col - b_row + ig_row, -jnp.inf)
        a = b_col + m_prev
        m_t = jnp.maximum(a, jnp.max(dm, axis=1, keepdims=True))
        w_c = jnp.exp(a - m_t)
        w = jnp.exp(dm - m_t)
        qh = q_ref[:, hs].astype(BF16)
        kh = k_ref[:, hs]
        vh = v_ref[:, hs].astype(BF16)
        c_h = c_ref[0, h]
        n_h = n_ref[0, :, hs]
        s_raw = _dot_nt(qh, kh.astype(BF16))
        q_c = _dot(qh, c_h.astype(BF16))
        q_n = jnp.sum(qh.astype(F32) * n_h, axis=-1, keepdims=True)
        m_last = m_t[length - 1:length]
        b_last = b_col[length - 1:length]
        wl_c = jnp.exp(a[length - 1:length] - m_last)
        wl_col = jnp.exp(b_last - b_col + ig_col - m_last)
        kw = kh.astype(F32) * wl_col
        c_ref[0, h] = wl_c * c_h + _dot_tn(kw.astype(BF16), vh)
        n_ref[0, :, hs] = wl_c * n_h + jnp.sum(kw, axis=0, keepdims=True)
        m_new = jnp.where(lane == h, m_last, m_new)
        if pending is not None:
            finish_head(*pending)
        pending = (h, s_raw, q_c, q_n, w, w_c, m_t)
    finish_head(*pending)
    m_ref[0] = m_new


def _mlstm_call(q, k, v, gates, xc, z, norm_w, skip, c0, n0, m0, b, t_len):
    n, inner = q.shape
    dh = inner // ML_HEADS
    has_state = c0 is not None
    length = math.gcd(t_len, ML_CHUNK)
    nc = t_len // length
    row = lambda w: pl.BlockSpec((length, w), lambda i, c: (i * nc + c, 0))
    const = lambda w: pl.BlockSpec((1, w), lambda i, c: (0, 0))
    c_spec = pl.BlockSpec((1, ML_HEADS, dh, dh), lambda i, c: (i, 0, 0, 0))
    n_spec = pl.BlockSpec((1, 1, inner), lambda i, c: (i, 0, 0))
    m_spec = pl.BlockSpec((1, 1, LANES), lambda i, c: (i, 0, 0))
    in_specs = [row(inner), row(inner), row(inner), row(LANES), row(inner), row(inner),
                const(inner), const(inner)]
    args = [q, k, v, gates, xc, z, norm_w.reshape(1, inner), skip.reshape(1, inner)]
    if has_state:
        in_specs += [c_spec, n_spec, m_spec]
        m0_pad = jnp.zeros((b, 1, LANES), F32).at[:, 0, :ML_HEADS].set(m0)
        args += [c0, n0.reshape(b, 1, inner), m0_pad]
    ho, c_t, n_t, m_t = pl.pallas_call(
        functools.partial(_mlstm_kernel, has_state=has_state),
        grid=(b, nc),
        in_specs=in_specs,
        out_specs=[row(inner), c_spec, n_spec, m_spec],
        out_shape=[
            jax.ShapeDtypeStruct((n, inner), BF16 if length % 16 == 0 else F32),
            jax.ShapeDtypeStruct((b, ML_HEADS, dh, dh), F32),
            jax.ShapeDtypeStruct((b, 1, inner), F32),
            jax.ShapeDtypeStruct((b, 1, LANES), F32),
        ],
        compiler_params=_cparams("parallel", "arbitrary"),
        name="mlstm_scan",
    )(*args)
    return ho, c_t, n_t.reshape(b, ML_HEADS, dh), m_t[:, 0, :ML_HEADS]


def _block_diag_tiles(w):
    n_blk, blk, _ = w.shape
    per = MXU_DIM // blk
    rows = w.reshape(n_blk // per, per, blk, blk).transpose(0, 1, 3, 2).reshape(n_blk // per, MXU_DIM, blk)
    r_blk = np.arange(MXU_DIM)[:, None] // blk
    c_blk = np.arange(MXU_DIM)[None, :] // blk
    same_block = jnp.asarray((r_blk == c_blk).astype(np.float32))
    return jnp.tile(rows, (1, 1, per)) * same_block


def _prep_weights(p):
    w = {}
    w['hg_win'] = p['hg_win'].astype(BF16)
    w['hg_wo'] = p['hg_wo'].astype(BF16)
    w['ml_wup'] = p['ml_wup'].astype(BF16)
    w['ml_wdown'] = p['ml_wdown'].astype(BF16)
    dep, n_exp, d_model, dff = p['moe_w1'].shape
    n_stack = dep * n_exp // MOE_EPG
    w['moe_w1'] = p['moe_w1'].reshape(n_stack, MOE_EPG, d_model, dff)
    w['moe_w3'] = p['moe_w3'].reshape(n_stack, MOE_EPG, d_model, dff)
    w['moe_w2'] = p['moe_w2'].reshape(n_stack, MOE_EPG, dff, d_model)
    depth, d, g = p['moe_wrg'].shape
    n_exp = g * p['moe_wre'].shape[-1]
    wr = jnp.zeros((depth, d, LANES), F32)
    wr = wr.at[:, :, :g].set(p['moe_wrg']).at[:, :, g:g + n_exp].set(p['moe_wre'].reshape(depth, d, n_exp))
    w['moe_wr'] = wr
    br = jnp.zeros((depth, 1, LANES), F32)
    br = br.at[:, 0, :g].set(p['moe_brg']).at[:, 0, g:g + n_exp].set(p['moe_bre'].reshape(depth, n_exp))
    w['moe_br'] = br
    n_b = p['ml_wq'].shape[0]
    wq = jnp.stack([_block_diag_tiles(p['ml_wq'][j]) for j in range(n_b)])
    wk = jnp.stack([_block_diag_tiles(p['ml_wk'][j]) for j in range(n_b)])
    wv = jnp.stack([_block_diag_tiles(p['ml_wv'][j]) for j in range(n_b)])
    w['ml_wqk'] = jnp.concatenate([wq, wk], axis=-1).astype(BF16)
    w['ml_wv'] = wv.astype(BF16)
    inner = p['ml_conv_b'].shape[-1]
    wg = jnp.zeros((n_b, 3 * inner, LANES), F32)
    wg = wg.at[:, :, :ML_HEADS].set(p['ml_wig']).at[:, :, ML_HEADS:2 * ML_HEADS].set(p['ml_wfg'])
    w['ml_wg'] = wg.reshape(n_b, 3, inner, LANES).astype(BF16)
    bg = jnp.zeros((n_b, 1, LANES), F32)
    bg = bg.at[:, 0, :ML_HEADS].set(p['ml_big']).at[:, 0, ML_HEADS:2 * ML_HEADS].set(p['ml_bfg'])
    w['ml_bg'] = bg
    return w


def _lb_kernel(lb_ref, o_ref):
    x = lb_ref[...]
    mx = jnp.max(x, axis=0, keepdims=True)
    ex = jnp.exp(x - mx)
    sm = ex / jnp.sum(ex, axis=0, keepdims=True)
    rows = []
    run = jnp.zeros_like(sm[0:1])
    for i in range(x.shape[0]):
        run = run + sm[i:i + 1]
        rows.append(run)
    o_ref[...] = jnp.concatenate(rows, axis=0)


def _lb_call(hg_lb):
    return pl.pallas_call(
        _lb_kernel,
        out_shape=jax.ShapeDtypeStruct(hg_lb.shape, F32),
        name="hgrn_lower_bound",
    )(hg_lb)


def _trunk(x3, mods, s_hg, s_c, s_n, s_m, s_conv, p, w, lb_all):
    b, t_len, d = x3.shape
    n = b * t_len
    x = x3.reshape(n, d)
    depth = p['norm_g'].shape[0]
    new_hg, new_c, new_n, new_m, new_conv = [], [], [], [], []
    for l in range(depth):
        sh1, sc1, g1, sh2, sc2, g2 = [mods[l][:, i * d:(i + 1) * d] for i in range(6)]
        if l % 2 == 0:
            a = l // 2
            proj = _norm_mm_call(x, p['norm_g'][l, 0], sc1, sh1, w['hg_win'][a], t_len, "hgrn_in_proj")
            o, s_t = _gla_call(proj, lb_all[l], p['hg_norm'][a], None if s_hg is None else s_hg[a], b, t_len)
            new_hg.append(s_t)
            mix = (o, w['hg_wo'][a], g1)
        else:
            j = l // 2
            q, k, v, xc, z, gates, conv_tail = _up_conv_qkv_call(
                x, p['norm_g'][l, 0], sc1, sh1, w['ml_wup'][j],
                None if s_conv is None else s_conv[j], p['ml_conv_w'][j], p['ml_conv_b'][j],
                w['ml_wqk'][j], w['ml_wv'][j], w['ml_wg'][j], w['ml_bg'][j], b, t_len)
            ho, c_t, n_t, m_t = _mlstm_call(
                q, k, v, gates, xc, z, p['ml_norm'][j], p['ml_skip'][j],
                None if s_c is None else s_c[j], None if s_n is None else s_n[j],
                None if s_m is None else s_m[j], b, t_len)
            new_c.append(c_t)
            new_n.append(n_t)
            new_m.append(m_t)
            new_conv.append(conv_tail)
            mix = (ho.astype(BF16), w['ml_wdown'][j], g1)
        x = _moe_call(mix, x, p['norm_g'][l, 1], sc2, sh2, g2, w['moe_wr'][l], w['moe_br'][l],
                      w['moe_w1'], w['moe_w3'], w['moe_w2'], l,
                      p['final_g'] if l == depth - 1 else None, t_len, "moe_layer%d" % l)
    return (x.reshape(b, t_len, d), jnp.stack(new_hg), jnp.stack(new_c), jnp.stack(new_n),
            jnp.stack(new_m), jnp.stack(new_conv))


def kernel(x_prompt, x_sample, c_prompt, c_sample, state_hgrn, state_mlstm_c, state_mlstm_n, state_mlstm_m, state_conv, w_ada, b_ada, norm_g, final_g, hg_win, hg_wo, hg_norm, hg_lb, ml_wup, ml_conv_w, ml_conv_b, ml_wq, ml_wk, ml_wv, ml_wig, ml_big, ml_wfg, ml_bfg, ml_norm, ml_skip, ml_wdown, moe_wrg, moe_brg, moe_wre, moe_bre, moe_w1, moe_w3, moe_w2):
    p = dict(w_ada=w_ada, b_ada=b_ada, norm_g=norm_g, final_g=final_g,
             hg_win=hg_win, hg_wo=hg_wo, hg_norm=hg_norm, hg_lb=hg_lb,
             ml_wup=ml_wup, ml_conv_w=ml_conv_w, ml_conv_b=ml_conv_b, ml_wq=ml_wq, ml_wk=ml_wk, ml_wv=ml_wv,
             ml_wig=ml_wig, ml_big=ml_big, ml_wfg=ml_wfg, ml_bfg=ml_bfg, ml_norm=ml_norm, ml_skip=ml_skip,
             ml_wdown=ml_wdown, moe_wrg=moe_wrg, moe_brg=moe_brg, moe_wre=moe_wre, moe_bre=moe_bre,
             moe_w1=moe_w1, moe_w3=moe_w3, moe_w2=moe_w2)
    w = _prep_weights(p)
    lb_all = _lb_call(hg_lb)
    bp = x_prompt.shape[0]
    c_all = jnp.concatenate([c_prompt, c_sample], axis=0)
    mod_all = _ada_call(c_all, w_ada, b_ada)
    mods_p = [mod_all[l, :bp] for l in range(mod_all.shape[0])]
    mods_s = [mod_all[l, bp:] for l in range(mod_all.shape[0])]
    y_p, hg_p, mc_p, mn_p, mm_p, conv_p = _trunk(x_prompt, mods_p, None, None, None, None, None, p, w, lb_all)
    y_s, hg_s, mc_s, mn_s, mm_s, conv_s = _trunk(x_sample, mods_s, state_hgrn, state_mlstm_c, state_mlstm_n,
                                                 state_mlstm_m, state_conv, p, w, lb_all)
    return (y_p, y_s, hg_p, mc_p, mn_p, mm_p, conv_p, hg_s, mc_s, mn_s, mm_s, conv_s)
```
